```python
import math
import jax
import jax.numpy as jnp
from jax import lax
import numpy as np

D_MODEL = 2048
BATCH = 4
SEQ = 2048
DEPTH = 4
DEC_BATCH = 128
DEC_SEQ = 8
PAST_LEN = 16384
PAGE_SIZE = 128

D_MIX = D_MODEL
D_A = D_MIX // 2
D_B = D_MIX - D_A
CHUNK_A = 128
HEAD_A = 128
N_HEADS_A = D_A // HEAD_A
HEAD_B = 128
N_HEADS_B = D_B // HEAD_B
CONV_W = 4
CHUNK_B = 128
D_IN = 2 * D_A + 2 * D_B + 2 * N_HEADS_B
N_GROUPS = 4
EXPERTS_PER_GROUP = 8
N_EXPERTS = N_GROUPS * EXPERTS_PER_GROUP
TOP_K_INNER = 2
D_EXPERT = D_MODEL // 2
MOE_BLOCK = 128
N_MOD = 6
EPS = 1e-6
K_SCALE = 1.0 / math.sqrt(HEAD_B)

kernel_name = 'hymba_gmlp_mlstm_hmoe_adaln_step'


def _rmsnorm(x, g):
    xf = x.astype(jnp.float32)
    y = xf * lax.rsqrt(jnp.mean(xf * xf, axis=-1, keepdims=True) + EPS) * g.astype(jnp.float32)
    return y.astype(x.dtype)


def _layernorm(x, g, b):
    xf = x.astype(jnp.float32)
    xc = xf - jnp.mean(xf, axis=-1, keepdims=True)
    var = jnp.mean(xc * xc, axis=-1, keepdims=True)
    y = xc * lax.rsqrt(var + EPS) * g.astype(jnp.float32) + b.astype(jnp.float32)
    return y.astype(x.dtype)


def _chunk_spatial_gating(u, v, w_s, b_s):
    bsz, t, nh, hd = v.shape
    L = min(t, CHUNK_A)
    pad = (-t) % L
    nc = (t + pad) // L
    vc = jnp.pad(v, ((0, 0), (0, pad), (0, 0), (0, 0))).reshape(bsz, nc, L, nh, hd)
    causal = jnp.tril(jnp.ones((L, L), dtype=bool))
    w = jnp.where(causal[None], w_s[:, :L, :L], jnp.zeros((), w_s.dtype)).astype(v.dtype)
    bias = jnp.transpose(b_s[:, :L]).astype(v.dtype)
    z = jnp.einsum('hts,bnshc->bnthc', w, vc) + bias[None, None, :, :, None]
    z = z.reshape(bsz, nc * L, nh, hd)[:, :t]
    return u * z


def _causal_conv(xm, buf, w, b):
    t = xm.shape[1]
    xp = jnp.concatenate([buf.astype(xm.dtype), xm], axis=1)
    y = b.astype(xm.dtype)
    for j in range(CONV_W):
        y = y + xp[:, j:j + t] * w[j].astype(xm.dtype)
    return y, xp[:, xp.shape[1] - (CONV_W - 1):]


def _mlstm_chunkwise(q, k, v, ig, lf, c0, n0, m0):
    bsz, t, nh, hd = q.shape
    L = math.gcd(t, CHUNK_B)
    nc = t // L
    f32 = jnp.float32

    def chunks(a):
        a = a.astype(f32)
        return jnp.swapaxes(a.reshape((bsz, nc, L) + a.shape[2:]), 0, 1)

    causal = jnp.tril(jnp.ones((L, L), dtype=bool))[None, :, :, None]

    def step(carry, inp):
        c, n, m = carry
        qc, kc, vc, ic, fc = inp
        b = jnp.cumsum(fc, axis=1)
        log_inter = b + m[:, None, :]
        d = b[:, :, None, :] - b[:, None, :, :] + ic[:, None, :, :]
        d = jnp.where(causal, d, -jnp.inf)
        mt = jnp.maximum(log_inter, jnp.max(d, axis=2))
        w_inter = jnp.exp(log_inter - mt)
        w_intra = jnp.exp(d - mt[:, :, None, :])
        s = jnp.einsum('bthd,bshd->btsh', qc, kc) * w_intra
        num = w_inter[..., None] * jnp.einsum('bthk,bhkv->bthv', qc, c) + jnp.einsum('btsh,bshv->bthv', s, vc)
        den = w_inter * jnp.einsum('bthk,bhk->bth', qc, n) + jnp.sum(s, axis=2)
        h = num / jnp.maximum(jnp.abs(den), jnp.exp(-mt))[..., None]
        w_last = w_intra[:, -1]
        decay = w_inter[:, -1]
        c_new = decay[:, :, None, None] * c + jnp.einsum('bsh,bshk,bshv->bhkv', w_last, kc, vc)
        n_new = decay[:, :, None] * n + jnp.einsum('bsh,bshk->bhk', w_last, kc)
        return (c_new, n_new, mt[:, -1]), h

    carry0 = (c0.astype(f32), n0.astype(f32), m0.astype(f32))
    (c, n, m), hs = lax.scan(step, carry0, (chunks(q), chunks(k), chunks(v), chunks(ig), chunks(lf)))
    h = jnp.swapaxes(hs, 0, 1).reshape(bsz, t, nh, hd)
    return h, c, n, m


def _dispatch_experts(hf, expert, gates, w_eg, w_eu, w_ed):
    n_tok = hf.shape[0]
    n_asg = n_tok * TOP_K_INNER
    e_flat = expert.reshape(-1)
    tok_flat = jnp.repeat(jnp.arange(n_tok, dtype=jnp.int32), TOP_K_INNER)
    g_flat = gates.reshape(-1)
    order = jnp.argsort(e_flat)
    e_sorted = e_flat[order]
    tok_sorted = tok_flat[order]
    counts = jax.ops.segment_sum(jnp.ones_like(e_flat), e_flat, num_segments=N_EXPERTS)
    padded = (counts + MOE_BLOCK - 1) // MOE_BLOCK * MOE_BLOCK
    start = jnp.cumsum(counts) - counts
    pend = jnp.cumsum(padded)
    pstart = pend - padded
    dest = pstart[e_sorted] + jnp.arange(n_asg, dtype=jnp.int32) - start[e_sorted]
    n_rows = (n_asg + N_EXPERTS * (MOE_BLOCK - 1) + MOE_BLOCK - 1) // MOE_BLOCK * MOE_BLOCK
    n_blocks = n_rows // MOE_BLOCK
    x_buf = jnp.zeros((n_rows, D_MODEL), hf.dtype).at[dest].set(hf[tok_sorted])
    block_start = jnp.arange(n_blocks, dtype=jnp.int32) * MOE_BLOCK
    block_expert = jnp.minimum(jnp.searchsorted(pend, block_start, side='right'), N_EXPERTS - 1).astype(jnp.int32)

    def expert_block(args):
        xb, e = args
        gb = xb @ w_eg[e]
        ub = xb @ w_eu[e]
        return (jax.nn.silu(gb) * ub) @ w_ed[e]

    y_buf = lax.map(expert_block, (x_buf.reshape(n_blocks, MOE_BLOCK, D_MODEL), block_expert))
    y_buf = y_buf.reshape(n_rows, D_MODEL)
    y_sorted = y_buf[dest].astype(jnp.float32) * g_flat[order][:, None]
    return jax.ops.segment_sum(y_sorted, tok_sorted, num_segments=n_tok).astype(hf.dtype)


def _hier_moe(h, w_r1, b_r1, w_r2, b_r2, w_eg, w_eu, w_ed):
    shp = h.shape
    hf = h.reshape(-1, D_MODEL)
    n_tok = hf.shape[0]
    hr = hf.astype(jnp.float32)
    logit_g = hr @ w_r1.astype(jnp.float32) + b_r1.astype(jnp.float32)
    p_g = jax.nn.softmax(logit_g, axis=-1)
    g_sel = jnp.argmax(logit_g, axis=-1).astype(jnp.int32)
    pg_sel = jnp.take_along_axis(p_g, g_sel[:, None], axis=-1)
    logit_all = jnp.einsum('nd,gde->nge', hr, w_r2.astype(jnp.float32)) + b_r2.astype(jnp.float32)
    logit_e = logit_all[jnp.arange(n_tok), g_sel]
    p_e = jax.nn.softmax(logit_e, axis=-1)
    top_p, top_i = lax.top_k(p_e, TOP_K_INNER)
    gates = pg_sel * top_p / jnp.sum(top_p, axis=-1, keepdims=True)
    expert = g_sel[:, None] * EXPERTS_PER_GROUP + top_i.astype(jnp.int32)
    y = _dispatch_experts(hf, expert, gates, w_eg, w_eu, w_ed)
    return y.reshape(shp)


def _layer(x, c, conv_buf, c_state, n_state, m_state, p):
    bsz, t, _ = x.shape
    mod = (jax.nn.silu(c) @ p['w_ada'] + p['b_ada']).reshape(bsz, N_MOD, 1, D_MODEL)
    shift1, scale1, gate1, shift2, scale2, gate2 = (mod[:, i] for i in range(N_MOD))

    h = _rmsnorm(x, p['g_norm1']) * (1 + scale1) + shift1
    proj = h @ p['w_in']
    u_pre, v_pre, xm, zo, i_pre, f_pre = jnp.split(
        proj, [D_A, 2 * D_A, 2 * D_A + D_B, 2 * D_A + 2 * D_B, 2 * D_A + 2 * D_B + N_HEADS_B], axis=-1)

    u = jax.nn.gelu(u_pre).reshape(bsz, t, N_HEADS_A, HEAD_A)
    v = _layernorm(jax.nn.gelu(v_pre), p['ln_v_g'], p['ln_v_b'])
    y_a = _chunk_spatial_gating(u, v.reshape(bsz, t, N_HEADS_A, HEAD_A), p['w_spatial'], p['b_spatial'])
    y_a = y_a.reshape(bsz, t, D_A)

    xc, new_buf = _causal_conv(xm, conv_buf, p['w_conv'], p['b_conv'])
    xc = jax.nn.silu(xc).reshape(bsz, t, N_HEADS_B, HEAD_B)
    xmh = xm.reshape(bsz, t, N_HEADS_B, HEAD_B)
    q = jnp.einsum('bthd,hde->bthe', xc, p['w_q'])
    k = jnp.einsum('bthd,hde->bthe', xc, p['w_k']) * K_SCALE
    vb = jnp.einsum('bthd,hde->bthe', xmh, p['w_v'])
    ig = (i_pre + p['b_i']).astype(jnp.float32)
    lf = jax.nn.log_sigmoid((f_pre + p['b_f']).astype(jnp.float32))
    hb, c_new, n_new, m_new = _mlstm_chunkwise(q, k, vb, ig, lf, c_state, n_state, m_state)
    hb = hb * lax.rsqrt(jnp.mean(hb * hb, axis=-1, keepdims=True) + EPS)
    hb = hb * p['g_mh'].astype(jnp.float32).reshape(N_HEADS_B, HEAD_B)
    y_b = jax.nn.sigmoid(zo) * hb.reshape(bsz, t, D_B).astype(x.dtype)

    mix = jnp.concatenate([y_a, y_b], axis=-1) @ p['w_out']
    x = x + gate1 * mix

    h2 = _rmsnorm(x, p['g_norm2']) * (1 + scale2) + shift2
    x = x + gate2 * _hier_moe(h2, p['w_r1'], p['b_r1'], p['w_r2'], p['b_r2'],
                              p['w_e_gate'], p['w_e_up'], p['w_e_down'])
    return x, new_buf, c_new, n_new, m_new, v


def setup_inputs(seed: int = 0) -> dict:
    key = jax.random.key(seed)
    k = jax.random.split(key, 34)

    def nrm(kk, shape, scale=1.0):
        return jax.random.normal(kk, shape, jnp.float32) * scale

    b_f = jnp.tile(jnp.linspace(3.0, 6.0, N_HEADS_B, dtype=jnp.float32)[None], (DEPTH, 1)) + nrm(k[23], (DEPTH, N_HEADS_B), 0.02)
    return {
        'x_prompt': nrm(k[0], (BATCH, SEQ, D_MODEL)),
        'x_sample': nrm(k[1], (DEC_BATCH, DEC_SEQ, D_MODEL)),
        'c_prompt': nrm(k[2], (BATCH, D_MODEL)),
        'c_sample': nrm(k[3], (DEC_BATCH, D_MODEL)),
        'state_mlstm_C': nrm(k[4], (DEPTH, DEC_BATCH, N_HEADS_B, HEAD_B, HEAD_B)),
        'state_mlstm_n': nrm(k[5], (DEPTH, DEC_BATCH, N_HEADS_B, HEAD_B)),
        'state_mlstm_m': nrm(k[6], (DEPTH, DEC_BATCH, N_HEADS_B)),
        'state_conv': nrm(k[7], (DEPTH, DEC_BATCH, CONV_W - 1, D_B)),
        'w_ada': nrm(k[8], (DEPTH, D_MODEL, N_MOD * D_MODEL), 0.5 * D_MODEL ** -0.5),
        'b_ada': nrm(k[9], (DEPTH, N_MOD * D_MODEL), 0.02),
        'g_norm1': 1.0 + nrm(k[10], (DEPTH, D_MODEL), 0.02),
        'g_norm2': 1.0 + nrm(k[11], (DEPTH, D_MODEL), 0.02),
        'w_in': nrm(k[12], (DEPTH, D_MODEL, D_IN), D_MODEL ** -0.5),
        'ln_v_g': 1.0 + nrm(k[13], (DEPTH, D_A), 0.02),
        'ln_v_b': nrm(k[14], (DEPTH, D_A), 0.02),
        'w_spatial': nrm(k[15], (DEPTH, N_HEADS_A, CHUNK_A, CHUNK_A), CHUNK_A ** -0.5),
        'b_spatial': 1.0 + nrm(k[16], (DEPTH, N_HEADS_A, CHUNK_A), 0.02),
        'w_conv': nrm(k[17], (DEPTH, CONV_W, D_B), CONV_W ** -0.5),
        'b_conv': nrm(k[18], (DEPTH, D_B), 0.02),
        'w_q': nrm(k[19], (DEPTH, N_HEADS_B, HEAD_B, HEAD_B), HEAD_B ** -0.5),
        'w_k': nrm(k[20], (DEPTH, N_HEADS_B, HEAD_B, HEAD_B), HEAD_B ** -0.5),
        'w_v': nrm(k[21], (DEPTH, N_HEADS_B, HEAD_B, HEAD_B), HEAD_B ** -0.5),
        'b_i': nrm(k[22], (DEPTH, N_HEADS_B), 0.1),
        'b_f': b_f,
        'g_mh': 1.0 + nrm(k[24], (DEPTH, D_B), 0.02),
        'w_out': nrm(k[25], (DEPTH, D_MIX, D_MODEL), D_MIX ** -0.5),
        'w_r1': nrm(k[26], (DEPTH, D_MODEL, N_GROUPS), D_MODEL ** -0.5),
        'b_r1': nrm(k[27], (DEPTH, N_GROUPS), 0.01),
        'w_r2': nrm(k[28], (DEPTH, N_GROUPS, D_MODEL, EXPERTS_PER_GROUP), D_MODEL ** -0.5),
        'b_r2': nrm(k[29], (DEPTH, N_GROUPS, EXPERTS_PER_GROUP), 0.01),
        'w_e_gate': nrm(k[30], (DEPTH, N_EXPERTS, D_MODEL, D_EXPERT), D_MODEL ** -0.5),
        'w_e_up': nrm(k[31], (DEPTH, N_EXPERTS, D_MODEL, D_EXPERT), D_MODEL ** -0.5),
        'w_e_down': nrm(k[32], (DEPTH, N_EXPERTS, D_EXPERT, D_MODEL), D_EXPERT ** -0.5),
        'g_final': 1.0 + nrm(k[33], (D_MODEL,), 0.02),
    }


def reference(x_prompt, x_sample, c_prompt, c_sample, state_mlstm_C, state_mlstm_n, state_mlstm_m, state_conv,
              w_ada, b_ada, g_norm1, g_norm2, w_in, ln_v_g, ln_v_b, w_spatial, b_spatial, w_conv, b_conv,
              w_q, w_k, w_v, b_i, b_f, g_mh, w_out, w_r1, b_r1, w_r2, b_r2, w_e_gate, w_e_up, w_e_down, g_final):
    bp = x_prompt.shape[0]
    conv0 = jnp.zeros((bp, CONV_W - 1, D_B), x_prompt.dtype)
    c0 = jnp.zeros((bp, N_HEADS_B, HEAD_B, HEAD_B), jnp.float32)
    n0 = jnp.zeros((bp, N_HEADS_B, HEAD_B), jnp.float32)
    m0 = jnp.zeros((bp, N_HEADS_B), jnp.float32)
    xp, xs = x_prompt, x_sample
    cp_l, np_l, mp_l, bp_l = [], [], [], []
    cs_l, ns_l, ms_l, bs_l, vs_l = [], [], [], [], []
    for l in range(DEPTH):
        p = dict(w_ada=w_ada[l], b_ada=b_ada[l], g_norm1=g_norm1[l], g_norm2=g_norm2[l], w_in=w_in[l],
                 ln_v_g=ln_v_g[l], ln_v_b=ln_v_b[l], w_spatial=w_spatial[l], b_spatial=b_spatial[l],
                 w_conv=w_conv[l], b_conv=b_conv[l], w_q=w_q[l], w_k=w_k[l], w_v=w_v[l], b_i=b_i[l], b_f=b_f[l],
                 g_mh=g_mh[l], w_out=w_out[l], w_r1=w_r1[l], b_r1=b_r1[l], w_r2=w_r2[l], b_r2=b_r2[l],
                 w_e_gate=w_e_gate[l], w_e_up=w_e_up[l], w_e_down=w_e_down[l])
        xp, bufp, cp, np_, mp, _ = _layer(xp, c_prompt, conv0, c0, n0, m0, p)
        xs, bufs, cs, ns, ms, vs = _layer(xs, c_sample, state_conv[l], state_mlstm_C[l], state_mlstm_n[l],
                                          state_mlstm_m[l], p)
        cp_l.append(cp); np_l.append(np_); mp_l.append(mp); bp_l.append(bufp)
        cs_l.append(cs); ns_l.append(ns); ms_l.append(ms); bs_l.append(bufs); vs_l.append(vs)
    y_prompt = _rmsnorm(xp, g_final)
    y_sample = _rmsnorm(xs, g_final)
    new_C_prompt = jnp.stack(cp_l)
    new_n_prompt = jnp.stack(np_l)
    new_m_prompt = jnp.stack(mp_l)
    new_conv_prompt = jnp.stack(bp_l)
    new_C_sample = jnp.stack(cs_l)
    new_n_sample = jnp.stack(ns_l)
    new_m_sample = jnp.stack(ms_l)
    new_conv_sample = jnp.stack(bs_l)
    new_chunk_v_sample = jnp.stack(vs_l)
    return (y_prompt, y_sample, new_C_prompt, new_n_prompt, new_m_prompt, new_conv_prompt,
            new_C_sample, new_n_sample, new_m_sample, new_conv_sample, new_chunk_v_sample)
```

```python
import functools
import math

import jax
import jax.numpy as jnp
from jax import lax
from jax.experimental import pallas as pl
from jax.experimental.pallas import tpu as pltpu

F32 = jnp.float32
BF16 = jnp.bfloat16
EPS = 1e-6
LANES = 128
SUBLANES = 8
TILE = 128
CONV_W = 4
TOP_K = 2
VMEM_LIMIT_BYTES = 56 * 1024 * 1024
MOE_BLOCK_ROWS = 256
GATHER_CHUNK = 512
NEG_INF = float("-inf")


def _cparams(sem):
    return pltpu.CompilerParams(dimension_semantics=sem, vmem_limit_bytes=VMEM_LIMIT_BYTES)


def _split3(x):
    x1 = x.astype(BF16)
    r1 = x - x1.astype(F32)
    x2 = r1.astype(BF16)
    r2 = r1 - x2.astype(F32)
    return x1, x2, r2.astype(BF16)


def _ada_kernel(c_ref, w_ref, b_ref, o_ref):
    c = c_ref[...]
    a = c * jax.nn.sigmoid(c)
    o_ref[0] = jnp.dot(a.astype(BF16), w_ref[0].astype(BF16), preferred_element_type=F32) + b_ref[0]


def _ada(c_all, w_ada, b_ada):
    depth, d, n = w_ada.shape
    bc = c_all.shape[0]
    tn = min(1024, n)
    return pl.pallas_call(
        _ada_kernel,
        grid=(depth, n // tn),
        in_specs=[pl.BlockSpec((bc, d), lambda l, j: (0, 0)),
                  pl.BlockSpec((1, d, tn), lambda l, j: (l, 0, j)),
                  pl.BlockSpec((1, 1, tn), lambda l, j: (l, 0, j))],
        out_specs=pl.BlockSpec((1, bc, tn), lambda l, j: (l, 0, j)),
        out_shape=jax.ShapeDtypeStruct((depth, bc, n), F32),
        compiler_params=_cparams(("arbitrary", "arbitrary")),
        name="ada",
    )(c_all, w_ada, b_ada.reshape(depth, 1, n))


def _inproj_kernel(x_ref, sc_ref, sh_ref, g_ref, w_ref, wg_ref, o_ref, og_ref, hn_ref):
    @pl.when(pl.program_id(2) == 0)
    def _():
        x = x_ref[...]
        ms = jnp.mean(x * x, axis=-1, keepdims=True)
        h = x * lax.rsqrt(ms + EPS) * g_ref[...]
        h = h * (1.0 + sc_ref[...]) + sh_ref[...]
        hb = h.reshape(hn_ref.shape).astype(BF16)
        hn_ref[...] = hb
        og_ref[...] = jnp.dot(hb, wg_ref[...], preferred_element_type=F32)

    o_ref[...] = jnp.dot(hn_ref[...], w_ref[...], preferred_element_type=F32)


def _row_tiling(b, t, max_rows):
    if t >= max_rows:
        assert t % max_rows == 0
        return 1, max_rows
    bb = min(b, max_rows // t)
    assert b % bb == 0 and t % SUBLANES == 0
    return bb, t


def _inproj(x, scale, shift, g_norm, w_main, w_gate, max_rows):
    b, t, d = x.shape
    n = w_main.shape[1]
    ng = w_gate.shape[1]
    bb, tt = _row_tiling(b, t, max_rows)
    rows = bb * tt
    tn = min(512, n)
    nt = t // tt
    return pl.pallas_call(
        _inproj_kernel,
        grid=(b // bb, nt, n // tn),
        in_specs=[pl.BlockSpec((bb, tt, d), lambda i, k, j: (i, k, 0)),
                  pl.BlockSpec((bb, 1, d), lambda i, k, j: (i, 0, 0)),
                  pl.BlockSpec((bb, 1, d), lambda i, k, j: (i, 0, 0)),
                  pl.BlockSpec((1, 1, d), lambda i, k, j: (0, 0, 0)),
                  pl.BlockSpec((d, tn), lambda i, k, j: (0, j)),
                  pl.BlockSpec((d, ng), lambda i, k, j: (0, 0))],
        out_specs=[pl.BlockSpec((rows, tn), lambda i, k, j: (i * nt + k, j)),
                   pl.BlockSpec((rows, ng), lambda i, k, j: (i * nt + k, 0))],
        out_shape=[jax.ShapeDtypeStruct((b * t, n), F32), jax.ShapeDtypeStruct((b * t, ng), F32)],
        scratch_shapes=[pltpu.VMEM((rows, d), BF16)],
        compiler_params=_cparams(("arbitrary", "arbitrary", "arbitrary")),
        name="inproj",
    )(x, scale, shift, g_norm.reshape(1, 1, d), w_main, w_gate)


def _grpa_kernel(u_ref, v_ref, lg_ref, lb_ref, w_ref, bias_ref, ya_ref, *maybe_v_out):
    rows, da = u_ref.shape
    nh = w_ref.shape[0]
    g = jax.nn.gelu(v_ref[...])
    mu = jnp.mean(g, axis=-1, keepdims=True)
    gc = g - mu
    var = jnp.mean(gc * gc, axis=-1, keepdims=True)
    v = gc * lax.rsqrt(var + EPS) * lg_ref[...] + lb_ref[...]
    if maybe_v_out:
        maybe_v_out[0][...] = v
    vb = v.astype(BF16)
    for c in range(rows // TILE):
        r0 = c * TILE
        for h in range(nh):
            c0 = h * LANES
            z = jnp.dot(w_ref[h], vb[r0:r0 + TILE, c0:c0 + LANES], preferred_element_type=F32)
            u = jax.nn.gelu(u_ref[r0:r0 + TILE, c0:c0 + LANES])
            y = u * (z + bias_ref[:, c0:c0 + LANES])
            ya_ref[r0:r0 + TILE, c0:c0 + LANES] = y.astype(BF16)


def _grpa(proj, ln_g, ln_b, w_mix, bias_full, da, want_v):
    n_rows = proj.shape[0]
    rows = min(512, n_rows)
    nh = w_mix.shape[0]
    out_specs = [pl.BlockSpec((rows, da), lambda i: (i, 0))]
    out_shape = [jax.ShapeDtypeStruct((n_rows, da), BF16)]
    if want_v:
        out_specs.append(pl.BlockSpec((rows, da), lambda i: (i, 0)))
        out_shape.append(jax.ShapeDtypeStruct((n_rows, da), F32))
    return pl.pallas_call(
        _grpa_kernel,
        grid=(n_rows // rows,),
        in_specs=[pl.BlockSpec((rows, da), lambda i: (i, 0)),
                  pl.BlockSpec((rows, da), lambda i: (i, 1)),
                  pl.BlockSpec((1, da), lambda i: (0, 0)),
                  pl.BlockSpec((1, da), lambda i: (0, 0)),
                  pl.BlockSpec((nh, TILE, TILE), lambda i: (0, 0, 0)),
                  pl.BlockSpec((TILE, da), lambda i: (0, 0))],
        out_specs=out_specs,
        out_shape=out_shape,
        compiler_params=_cparams(("arbitrary",)),
        name="grpa",
    )(proj, proj, ln_g.reshape(1, da), ln_b.reshape(1, da), w_mix, bias_full)


def _group_last(x, L):
    if L == TILE:
        return jnp.broadcast_to(x[TILE - 1:TILE, :], x.shape)
    g = TILE // L
    x3 = x.reshape(g, L, x.shape[1])
    return jnp.broadcast_to(x3[:, L - 1:L, :], x3.shape).reshape(x.shape)


def _group_sum(x, L):
    if L == TILE:
        return jnp.broadcast_to(jnp.sum(x, axis=0, keepdims=True), x.shape)
    g = TILE // L
    x3 = x.reshape(g, L, x.shape[1])
    return jnp.broadcast_to(jnp.sum(x3, axis=1, keepdims=True), x3.shape).reshape(x.shape)


def _mlstm_kernel(*refs, L, carry, nh, k_scale):
    if carry:
        (xm_ref, zo_ref, g_ref, wconv_ref, bconv_ref, wq_ref, wk_ref, wkt_ref, wv_ref, bi_ref, bf_ref, gmh_ref,
         yb_ref, c_ref, n_ref, m_ref, prev_scr) = refs
        c_in_ref = c_ref
        first = pl.program_id(1) == 0

        @pl.when(first)
        def _():
            c_ref[...] = jnp.zeros(c_ref.shape, F32)
            n_ref[...] = jnp.zeros(n_ref.shape, F32)
            m_ref[...] = jnp.zeros(m_ref.shape, F32)
            prev_scr[...] = jnp.zeros(prev_scr.shape, F32)

        prev = prev_scr[...]
        n0_rows = jnp.broadcast_to(n_ref[0], (TILE, n_ref.shape[2]))
        m0 = jnp.broadcast_to(m_ref[0], (TILE, LANES))
    else:
        (xm_ref, zo_ref, g_ref, prev_ref, c_in_ref, n0_ref, m0_ref,
         wconv_ref, bconv_ref, wq_ref, wk_ref, wkt_ref, wv_ref, bi_ref, bf_ref, gmh_ref,
         yb_ref, c_ref, n_ref, m_ref, q_scr, qc_scr, dec_scr) = refs
        prev = prev_ref[...]
        n0_rows = n0_ref[...]
        m0 = m0_ref[...]

    groups = TILE // L
    x = xm_ref[...]
    row_in_group = lax.broadcasted_iota(jnp.int32, (TILE, 1), 0) % L

    acc = bconv_ref[...] + wconv_ref[CONV_W - 1:CONV_W, :] * x
    for j in range(1, CONV_W):
        from_prev = pltpu.roll(prev, (j - L) % TILE, 0)
        shifted = jnp.where(row_in_group < j, from_prev, pltpu.roll(x, j, 0))
        acc = acc + wconv_ref[CONV_W - 1 - j:CONV_W - j, :] * shifted
    xc = acc * jax.nn.sigmoid(acc)
    if carry:
        prev_scr[...] = x

    gates = g_ref[...]
    gi = gates[:, :LANES] + bi_ref[...]
    gf = gates[:, LANES:] + bf_ref[...]
    lf = jnp.minimum(gf, 0.0) - jnp.log1p(jnp.exp(-jnp.abs(gf)))

    r_i = lax.broadcasted_iota(jnp.int32, (TILE, TILE), 0)
    c_i = lax.broadcasted_iota(jnp.int32, (TILE, TILE), 1)
    mask = (c_i <= r_i) & ((c_i // L) == (r_i // L)) if L != TILE else (c_i <= r_i)
    mask_b = jnp.where(mask, 1.0, 0.0).astype(BF16)
    lf1, lf2, lf3 = _split3(lf)
    cum = (jnp.dot(mask_b, lf1, preferred_element_type=F32) + jnp.dot(mask_b, lf2, preferred_element_type=F32)
           + jnp.dot(mask_b, lf3, preferred_element_type=F32))
    cum_t = cum.T
    gi_t = gi.T
    log_inter = cum + m0
    lane = lax.broadcasted_iota(jnp.int32, (TILE, LANES), 1)

    def d_matrix(h):
        d = cum[:, h:h + 1] - cum_t[h:h + 1, :] + gi_t[h:h + 1, :]
        return jnp.where(mask, d, NEG_INF)

    mt = jnp.zeros((TILE, LANES), F32)
    for h in range(nh):
        mt_h = jnp.maximum(log_inter[:, h:h + 1], jnp.max(d_matrix(h), axis=1, keepdims=True))
        mt = jnp.where(lane == h, mt_h, mt)

    m_last = _group_last(mt, L)
    w_inter = jnp.exp(log_inter - mt)
    w_last = jnp.exp(_group_last(cum, L) - cum + gi - m_last)
    w_last_t = w_last.T
    inv_floor = jnp.exp(-mt)

    if not carry:
        dec_scr[...] = w_inter

    for h in range(nh):
        c0 = h * LANES
        xc_h = xc[:, c0:c0 + LANES].astype(BF16)
        q = jnp.dot(xc_h, wq_ref[h], preferred_element_type=F32)
        k = jnp.dot(xc_h, wk_ref[h], preferred_element_type=F32) * k_scale
        k_t = lax.dot_general(wkt_ref[h], xc_h, (((1,), (1,)), ((), ())), preferred_element_type=F32) * k_scale
        v = jnp.dot(x[:, c0:c0 + LANES].astype(BF16), wv_ref[h], preferred_element_type=F32)
        qb = q.astype(BF16)
        vb = v.astype(BF16)
        mt_h = mt[:, h:h + 1]
        wi_h = w_inter[:, h:h + 1]
        w_intra = jnp.exp(d_matrix(h) - mt_h)
        s = jnp.dot(qb, k_t.astype(BF16), preferred_element_type=F32) * w_intra

        if carry:
            qc = jnp.dot(qb, c_in_ref[0, h].astype(BF16), preferred_element_type=F32)
        else:
            q_scr[...] = q

            def qc_body(g, _, h=h):
                r0 = pl.multiple_of(g * L, L)
                qg = q_scr[pl.ds(r0, L), :].astype(BF16)
                qc_scr[pl.ds(r0, L), :] = jnp.dot(qg, c_in_ref[g, h].astype(BF16), preferred_element_type=F32)
                return 0

            lax.fori_loop(0, groups, qc_body, 0)
            qc = qc_scr[...]

        num = wi_h * qc + jnp.dot(s.astype(BF16), vb, preferred_element_type=F32)
        qn = jnp.sum(q * n0_rows[:, c0:c0 + LANES], axis=1, keepdims=True)
        den = wi_h * qn + jnp.sum(s, axis=1, keepdims=True)
        hh = num / jnp.maximum(jnp.abs(den), inv_floor[:, h:h + 1])
        hh = hh * lax.rsqrt(jnp.mean(hh * hh, axis=1, keepdims=True) + EPS) * gmh_ref[:, c0:c0 + LANES]
        yb_ref[:, c0:c0 + LANES] = (jax.nn.sigmoid(zo_ref[:, c0:c0 + LANES]) * hh).astype(BF16)

        kw_t = k_t * w_last_t[h:h + 1, :]
        n_rows = wi_h * n0_rows[:, c0:c0 + LANES] + _group_sum(w_last[:, h:h + 1] * k, L)
        if carry:
            decay = jnp.broadcast_to(w_inter[TILE - 1:TILE, h:h + 1], (LANES, LANES))
            c_ref[0, h] = decay * c_in_ref[0, h] + jnp.dot(kw_t.astype(BF16), vb, preferred_element_type=F32)
            n_ref[0, :, c0:c0 + LANES] = n_rows[TILE - 1:TILE, :]
        else:
            n_ref[:, c0:c0 + LANES] = n_rows

            def c_body(g, _, h=h, kw_t=kw_t, vb=vb):
                kw_g = jnp.where((c_i // L) == g, kw_t, 0.0).astype(BF16)
                upd = jnp.dot(kw_g, vb, preferred_element_type=F32)
                drow = dec_scr[pl.ds(g * L + L - 1, 1), :]
                decay = jnp.broadcast_to(drow[:, h:h + 1], (LANES, LANES))
                c_ref[g, h] = decay * c_in_ref[g, h] + upd
                return 0

            lax.fori_loop(0, groups, c_body, 0)

    if carry:
        m_ref[0] = mt[TILE - 1:TILE, :]
    else:
        m_ref[...] = mt


def _mlstm_weights(p):
    return [p["w_conv"], p["b_conv"], p["w_q"], p["w_k"], p["w_kt"], p["w_v"], p["b_i"], p["b_f"], p["g_mh"]]


def _mlstm_weight_specs(nh, db, nidx):
    z2 = (lambda i, c: (0, 0)) if nidx == 2 else (lambda i: (0, 0))
    z3 = (lambda i, c: (0, 0, 0)) if nidx == 2 else (lambda i: (0, 0, 0))
    hw = pl.BlockSpec((nh, LANES, LANES), z3)
    return [pl.BlockSpec((CONV_W, db), z2), pl.BlockSpec((1, db), z2), hw, hw, hw, hw,
            pl.BlockSpec((1, LANES), z2), pl.BlockSpec((1, LANES), z2), pl.BlockSpec((1, db), z2)]


def _mlstm_prompt(proj, gates, p, b, t, db, nh, xm_col, zo_col):
    nt = t // TILE
    kern = functools.partial(_mlstm_kernel, L=TILE, carry=True, nh=nh, k_scale=p["k_scale"])
    return pl.pallas_call(
        kern,
        grid=(b, nt),
        in_specs=[pl.BlockSpec((TILE, db), lambda i, c: (i * nt + c, xm_col)),
                  pl.BlockSpec((TILE, db), lambda i, c: (i * nt + c, zo_col)),
                  pl.BlockSpec((TILE, 2 * LANES), lambda i, c: (i * nt + c, 0))] + _mlstm_weight_specs(nh, db, 2),
        out_specs=[pl.BlockSpec((TILE, db), lambda i, c: (i * nt + c, 0)),
                   pl.BlockSpec((1, nh, LANES, LANES), lambda i, c: (i, 0, 0, 0)),
                   pl.BlockSpec((1, 1, db), lambda i, c: (i, 0, 0)),
                   pl.BlockSpec((1, 1, LANES), lambda i, c: (i, 0, 0))],
        out_shape=[jax.ShapeDtypeStruct((b * t, db), BF16),
                   jax.ShapeDtypeStruct((b, nh, LANES, LANES), F32),
                   jax.ShapeDtypeStruct((b, 1, db), F32),
                   jax.ShapeDtypeStruct((b, 1, LANES), F32)],
        scratch_shapes=[pltpu.VMEM((TILE, db), F32)],
        compiler_params=_cparams(("arbitrary", "arbitrary")),
        name="mlstm_prompt",
    )(proj, proj, gates, *_mlstm_weights(p))


def _mlstm_sample(proj, gates, prev_rows, c0, n0_rows, m0_rows, p, L, db, nh, xm_col, zo_col):
    n_rows = proj.shape[0]
    groups = TILE // L
    kern = functools.partial(_mlstm_kernel, L=L, carry=False, nh=nh, k_scale=p["k_scale"])
    row_spec = pl.BlockSpec((TILE, db), lambda i: (i, 0))
    return pl.pallas_call(
        kern,
        grid=(n_rows // TILE,),
        in_specs=[pl.BlockSpec((TILE, db), lambda i: (i, xm_col)),
                  pl.BlockSpec((TILE, db), lambda i: (i, zo_col)),
                  pl.BlockSpec((TILE, 2 * LANES), lambda i: (i, 0)),
                  row_spec,
                  pl.BlockSpec((groups, nh, LANES, LANES), lambda i: (i, 0, 0, 0)),
                  row_spec,
                  pl.BlockSpec((TILE, LANES), lambda i: (i, 0))] + _mlstm_weight_specs(nh, db, 1),
        out_specs=[row_spec,
                   pl.BlockSpec((groups, nh, LANES, LANES), lambda i: (i, 0, 0, 0)),
                   row_spec,
                   pl.BlockSpec((TILE, LANES), lambda i: (i, 0))],
        out_shape=[jax.ShapeDtypeStruct((n_rows, db), BF16),
                   jax.ShapeDtypeStruct(c0.shape, F32),
                   jax.ShapeDtypeStruct((n_rows, db), F32),
                   jax.ShapeDtypeStruct((n_rows, LANES), F32)],
        scratch_shapes=[pltpu.VMEM((TILE, LANES), F32), pltpu.VMEM((TILE, LANES), F32),
                        pltpu.VMEM((TILE, LANES), F32)],
        compiler_params=_cparams(("arbitrary",)),
        name="mlstm_sample",
    )(proj, proj, gates, prev_rows, c0, n0_rows, m0_rows, *_mlstm_weights(p))


def _route(lg, n_groups, per_group):
    lane_i = lax.broadcasted_iota(jnp.int32, lg.shape, 1)
    lane = lane_i.astype(F32)
    far = float(LANES)
    is_g = lane_i < n_groups
    lgg = jnp.where(is_g, lg, NEG_INF)
    mg = jnp.max(lgg, axis=1, keepdims=True)
    g_sel = jnp.min(jnp.where(lgg == mg, lane, far), axis=1, keepdims=True)
    p_g = 1.0 / jnp.sum(jnp.where(is_g, jnp.exp(lgg - mg), 0.0), axis=1, keepdims=True)
    lo = n_groups + g_sel * per_group
    sel = (lane >= lo) & (lane < lo + per_group)
    le = jnp.where(sel, lg, NEG_INF)
    m1 = jnp.max(le, axis=1, keepdims=True)
    i1 = jnp.min(jnp.where(le == m1, lane, far), axis=1, keepdims=True)
    le2 = jnp.where(lane == i1, NEG_INF, le)
    m2 = jnp.max(le2, axis=1, keepdims=True)
    i2 = jnp.min(jnp.where(le2 == m2, lane, far), axis=1, keepdims=True)
    ratio = jnp.exp(m2 - m1)
    gate1 = p_g / (1.0 + ratio)
    gate2 = p_g * ratio / (1.0 + ratio)
    out = jnp.where(lane_i == 0, i1 - n_groups, 0.0)
    out = jnp.where(lane_i == 1, i2 - n_groups, out)
    out = jnp.where(lane_i == 2, gate1, out)
    return jnp.where(lane_i == 3, gate2, out)


def _outproj_kernel(x_ref, ya_ref, yb_ref, w_ref, g1_ref, sc_ref, sh_ref, gn_ref, wr_ref, br_ref,
                    xn_ref, h2_ref, rt_ref, *, n_groups, per_group):
    da = ya_ref.shape[1]
    mix = (jnp.dot(ya_ref[...], w_ref[:da, :], preferred_element_type=F32)
           + jnp.dot(yb_ref[...], w_ref[da:, :], preferred_element_type=F32))
    x = x_ref[...] + g1_ref[...] * mix.reshape(x_ref.shape)
    xn_ref[...] = x
    ms = jnp.mean(x * x, axis=-1, keepdims=True)
    h = x * lax.rsqrt(ms + EPS) * gn_ref[...]
    h = (h * (1.0 + sc_ref[...]) + sh_ref[...]).reshape(h2_ref.shape)
    h2_ref[...] = h
    h_hi = h.astype(BF16)
    h_lo = (h - h_hi.astype(F32)).astype(BF16)
    lg = (jnp.dot(h_hi, wr_ref[0], preferred_element_type=F32) + jnp.dot(h_lo, wr_ref[0], preferred_element_type=F32)
          + jnp.dot(h_hi, wr_ref[1], preferred_element_type=F32) + br_ref[...])
    rt_ref[...] = _route(lg, n_groups, per_group)


def _outproj(x, ya, yb, w_out, gate, scale, shift, g_norm, w_router, b_router, n_groups, per_group, max_rows):
    b, t, d = x.shape
    bb, tt = _row_tiling(b, t, max_rows)
    rows = bb * tt
    nt = t // tt
    da = ya.shape[1]
    xspec = pl.BlockSpec((bb, tt, d), lambda i, k: (i, k, 0))
    mspec = pl.BlockSpec((bb, 1, d), lambda i, k: (i, 0, 0))
    kern = functools.partial(_outproj_kernel, n_groups=n_groups, per_group=per_group)
    return pl.pallas_call(
        kern,
        grid=(b // bb, nt),
        in_specs=[xspec,
                  pl.BlockSpec((rows, da), lambda i, k: (i * nt + k, 0)),
                  pl.BlockSpec((rows, ya.shape[1]), lambda i, k: (i * nt + k, 0)),
                  pl.BlockSpec(w_out.shape, lambda i, k: (0, 0)),
                  mspec, mspec, mspec,
                  pl.BlockSpec((1, 1, d), lambda i, k: (0, 0, 0)),
                  pl.BlockSpec(w_router.shape, lambda i, k: (0, 0, 0)),
                  pl.BlockSpec((1, LANES), lambda i, k: (0, 0))],
        out_specs=[xspec,
                   pl.BlockSpec((rows, d), lambda i, k: (i * nt + k, 0)),
                   pl.BlockSpec((rows, LANES), lambda i, k: (i * nt + k, 0))],
        out_shape=[jax.ShapeDtypeStruct(x.shape, F32),
                   jax.ShapeDtypeStruct((b * t, d), F32),
                   jax.ShapeDtypeStruct((b * t, LANES), F32)],
        compiler_params=_cparams(("arbitrary", "arbitrary")),
        name="outproj",
    )(x, ya, yb, w_out, gate, scale, shift, g_norm.reshape(1, 1, d), w_router, b_router)


def _gather_kernel(idx_ref, src_a_ref, src_b_ref, out_ref, sem, *, n_a, chunk):
    i = pl.program_id(0)
    base = i * chunk

    def issue(r, _):
        t = idx_ref[base + r]

        @pl.when(t < n_a)
        def _():
            pltpu.make_async_copy(src_a_ref.at[pl.ds(t, 1)], out_ref.at[pl.ds(base + r, 1)], sem).start()

        @pl.when(t >= n_a)
        def _():
            pltpu.make_async_copy(src_b_ref.at[pl.ds(t - n_a, 1)], out_ref.at[pl.ds(base + r, 1)], sem).start()

        return 0

    lax.fori_loop(0, chunk, issue, 0)
    pltpu.make_async_copy(out_ref.at[pl.ds(base, chunk)], out_ref.at[pl.ds(base, chunk)], sem).wait()


def _gather_rows(idx, src_a, src_b):
    n_out = idx.shape[0]
    chunk = math.gcd(GATHER_CHUNK, n_out)
    d = src_a.shape[1]
    kern = functools.partial(_gather_kernel, n_a=src_a.shape[0], chunk=chunk)
    return pl.pallas_call(
        kern,
        grid_spec=pltpu.PrefetchScalarGridSpec(
            num_scalar_prefetch=1,
            grid=(n_out // chunk,),
            in_specs=[pl.BlockSpec(memory_space=pl.ANY), pl.BlockSpec(memory_space=pl.ANY)],
            out_specs=pl.BlockSpec(memory_space=pl.ANY),
            scratch_shapes=[pltpu.SemaphoreType.DMA(())]),
        out_shape=jax.ShapeDtypeStruct((n_out, d), src_a.dtype),
        compiler_params=_cparams(("arbitrary",)),
        name="gather_rows",
    )(idx, src_a, src_b)


def _expert_up_kernel(ie_ref, ij_ref, ib_ref, ioj_ref, inew_ref, n_ref, x_ref, wg_ref, wu_ref, o_ref,
                      wgb_scr, wub_scr):
    i = pl.program_id(0)

    @pl.when(i >= n_ref[0])
    def _():
        o_ref[...] = jnp.zeros(o_ref.shape, o_ref.dtype)

    @pl.when(i < n_ref[0])
    def _():
        @pl.when(inew_ref[i] == 1)
        def _():
            wgb_scr[...] = wg_ref[0].astype(BF16)
            wub_scr[...] = wu_ref[0].astype(BF16)

        xb = x_ref[...].astype(BF16)
        a = jnp.dot(xb, wgb_scr[...], preferred_element_type=F32)
        u = jnp.dot(xb, wub_scr[...], preferred_element_type=F32)
        o_ref[...] = (a * jax.nn.sigmoid(a) * u).astype(BF16)


def _expert_up(plan, x_buf, w_gate, w_up):
    n_rows, d = x_buf.shape
    de = w_gate.shape[2]
    tn = de // 2
    bm = MOE_BLOCK_ROWS
    n_items = plan["item_e"].shape[0]
    return pl.pallas_call(
        _expert_up_kernel,
        grid_spec=pltpu.PrefetchScalarGridSpec(
            num_scalar_prefetch=6,
            grid=(n_items,),
            in_specs=[pl.BlockSpec((bm, d), lambda i, ie, ij, ib, ioj, inew, n: (ib[i], 0)),
                      pl.BlockSpec((1, d, tn), lambda i, ie, ij, ib, ioj, inew, n: (ie[i], 0, ij[i])),
                      pl.BlockSpec((1, d, tn), lambda i, ie, ij, ib, ioj, inew, n: (ie[i], 0, ij[i]))],
            out_specs=pl.BlockSpec((bm, tn), lambda i, ie, ij, ib, ioj, inew, n: (ib[i], ioj[i])),
            scratch_shapes=[pltpu.VMEM((d, tn), BF16), pltpu.VMEM((d, tn), BF16)]),
        out_shape=jax.ShapeDtypeStruct((n_rows, de), BF16),
        compiler_params=_cparams(("arbitrary",)),
        name="expert_up",
    )(plan["item_e"], plan["item_j"], plan["item_b"], plan["item_oj"], plan["item_new"], plan["n_items"],
      x_buf, w_gate, w_up)


def _expert_down_kernel(be_ref, bb_ref, bnew_ref, n_ref, h_ref, wd_ref, o_ref, wdb_scr):
    i = pl.program_id(0)

    @pl.when(i >= n_ref[0])
    def _():
        o_ref[...] = jnp.zeros(o_ref.shape, o_ref.dtype)

    @pl.when(i < n_ref[0])
    def _():
        @pl.when(bnew_ref[i] == 1)
        def _():
            wdb_scr[...] = wd_ref[0].astype(BF16)

        o_ref[...] = jnp.dot(h_ref[...], wdb_scr[...], preferred_element_type=F32)


def _expert_down(plan, h_buf, w_down):
    n_rows, de = h_buf.shape
    d = w_down.shape[2]
    bm = MOE_BLOCK_ROWS
    n_blocks = plan["blk_e"].shape[0]
    return pl.pallas_call(
        _expert_down_kernel,
        grid_spec=pltpu.PrefetchScalarGridSpec(
            num_scalar_prefetch=4,
            grid=(n_blocks,),
            in_specs=[pl.BlockSpec((bm, de), lambda i, be, bb, bnew, n: (bb[i], 0)),
                      pl.BlockSpec((1, de, d), lambda i, be, bb, bnew, n: (be[i], 0, 0))],
            out_specs=pl.BlockSpec((bm, d), lambda i, be, bb, bnew, n: (bb[i], 0)),
            scratch_shapes=[pltpu.VMEM((de, d), BF16)]),
        out_shape=jax.ShapeDtypeStruct((n_rows, d), F32),
        compiler_params=_cparams(("arbitrary",)),
        name="expert_down",
    )(plan["blk_e"], plan["blk_b"], plan["blk_new"], plan["n_blocks"], h_buf, w_down)


def _moe_plan(expert, n_experts):
    bm = MOE_BLOCK_ROWS
    n_tok = expert.shape[0]
    n_asg = n_tok * TOP_K
    e_flat = expert.reshape(-1)
    onehot = (e_flat[:, None] == jnp.arange(n_experts, dtype=jnp.int32)[None, :]).astype(jnp.int32)
    csum = jnp.cumsum(onehot, axis=0)
    counts = csum[-1]
    rank = jnp.take_along_axis(csum, e_flat[:, None], axis=1)[:, 0] - 1
    nblk = (counts + bm - 1) // bm
    bend = jnp.cumsum(nblk)
    bstart = bend - nblk
    dest = bstart[e_flat] * bm + rank
    n_blocks = (n_asg + n_experts * (bm - 1) + bm - 1) // bm
    n_rows = -(-n_blocks * bm // GATHER_CHUNK) * GATHER_CHUNK
    tok = jnp.arange(n_asg, dtype=jnp.int32) // TOP_K
    row_src = jnp.zeros((n_rows,), jnp.int32).at[dest].set(tok, unique_indices=True)
    used = bend[-1]

    def expert_of_block(b):
        return jnp.minimum(jnp.searchsorted(bend, b, side="right"), n_experts - 1).astype(jnp.int32)

    blk = jnp.arange(n_blocks, dtype=jnp.int32)
    blk_e = expert_of_block(jnp.minimum(blk, used - 1))
    blk_new = (blk == bstart[blk_e]).astype(jnp.int32)
    n_items = 2 * n_blocks
    idx = jnp.arange(n_items, dtype=jnp.int32)
    it = jnp.minimum(idx, 2 * used - 1)
    item_e = expert_of_block(it // 2)
    off = it - 2 * bstart[item_e]
    item_j = (off // nblk[item_e]).astype(jnp.int32)
    item_b = (bstart[item_e] + off % nblk[item_e]).astype(jnp.int32)
    item_new = (item_b == bstart[item_e]).astype(jnp.int32)
    spare = idx - 2 * used
    is_spare = spare >= 0
    item_b = jnp.where(is_spare, used + spare // 2, item_b)
    item_oj = jnp.where(is_spare, spare % 2, item_j)
    return dict(dest=dest, row_src=row_src, blk_e=blk_e, blk_b=blk, blk_new=blk_new,
                n_blocks=used.reshape(1).astype(jnp.int32), item_e=item_e, item_j=item_j,
                item_b=item_b, item_oj=item_oj, item_new=item_new,
                n_items=(2 * used).reshape(1).astype(jnp.int32))


def _combine_kernel(x_ref, y_ref, rt_ref, g2_ref, gf_ref, o_ref, *, final_norm):
    rt = rt_ref[...]
    y = rt[:, 2:3] * y_ref[0] + rt[:, 3:4] * y_ref[1]
    x = x_ref[...] + g2_ref[...] * y.reshape(x_ref.shape)
    if final_norm:
        ms = jnp.mean(x * x, axis=-1, keepdims=True)
        x = x * lax.rsqrt(ms + EPS) * gf_ref[...]
    o_ref[...] = x


def _combine(x, y_tok, row0, route, gate, g_final, final_norm, max_rows):
    b, t, d = x.shape
    bb, tt = _row_tiling(b, t, max_rows)
    rows = bb * tt
    nt = t // tt
    assert row0 % rows == 0
    blk0 = row0 // rows
    xspec = pl.BlockSpec((bb, tt, d), lambda i, k: (i, k, 0))
    kern = functools.partial(_combine_kernel, final_norm=final_norm)
    return pl.pallas_call(
        kern,
        grid=(b // bb, nt),
        in_specs=[xspec,
                  pl.BlockSpec((TOP_K, rows, d), lambda i, k: (0, blk0 + i * nt + k, 0)),
                  pl.BlockSpec((rows, LANES), lambda i, k: (i * nt + k, 0)),
                  pl.BlockSpec((bb, 1, d), lambda i, k: (i, 0, 0)),
                  pl.BlockSpec((1, 1, d), lambda i, k: (0, 0, 0))],
        out_specs=xspec,
        out_shape=jax.ShapeDtypeStruct(x.shape, F32),
        compiler_params=_cparams(("arbitrary", "arbitrary")),
        name="combine",
    )(x, y_tok, route, gate, g_final.reshape(1, 1, d))


def _layer_params(l, w):
    d = w["w_in"].shape[1]
    nh = w["w_q"].shape[1]
    da = w["ln_v_g"].shape[1]
    db = w["w_conv"].shape[2]
    n_main = 2 * da + 2 * db
    w_in = w["w_in"][l]
    w_gate = jnp.zeros((d, 2 * LANES), BF16)
    w_gate = w_gate.at[:, :nh].set(w_in[:, n_main:n_main + nh].astype(BF16))
    w_gate = w_gate.at[:, LANES:LANES + nh].set(w_in[:, n_main + nh:n_main + 2 * nh].astype(BF16))
    n_groups = w["w_r1"].shape[2]
    per_group = w["w_r2"].shape[3]
    w_r = jnp.concatenate([w["w_r1"][l], jnp.transpose(w["w_r2"][l], (1, 0, 2)).reshape(d, n_groups * per_group)],
                          axis=1)
    n_r = w_r.shape[1]
    w_r = jnp.pad(w_r, ((0, 0), (0, LANES - n_r)))
    w_r_hi = w_r.astype(BF16)
    w_r_lo = (w_r - w_r_hi.astype(F32)).astype(BF16)
    b_r = jnp.pad(jnp.concatenate([w["b_r1"][l], w["b_r2"][l].reshape(-1)]), (0, LANES - n_r)).reshape(1, LANES)
    return dict(
        g_norm1=w["g_norm1"][l], g_norm2=w["g_norm2"][l],
        w_main=w_in[:, :n_main].astype(BF16), w_gate=w_gate,
        ln_g=w["ln_v_g"][l], ln_b=w["ln_v_b"][l],
        w_conv=w["w_conv"][l], b_conv=w["b_conv"][l].reshape(1, db),
        w_q=w["w_q"][l].astype(BF16), w_k=w["w_k"][l].astype(BF16),
        w_kt=jnp.swapaxes(w["w_k"][l], 1, 2).astype(BF16), w_v=w["w_v"][l].astype(BF16),
        b_i=jnp.pad(w["b_i"][l], (0, LANES - nh)).reshape(1, LANES),
        b_f=jnp.pad(w["b_f"][l], (0, LANES - nh)).reshape(1, LANES),
        g_mh=w["g_mh"][l].reshape(1, db), k_scale=1.0 / math.sqrt(LANES),
        w_out=w["w_out"][l].astype(BF16),
        w_router=jnp.stack([w_r_hi, w_r_lo]), b_router=b_r,
        n_groups=n_groups, per_group=per_group,
    )


def _mix_weights(w_spatial, b_spatial, L, da):
    nh = w_spatial.shape[0]
    causal = jnp.tril(jnp.ones((L, L), dtype=bool))
    w = jnp.where(causal[None], w_spatial[:, :L, :L], 0.0)
    reps = TILE // L
    if reps > 1:
        eye = jnp.eye(reps, dtype=F32)
        w = jnp.einsum("ab,hts->hatbs", eye, w).reshape(nh, TILE, TILE)
    bias = jnp.tile(jnp.transpose(b_spatial[:, :L]), (reps, 1))
    bias_full = jnp.repeat(bias, da // nh, axis=1)
    return w.astype(BF16), bias_full


def kernel(x_prompt, x_sample, c_prompt, c_sample, state_mlstm_C, state_mlstm_n, state_mlstm_m, state_conv, w_ada, b_ada, g_norm1, g_norm2, w_in, ln_v_g, ln_v_b, w_spatial, b_spatial, w_conv, b_conv, w_q, w_k, w_v, b_i, b_f, g_mh, w_out, w_r1, b_r1, w_r2, b_r2, w_e_gate, w_e_up, w_e_down, g_final):
    weights = dict(w_in=w_in, ln_v_g=ln_v_g, ln_v_b=ln_v_b, w_conv=w_conv, b_conv=b_conv, w_q=w_q, w_k=w_k,
                   w_v=w_v, b_i=b_i, b_f=b_f, g_mh=g_mh, w_out=w_out, w_r1=w_r1, b_r1=b_r1, w_r2=w_r2, b_r2=b_r2,
                   g_norm1=g_norm1, g_norm2=g_norm2)
    depth = w_in.shape[0]
    bp, tp, d = x_prompt.shape
    bs, ts, _ = x_sample.shape
    nh = w_q.shape[1]
    da = ln_v_g.shape[1]
    db = w_conv.shape[2]
    n_mod = w_ada.shape[2] // d
    n_experts = w_e_gate.shape[1]
    n_p, n_s = bp * tp, bs * ts
    assert da == db and da % LANES == 0 and nh * LANES == db and tp % TILE == 0 and TILE % ts == 0
    assert ts >= CONV_W - 1 and ts % SUBLANES == 0 and n_s % TILE == 0
    xm_col, zo_col = 2 * da // db, 2 * da // db + 1

    mod = _ada(jnp.concatenate([c_prompt, c_sample], axis=0), w_ada, b_ada).reshape(depth, bp + bs, n_mod, 1, d)

    xp, xs = x_prompt, x_sample
    outs = {k: [] for k in ("cp", "np", "mp", "bp", "cs", "ns", "ms", "bs", "vs")}
    for l in range(depth):
        p = _layer_params(l, weights)
        mods_p = [mod[l, :bp, i] for i in range(n_mod)]
        mods_s = [mod[l, bp:, i] for i in range(n_mod)]

        proj_p, gates_p = _inproj(xp, mods_p[1], mods_p[0], p["g_norm1"], p["w_main"], p["w_gate"], 1024)
        w_mix_p, bias_p = _mix_weights(w_spatial[l], b_spatial[l], min(tp, TILE), da)
        (ya_p,) = _grpa(proj_p, p["ln_g"], p["ln_b"], w_mix_p, bias_p, da, False)
        yb_p, c_p, n_pn, m_p = _mlstm_prompt(proj_p, gates_p, p, bp, tp, db, nh, xm_col, zo_col)
        outs["cp"].append(c_p)
        outs["np"].append(n_pn.reshape(bp, nh, LANES))
        outs["mp"].append(m_p[:, 0, :nh])
        outs["bp"].append(proj_p.reshape(bp, tp, -1)[:, tp - (CONV_W - 1):, 2 * da:2 * da + db])

        proj_s, gates_s = _inproj(xs, mods_s[1], mods_s[0], p["g_norm1"], p["w_main"], p["w_gate"], 512)
        w_mix_s, bias_s = _mix_weights(w_spatial[l], b_spatial[l], min(ts, TILE), da)
        ya_s, v_s = _grpa(proj_s, p["ln_g"], p["ln_b"], w_mix_s, bias_s, da, True)
        prev_rows = jnp.pad(state_conv[l], ((0, 0), (ts - (CONV_W - 1), 0), (0, 0))).reshape(n_s, db)
        n0_rows = jnp.repeat(state_mlstm_n[l].reshape(bs, db), ts, axis=0)
        m0_rows = jnp.pad(jnp.repeat(state_mlstm_m[l], ts, axis=0), ((0, 0), (0, LANES - nh)))
        yb_s, c_s, n_rows_s, m_rows_s = _mlstm_sample(proj_s, gates_s, prev_rows, state_mlstm_C[l], n0_rows, m0_rows,
                                                      p, math.gcd(ts, TILE), db, nh, xm_col, zo_col)
        outs["cs"].append(c_s)
        outs["ns"].append(n_rows_s.reshape(bs, ts, nh, LANES)[:, ts - 1])
        outs["ms"].append(m_rows_s.reshape(bs, ts, LANES)[:, ts - 1, :nh])
        outs["bs"].append(proj_s.reshape(bs, ts, -1)[:, ts - (CONV_W - 1):, 2 * da:2 * da + db])
        outs["vs"].append(v_s.reshape(bs, ts, da))

        rargs = (p["w_router"], p["b_router"], p["n_groups"], p["per_group"])
        xp, h2_p, rt_p = _outproj(xp, ya_p, yb_p, p["w_out"], mods_p[2], mods_p[4], mods_p[3], p["g_norm2"], *rargs, 256)
        xs, h2_s, rt_s = _outproj(xs, ya_s, yb_s, p["w_out"], mods_s[2], mods_s[4], mods_s[3], p["g_norm2"], *rargs, 256)

        route = jnp.concatenate([rt_p, rt_s], axis=0)
        plan = _moe_plan(route[:, :TOP_K].astype(jnp.int32), n_experts)
        x_buf = _gather_rows(plan["row_src"], h2_p, h2_s)
        h_buf = _expert_up(plan, x_buf, w_e_gate[l], w_e_up[l])
        y_buf = _expert_down(plan, h_buf, w_e_down[l])
        dest_t = jnp.transpose(plan["dest"].reshape(n_p + n_s, TOP_K)).reshape(-1)
        y_tok = _gather_rows(dest_t, y_buf, y_buf).reshape(TOP_K, n_p + n_s, d)
        last = l == depth - 1
        xp = _combine(xp, y_tok, 0, rt_p, mods_p[5], g_final, last, 512)
        xs = _combine(xs, y_tok, n_p, rt_s, mods_s[5], g_final, last, 256)

    st = jnp.stack
    return (xp, xs, st(outs["cp"]), st(outs["np"]), st(outs["mp"]), st(outs["bp"]),
            st(outs["cs"]), st(outs["ns"]), st(outs["ms"]), st(outs["bs"]), st(outs["vs"]))
```

```python
import functools
import math

import jax
import jax.numpy as jnp
from jax import lax
from jax.experimental import pallas as pl
from jax.experimental.pallas import tpu as pltpu

F32 = jnp.float32
BF16 = jnp.bfloat16
EPS = 1e-6
LANES = 128
SUBLANES = 8
TILE = 128
CONV_W = 4
TOP_K = 2
VMEM_LIMIT_BYTES = 56 * 1024 * 1024
MOE_BLOCK_ROWS = 256
GATHER_CHUNK = 512
NEG_INF = float("-inf")


def _cparams(sem):
    return pltpu.CompilerParams(dimension_semantics=sem, vmem_limit_bytes=VMEM_LIMIT_BYTES)


def _split3(x):
    x1 = x.astype(BF16)
    r1 = x - x1.astype(F32)
    x2 = r1.astype(BF16)
    r2 = r1 - x2.astype(F32)
    return x1, x2, r2.astype(BF16)


def _ada_kernel(c_ref, w_ref, b_ref, o_ref):
    c = c_ref[...]
    a = c * jax.nn.sigmoid(c)
    o_ref[0] = jnp.dot(a.astype(BF16), w_ref[0].astype(BF16), preferred_element_type=F32) + b_ref[0]


def _ada(c_all, w_ada, b_ada):
    depth, d, n = w_ada.shape
    bc = c_all.shape[0]
    tn = min(1024, n)
    return pl.pallas_call(
        _ada_kernel,
        grid=(depth, n // tn),
        in_specs=[pl.BlockSpec((bc, d), lambda l, j: (0, 0)),
                  pl.BlockSpec((1, d, tn), lambda l, j: (l, 0, j)),
                  pl.BlockSpec((1, 1, tn), lambda l, j: (l, 0, j))],
        out_specs=pl.BlockSpec((1, bc, tn), lambda l, j: (l, 0, j)),
        out_shape=jax.ShapeDtypeStruct((depth, bc, n), F32),
        compiler_params=_cparams(("arbitrary", "arbitrary")),
        name="ada",
    )(c_all, w_ada, b_ada.reshape(depth, 1, n))


def _inproj_kernel(x_ref, sc_ref, sh_ref, g_ref, w_ref, wg_ref, o_ref, og_ref, hn_ref):
    @pl.when(pl.program_id(2) == 0)
    def _():
        x = x_ref[...]
        ms = jnp.mean(x * x, axis=-1, keepdims=True)
        h = x * lax.rsqrt(ms + EPS) * g_ref[...]
        h = h * (1.0 + sc_ref[...]) + sh_ref[...]
        hb = h.reshape(hn_ref.shape).astype(BF16)
        hn_ref[...] = hb
        og_ref[...] = jnp.dot(hb, wg_ref[...], preferred_element_type=F32)

    o_ref[...] = jnp.dot(hn_ref[...], w_ref[...], preferred_element_type=F32)


def _row_tiling(b, t, max_rows):
    if t >= max_rows:
        assert t % max_rows == 0
        return 1, max_rows
    bb = min(b, max_rows // t)
    assert b % bb == 0 and t % SUBLANES == 0
    return bb, t


def _inproj(x, scale, shift, g_norm, w_main, w_gate, max_rows):
    b, t, d = x.shape
    n = w_main.shape[1]
    ng = w_gate.shape[1]
    bb, tt = _row_tiling(b, t, max_rows)
    rows = bb * tt
    tn = min(512, n)
    nt = t // tt
    return pl.pallas_call(
        _inproj_kernel,
        grid=(b // bb, nt, n // tn),
        in_specs=[pl.BlockSpec((bb, tt, d), lambda i, k, j: (i, k, 0)),
                  pl.BlockSpec((bb, 1, d), lambda i, k, j: (i, 0, 0)),
                  pl.BlockSpec((bb, 1, d), lambda i, k, j: (i, 0, 0)),
                  pl.BlockSpec((1, 1, d), lambda i, k, j: (0, 0, 0)),
                  pl.BlockSpec((d, tn), lambda i, k, j: (0, j)),
                  pl.BlockSpec((d, ng), lambda i, k, j: (0, 0))],
        out_specs=[pl.BlockSpec((rows, tn), lambda i, k, j: (i * nt + k, j)),
                   pl.BlockSpec((rows, ng), lambda i, k, j: (i * nt + k, 0))],
        out_shape=[jax.ShapeDtypeStruct((b * t, n), F32), jax.ShapeDtypeStruct((b * t, ng), F32)],
        scratch_shapes=[pltpu.VMEM((rows, d), BF16)],
        compiler_params=_cparams(("arbitrary", "arbitrary", "arbitrary")),
        name="inproj",
    )(x, scale, shift, g_norm.reshape(1, 1, d), w_main, w_gate)


def _grpa_kernel(u_ref, v_ref, lg_ref, lb_ref, w_ref, bias_ref, ya_ref, *maybe_v_out):
    rows, da = u_ref.shape
    nh = w_ref.shape[0]
    g = jax.nn.gelu(v_ref[...])
    mu = jnp.mean(g, axis=-1, keepdims=True)
    gc = g - mu
    var = jnp.mean(gc * gc, axis=-1, keepdims=True)
    v = gc * lax.rsqrt(var + EPS) * lg_ref[...] + lb_ref[...]
    if maybe_v_out:
        maybe_v_out[0][...] = v
    vb = v.astype(BF16)
    for c in range(rows // TILE):
        r0 = c * TILE
        for h in range(nh):
            c0 = h * LANES
            z = jnp.dot(w_ref[h], vb[r0:r0 + TILE, c0:c0 + LANES], preferred_element_type=F32)
            u = jax.nn.gelu(u_ref[r0:r0 + TILE, c0:c0 + LANES])
            y = u * (z + bias_ref[:, c0:c0 + LANES])
            ya_ref[r0:r0 + TILE, c0:c0 + LANES] = y.astype(BF16)


def _grpa(proj, ln_g, ln_b, w_mix, bias_full, da, want_v):
    n_rows = proj.shape[0]
    rows = min(512, n_rows)
    nh = w_mix.shape[0]
    out_specs = [pl.BlockSpec((rows, da), lambda i: (i, 0))]
    out_shape = [jax.ShapeDtypeStruct((n_rows, da), BF16)]
    if want_v:
        out_specs.append(pl.BlockSpec((rows, da), lambda i: (i, 0)))
        out_shape.append(jax.ShapeDtypeStruct((n_rows, da), F32))
    return pl.pallas_call(
        _grpa_kernel,
        grid=(n_rows // rows,),
        in_specs=[pl.BlockSpec((rows, da), lambda i: (i, 0)),
                  pl.BlockSpec((rows, da), lambda i: (i, 1)),
                  pl.BlockSpec((1, da), lambda i: (0, 0)),
                  pl.BlockSpec((1, da), lambda i: (0, 0)),
                  pl.BlockSpec((nh, TILE, TILE), lambda i: (0, 0, 0)),
                  pl.BlockSpec((TILE, da), lambda i: (0, 0))],
        out_specs=out_specs,
        out_shape=out_shape,
        compiler_params=_cparams(("arbitrary",)),
        name="grpa",
    )(proj, proj, ln_g.reshape(1, da), ln_b.reshape(1, da), w_mix, bias_full)


def _group_last(x, L):
    if L == TILE:
        return jnp.broadcast_to(x[TILE - 1:TILE, :], x.shape)
    g = TILE // L
    x3 = x.reshape(g, L, x.shape[1])
    return jnp.broadcast_to(x3[:, L - 1:L, :], x3.shape).reshape(x.shape)


def _group_sum(x, L):
    if L == TILE:
        return jnp.broadcast_to(jnp.sum(x, axis=0, keepdims=True), x.shape)
    g = TILE // L
    x3 = x.reshape(g, L, x.shape[1])
    return jnp.broadcast_to(jnp.sum(x3, axis=1, keepdims=True), x3.shape).reshape(x.shape)


def _mlstm_kernel(*refs, L, carry, nh, k_scale):
    if carry:
        (xm_ref, zo_ref, g_ref, wconv_ref, bconv_ref, wq_ref, wk_ref, wkt_ref, wv_ref, bi_ref, bf_ref, gmh_ref,
         yb_ref, c_ref, n_ref, m_ref, prev_scr) = refs
        c_in_ref = c_ref
        first = pl.program_id(1) == 0

        @pl.when(first)
        def _():
            c_ref[...] = jnp.zeros(c_ref.shape, F32)
            n_ref[...] = jnp.zeros(n_ref.shape, F32)
            m_ref[...] = jnp.zeros(m_ref.shape, F32)
            prev_scr[...] = jnp.zeros(prev_scr.shape, F32)

        prev = prev_scr[...]
        n0_rows = jnp.broadcast_to(n_ref[0], (TILE, n_ref.shape[2]))
        m0 = jnp.broadcast_to(m_ref[0], (TILE, LANES))
    else:
        (xm_ref, zo_ref, g_ref, prev_ref, c_layer_ref, n0_ref, m0_ref,
         wconv_ref, bconv_ref, wq_ref, wk_ref, wkt_ref, wv_ref, bi_ref, bf_ref, gmh_ref,
         yb_ref, c_ref, n_ref, m_ref, q_scr, qc_scr, dec_scr) = refs
        c_in_ref = c_layer_ref.at[0]
        prev = prev_ref[...]
        n0_rows = n0_ref[...]
        m0 = m0_ref[...]

    groups = TILE // L
    x = xm_ref[...]
    row_in_group = lax.broadcasted_iota(jnp.int32, (TILE, 1), 0) % L

    acc = bconv_ref[...] + wconv_ref[CONV_W - 1:CONV_W, :] * x
    for j in range(1, CONV_W):
        from_prev = pltpu.roll(prev, (j - L) % TILE, 0)
        shifted = jnp.where(row_in_group < j, from_prev, pltpu.roll(x, j, 0))
        acc = acc + wconv_ref[CONV_W - 1 - j:CONV_W - j, :] * shifted
    xc = acc * jax.nn.sigmoid(acc)
    if carry:
        prev_scr[...] = x

    gates = g_ref[...]
    gi = gates[:, :LANES] + bi_ref[...]
    gf = gates[:, LANES:] + bf_ref[...]
    lf = jnp.minimum(gf, 0.0) - jnp.log1p(jnp.exp(-jnp.abs(gf)))

    r_i = lax.broadcasted_iota(jnp.int32, (TILE, TILE), 0)
    c_i = lax.broadcasted_iota(jnp.int32, (TILE, TILE), 1)
    mask = (c_i <= r_i) & ((c_i // L) == (r_i // L)) if L != TILE else (c_i <= r_i)
    mask_b = jnp.where(mask, 1.0, 0.0).astype(BF16)
    lf1, lf2, lf3 = _split3(lf)
    cum = (jnp.dot(mask_b, lf1, preferred_element_type=F32) + jnp.dot(mask_b, lf2, preferred_element_type=F32)
           + jnp.dot(mask_b, lf3, preferred_element_type=F32))
    cum_t = cum.T
    gi_t = gi.T
    log_inter = cum + m0
    lane = lax.broadcasted_iota(jnp.int32, (TILE, LANES), 1)

    def d_matrix(h):
        d = cum[:, h:h + 1] - cum_t[h:h + 1, :] + gi_t[h:h + 1, :]
        return jnp.where(mask, d, NEG_INF)

    mt = jnp.zeros((TILE, LANES), F32)
    for h in range(nh):
        mt_h = jnp.maximum(log_inter[:, h:h + 1], jnp.max(d_matrix(h), axis=1, keepdims=True))
        mt = jnp.where(lane == h, mt_h, mt)

    m_last = _group_last(mt, L)
    w_inter = jnp.exp(log_inter - mt)
    w_last = jnp.exp(_group_last(cum, L) - cum + gi - m_last)
    w_last_t = w_last.T
    inv_floor = jnp.exp(-mt)

    if not carry:
        dec_scr[...] = w_inter

    for h in range(nh):
        c0 = h * LANES
        xc_h = xc[:, c0:c0 + LANES].astype(BF16)
        q = jnp.dot(xc_h, wq_ref[h], preferred_element_type=F32)
        k = jnp.dot(xc_h, wk_ref[h], preferred_element_type=F32) * k_scale
        k_t = lax.dot_general(wkt_ref[h], xc_h, (((1,), (1,)), ((), ())), preferred_element_type=F32) * k_scale
        v = jnp.dot(x[:, c0:c0 + LANES].astype(BF16), wv_ref[h], preferred_element_type=F32)
        qb = q.astype(BF16)
        vb = v.astype(BF16)
        mt_h = mt[:, h:h + 1]
        wi_h = w_inter[:, h:h + 1]
        w_intra = jnp.exp(d_matrix(h) - mt_h)
        s = jnp.dot(qb, k_t.astype(BF16), preferred_element_type=F32) * w_intra

        if carry:
            qc = jnp.dot(qb, c_in_ref[0, h].astype(BF16), preferred_element_type=F32)
        else:
            q_scr[...] = q

            def qc_body(g, _, h=h):
                r0 = pl.multiple_of(g * L, L)
                qg = q_scr[pl.ds(r0, L), :].astype(BF16)
                qc_scr[pl.ds(r0, L), :] = jnp.dot(qg, c_in_ref[g, h].astype(BF16), preferred_element_type=F32)
                return 0

            lax.fori_loop(0, groups, qc_body, 0)
            qc = qc_scr[...]

        num = wi_h * qc + jnp.dot(s.astype(BF16), vb, preferred_element_type=F32)
        qn = jnp.sum(q * n0_rows[:, c0:c0 + LANES], axis=1, keepdims=True)
        den = wi_h * qn + jnp.sum(s, axis=1, keepdims=True)
        hh = num / jnp.maximum(jnp.abs(den), inv_floor[:, h:h + 1])
        hh = hh * lax.rsqrt(jnp.mean(hh * hh, axis=1, keepdims=True) + EPS) * gmh_ref[:, c0:c0 + LANES]
        yb_ref[:, c0:c0 + LANES] = (jax.nn.sigmoid(zo_ref[:, c0:c0 + LANES]) * hh).astype(BF16)

        kw_t = k_t * w_last_t[h:h + 1, :]
        n_rows = wi_h * n0_rows[:, c0:c0 + LANES] + _group_sum(w_last[:, h:h + 1] * k, L)
        if carry:
            decay = jnp.broadcast_to(w_inter[TILE - 1:TILE, h:h + 1], (LANES, LANES))
            c_ref[0, h] = decay * c_in_ref[0, h] + jnp.dot(kw_t.astype(BF16), vb, preferred_element_type=F32)
            n_ref[0, :, c0:c0 + LANES] = n_rows[TILE - 1:TILE, :]
        else:
            n_ref[:, c0:c0 + LANES] = n_rows

            def c_body(g, _, h=h, kw_t=kw_t, vb=vb):
                kw_g = jnp.where((c_i // L) == g, kw_t, 0.0).astype(BF16)
                upd = jnp.dot(kw_g, vb, preferred_element_type=F32)
                drow = dec_scr[pl.ds(g * L + L - 1, 1), :]
                decay = jnp.broadcast_to(drow[:, h:h + 1], (LANES, LANES))
                c_ref[g, h] = decay * c_in_ref[g, h] + upd
                return 0

            lax.fori_loop(0, groups, c_body, 0)

    if carry:
        m_ref[0] = mt[TILE - 1:TILE, :]
    else:
        m_ref[...] = mt


def _mlstm_weights(p):
    return [p["w_conv"], p["b_conv"], p["w_q"], p["w_k"], p["w_kt"], p["w_v"], p["b_i"], p["b_f"], p["g_mh"]]


def _mlstm_weight_specs(nh, db, nidx):
    z2 = (lambda i, c: (0, 0)) if nidx == 2 else (lambda i: (0, 0))
    z3 = (lambda i, c: (0, 0, 0)) if nidx == 2 else (lambda i: (0, 0, 0))
    hw = pl.BlockSpec((nh, LANES, LANES), z3)
    return [pl.BlockSpec((CONV_W, db), z2), pl.BlockSpec((1, db), z2), hw, hw, hw, hw,
            pl.BlockSpec((1, LANES), z2), pl.BlockSpec((1, LANES), z2), pl.BlockSpec((1, db), z2)]


def _mlstm_prompt(proj, gates, p, b, t, db, nh, xm_col, zo_col):
    nt = t // TILE
    kern = functools.partial(_mlstm_kernel, L=TILE, carry=True, nh=nh, k_scale=p["k_scale"])
    return pl.pallas_call(
        kern,
        grid=(b, nt),
        in_specs=[pl.BlockSpec((TILE, db), lambda i, c: (i * nt + c, xm_col)),
                  pl.BlockSpec((TILE, db), lambda i, c: (i * nt + c, zo_col)),
                  pl.BlockSpec((TILE, 2 * LANES), lambda i, c: (i * nt + c, 0))] + _mlstm_weight_specs(nh, db, 2),
        out_specs=[pl.BlockSpec((TILE, db), lambda i, c: (i * nt + c, 0)),
                   pl.BlockSpec((1, nh, LANES, LANES), lambda i, c: (i, 0, 0, 0)),
                   pl.BlockSpec((1, 1, db), lambda i, c: (i, 0, 0)),
                   pl.BlockSpec((1, 1, LANES), lambda i, c: (i, 0, 0))],
        out_shape=[jax.ShapeDtypeStruct((b * t, db), BF16),
                   jax.ShapeDtypeStruct((b, nh, LANES, LANES), F32),
                   jax.ShapeDtypeStruct((b, 1, db), F32),
                   jax.ShapeDtypeStruct((b, 1, LANES), F32)],
        scratch_shapes=[pltpu.VMEM((TILE, db), F32)],
        compiler_params=_cparams(("arbitrary", "arbitrary")),
        name="mlstm_prompt",
    )(proj, proj, gates, *_mlstm_weights(p))


def _mlstm_sample(proj, gates, prev_rows, c0, layer, n0_rows, m0_rows, p, L, db, nh, xm_col, zo_col):
    n_rows = proj.shape[0]
    groups = TILE // L
    kern = functools.partial(_mlstm_kernel, L=L, carry=False, nh=nh, k_scale=p["k_scale"])
    row_spec = pl.BlockSpec((TILE, db), lambda i: (i, 0))
    return pl.pallas_call(
        kern,
        grid=(n_rows // TILE,),
        in_specs=[pl.BlockSpec((TILE, db), lambda i: (i, xm_col)),
                  pl.BlockSpec((TILE, db), lambda i: (i, zo_col)),
                  pl.BlockSpec((TILE, 2 * LANES), lambda i: (i, 0)),
                  row_spec,
                  pl.BlockSpec((1, groups, nh, LANES, LANES), lambda i: (layer, i, 0, 0, 0)),
                  row_spec,
                  pl.BlockSpec((TILE, LANES), lambda i: (i, 0))] + _mlstm_weight_specs(nh, db, 1),
        out_specs=[row_spec,
                   pl.BlockSpec((groups, nh, LANES, LANES), lambda i: (i, 0, 0, 0)),
                   row_spec,
                   pl.BlockSpec((TILE, LANES), lambda i: (i, 0))],
        out_shape=[jax.ShapeDtypeStruct((n_rows, db), BF16),
                   jax.ShapeDtypeStruct(c0.shape[1:], F32),
                   jax.ShapeDtypeStruct((n_rows, db), F32),
                   jax.ShapeDtypeStruct((n_rows, LANES), F32)],
        scratch_shapes=[pltpu.VMEM((TILE, LANES), F32), pltpu.VMEM((TILE, LANES), F32),
                        pltpu.VMEM((TILE, LANES), F32)],
        compiler_params=_cparams(("arbitrary",)),
        name="mlstm_sample",
    )(proj, proj, gates, prev_rows, c0, n0_rows, m0_rows, *_mlstm_weights(p))


def _store_token_major(ref, val):
    n, d = val.shape
    nck = d // LANES
    for c in range(nck):
        ref[pl.ds(c, n, stride=nck), :] = val[:, c * LANES:(c + 1) * LANES]


def _load_token_major(ref, n, c, nck):
    return ref[pl.ds(c, n, stride=nck), :]


def _route(lg, n_groups, per_group):
    lane_i = lax.broadcasted_iota(jnp.int32, lg.shape, 1)
    lane = lane_i.astype(F32)
    far = float(LANES)
    is_g = lane_i < n_groups
    lgg = jnp.where(is_g, lg, NEG_INF)
    mg = jnp.max(lgg, axis=1, keepdims=True)
    g_sel = jnp.min(jnp.where(lgg == mg, lane, far), axis=1, keepdims=True)
    p_g = 1.0 / jnp.sum(jnp.where(is_g, jnp.exp(lgg - mg), 0.0), axis=1, keepdims=True)
    lo = n_groups + g_sel * per_group
    sel = (lane >= lo) & (lane < lo + per_group)
    le = jnp.where(sel, lg, NEG_INF)
    m1 = jnp.max(le, axis=1, keepdims=True)
    i1 = jnp.min(jnp.where(le == m1, lane, far), axis=1, keepdims=True)
    le2 = jnp.where(lane == i1, NEG_INF, le)
    m2 = jnp.max(le2, axis=1, keepdims=True)
    i2 = jnp.min(jnp.where(le2 == m2, lane, far), axis=1, keepdims=True)
    ratio = jnp.exp(m2 - m1)
    gate1 = p_g / (1.0 + ratio)
    gate2 = p_g * ratio / (1.0 + ratio)
    out = jnp.where(lane_i == 0, i1 - n_groups, 0.0)
    out = jnp.where(lane_i == 1, i2 - n_groups, out)
    out = jnp.where(lane_i == 2, gate1, out)
    return jnp.where(lane_i == 3, gate2, out)


def _outproj_kernel(x_ref, ya_ref, yb_ref, w_ref, g1_ref, sc_ref, sh_ref, gn_ref, wr_ref, br_ref,
                    xn_ref, h2_ref, rt_ref, *, n_groups, per_group):
    da = ya_ref.shape[1]
    mix = (jnp.dot(ya_ref[...], w_ref[:da, :], preferred_element_type=F32)
           + jnp.dot(yb_ref[...], w_ref[da:, :], preferred_element_type=F32))
    x = x_ref[...] + g1_ref[...] * mix.reshape(x_ref.shape)
    xn_ref[...] = x
    ms = jnp.mean(x * x, axis=-1, keepdims=True)
    h = x * lax.rsqrt(ms + EPS) * gn_ref[...]
    rows = ya_ref.shape[0]
    h = (h * (1.0 + sc_ref[...]) + sh_ref[...]).reshape(rows, x_ref.shape[2])
    _store_token_major(h2_ref, h)
    h_hi = h.astype(BF16)
    h_lo = (h - h_hi.astype(F32)).astype(BF16)
    lg = (jnp.dot(h_hi, wr_ref[0], preferred_element_type=F32) + jnp.dot(h_lo, wr_ref[0], preferred_element_type=F32)
          + jnp.dot(h_hi, wr_ref[1], preferred_element_type=F32) + br_ref[...])
    rt_ref[...] = _route(lg, n_groups, per_group)


def _outproj(x, ya, yb, w_out, gate, scale, shift, g_norm, w_router, b_router, n_groups, per_group, max_rows):
    b, t, d = x.shape
    bb, tt = _row_tiling(b, t, max_rows)
    rows = bb * tt
    nt = t // tt
    da = ya.shape[1]
    xspec = pl.BlockSpec((bb, tt, d), lambda i, k: (i, k, 0))
    mspec = pl.BlockSpec((bb, 1, d), lambda i, k: (i, 0, 0))
    kern = functools.partial(_outproj_kernel, n_groups=n_groups, per_group=per_group)
    return pl.pallas_call(
        kern,
        grid=(b // bb, nt),
        in_specs=[xspec,
                  pl.BlockSpec((rows, da), lambda i, k: (i * nt + k, 0)),
                  pl.BlockSpec((rows, ya.shape[1]), lambda i, k: (i * nt + k, 0)),
                  pl.BlockSpec(w_out.shape, lambda i, k: (0, 0)),
                  mspec, mspec, mspec,
                  pl.BlockSpec((1, 1, d), lambda i, k: (0, 0, 0)),
                  pl.BlockSpec(w_router.shape, lambda i, k: (0, 0, 0)),
                  pl.BlockSpec((1, LANES), lambda i, k: (0, 0))],
        out_specs=[xspec,
                   pl.BlockSpec((rows * (d // LANES), LANES), lambda i, k: (i * nt + k, 0)),
                   pl.BlockSpec((rows, LANES), lambda i, k: (i * nt + k, 0))],
        out_shape=[jax.ShapeDtypeStruct(x.shape, F32),
                   jax.ShapeDtypeStruct((b * t * (d // LANES), LANES), F32),
                   jax.ShapeDtypeStruct((b * t, LANES), F32)],
        compiler_params=_cparams(("arbitrary", "arbitrary")),
        name="outproj",
    )(x, ya, yb, w_out, gate, scale, shift, g_norm.reshape(1, 1, d), w_router, b_router)


def _gather_kernel(idx_ref, src_a_ref, src_b_ref, out_ref, sem, *, n_a, chunk, nck):
    i = pl.program_id(0)
    base = i * chunk

    def issue(r, _):
        t = idx_ref[base + r]
        dst = out_ref.at[pl.ds(pl.multiple_of((base + r) * nck, nck), nck)]

        @pl.when(t < n_a)
        def _():
            pltpu.make_async_copy(src_a_ref.at[pl.ds(pl.multiple_of(t * nck, nck), nck)], dst, sem).start()

        @pl.when(t >= n_a)
        def _():
            pltpu.make_async_copy(src_b_ref.at[pl.ds(pl.multiple_of((t - n_a) * nck, nck), nck)], dst, sem).start()

        return 0

    lax.fori_loop(0, chunk, issue, 0)
    whole = out_ref.at[pl.ds(pl.multiple_of(base * nck, nck), chunk * nck)]
    pltpu.make_async_copy(whole, whole, sem).wait()


def _gather_rows(idx, src_a, src_b, nck):
    n_out = idx.shape[0]
    chunk = math.gcd(GATHER_CHUNK, n_out)
    d = src_a.shape[1]
    kern = functools.partial(_gather_kernel, n_a=src_a.shape[0] // nck, chunk=chunk, nck=nck)
    return pl.pallas_call(
        kern,
        grid_spec=pltpu.PrefetchScalarGridSpec(
            num_scalar_prefetch=1,
            grid=(n_out // chunk,),
            in_specs=[pl.BlockSpec(memory_space=pl.ANY), pl.BlockSpec(memory_space=pl.ANY)],
            out_specs=pl.BlockSpec(memory_space=pl.ANY),
            scratch_shapes=[pltpu.SemaphoreType.DMA(())]),
        out_shape=jax.ShapeDtypeStruct((n_out * nck, d), src_a.dtype),
        compiler_params=_cparams(("arbitrary",)),
        name="gather_rows",
    )(idx, src_a, src_b)


def _expert_up_kernel(ie_ref, ij_ref, ib_ref, ioj_ref, inew_ref, n_ref, x_ref, wg_ref, wu_ref, o_ref,
                      wgb_scr, wub_scr, xb_scr):
    i = pl.program_id(0)

    @pl.when(i >= n_ref[0])
    def _():
        o_ref[...] = jnp.zeros(o_ref.shape, o_ref.dtype)

    @pl.when(i < n_ref[0])
    def _():
        @pl.when(inew_ref[i] == 1)
        def _():
            wgb_scr[...] = wg_ref[0, 0].astype(BF16)
            wub_scr[...] = wu_ref[0, 0].astype(BF16)

        bm, d = xb_scr.shape
        nck = d // LANES
        for c in range(nck):
            xb_scr[:, c * LANES:(c + 1) * LANES] = _load_token_major(x_ref, bm, c, nck).astype(BF16)
        xb = xb_scr[...]
        a = jnp.dot(xb, wgb_scr[...], preferred_element_type=F32)
        u = jnp.dot(xb, wub_scr[...], preferred_element_type=F32)
        o_ref[...] = (a * jax.nn.sigmoid(a) * u).astype(BF16)


def _expert_up(plan, x_buf, w_gate, w_up, layer):
    d, de = w_gate.shape[2], w_gate.shape[3]
    nck = d // LANES
    n_rows = x_buf.shape[0] // nck
    tn = de // 2
    bm = MOE_BLOCK_ROWS
    n_items = plan["item_e"].shape[0]
    wspec = pl.BlockSpec((1, 1, d, tn), lambda i, ie, ij, ib, ioj, inew, n: (layer, ie[i], 0, ij[i]))
    return pl.pallas_call(
        _expert_up_kernel,
        grid_spec=pltpu.PrefetchScalarGridSpec(
            num_scalar_prefetch=6,
            grid=(n_items,),
            in_specs=[pl.BlockSpec((bm * nck, LANES), lambda i, ie, ij, ib, ioj, inew, n: (ib[i], 0)),
                      wspec, wspec],
            out_specs=pl.BlockSpec((bm, tn), lambda i, ie, ij, ib, ioj, inew, n: (ib[i], ioj[i])),
            scratch_shapes=[pltpu.VMEM((d, tn), BF16), pltpu.VMEM((d, tn), BF16), pltpu.VMEM((bm, d), BF16)]),
        out_shape=jax.ShapeDtypeStruct((n_rows, de), BF16),
        compiler_params=_cparams(("arbitrary",)),
        name="expert_up",
    )(plan["item_e"], plan["item_j"], plan["item_b"], plan["item_oj"], plan["item_new"], plan["n_items"],
      x_buf, w_gate, w_up)


def _expert_down_kernel(be_ref, bb_ref, bnew_ref, n_ref, h_ref, wd_ref, o_ref, wdb_scr):
    i = pl.program_id(0)

    @pl.when(i >= n_ref[0])
    def _():
        o_ref[...] = jnp.zeros(o_ref.shape, o_ref.dtype)

    @pl.when(i < n_ref[0])
    def _():
        @pl.when(bnew_ref[i] == 1)
        def _():
            wdb_scr[...] = wd_ref[0, 0].astype(BF16)

        _store_token_major(o_ref, jnp.dot(h_ref[...], wdb_scr[...], preferred_element_type=F32))


def _expert_down(plan, h_buf, w_down, layer):
    n_rows, de = h_buf.shape
    d = w_down.shape[3]
    nck = d // LANES
    bm = MOE_BLOCK_ROWS
    n_blocks = plan["blk_e"].shape[0]
    return pl.pallas_call(
        _expert_down_kernel,
        grid_spec=pltpu.PrefetchScalarGridSpec(
            num_scalar_prefetch=4,
            grid=(n_blocks,),
            in_specs=[pl.BlockSpec((bm, de), lambda i, be, bb, bnew, n: (bb[i], 0)),
                      pl.BlockSpec((1, 1, de, d), lambda i, be, bb, bnew, n: (layer, be[i], 0, 0))],
            out_specs=pl.BlockSpec((bm * nck, LANES), lambda i, be, bb, bnew, n: (bb[i], 0)),
            scratch_shapes=[pltpu.VMEM((de, d), BF16)]),
        out_shape=jax.ShapeDtypeStruct((n_rows * nck, LANES), F32),
        compiler_params=_cparams(("arbitrary",)),
        name="expert_down",
    )(plan["blk_e"], plan["blk_b"], plan["blk_new"], plan["n_blocks"], h_buf, w_down)


def _moe_plan(expert, n_experts):
    bm = MOE_BLOCK_ROWS
    n_tok = expert.shape[0]
    n_asg = n_tok * TOP_K
    e_flat = expert.reshape(-1)
    onehot = (e_flat[:, None] == jnp.arange(n_experts, dtype=jnp.int32)[None, :]).astype(jnp.int32)
    csum = jnp.cumsum(onehot, axis=0)
    counts = csum[-1]
    rank = jnp.take_along_axis(csum, e_flat[:, None], axis=1)[:, 0] - 1
    nblk = (counts + bm - 1) // bm
    bend = jnp.cumsum(nblk)
    bstart = bend - nblk
    dest = bstart[e_flat] * bm + rank
    n_blocks = (n_asg + n_experts * (bm - 1) + bm - 1) // bm
    n_rows = -(-n_blocks * bm // GATHER_CHUNK) * GATHER_CHUNK
    tok = jnp.arange(n_asg, dtype=jnp.int32) // TOP_K
    row_src = jnp.zeros((n_rows,), jnp.int32).at[dest].set(tok, unique_indices=True)
    used = bend[-1]

    def expert_of_block(b):
        return jnp.minimum(jnp.sum((bend[None, :] <= b[:, None]).astype(jnp.int32), axis=1), n_experts - 1)

    blk = jnp.arange(n_blocks, dtype=jnp.int32)
    blk_e = expert_of_block(jnp.minimum(blk, used - 1))
    blk_new = (blk == bstart[blk_e]).astype(jnp.int32)
    n_items = 2 * n_blocks
    idx = jnp.arange(n_items, dtype=jnp.int32)
    it = jnp.minimum(idx, 2 * used - 1)
    item_e = expert_of_block(it // 2)
    off = it - 2 * bstart[item_e]
    item_j = (off // nblk[item_e]).astype(jnp.int32)
    item_b = (bstart[item_e] + off % nblk[item_e]).astype(jnp.int32)
    item_new = (item_b == bstart[item_e]).astype(jnp.int32)
    spare = idx - 2 * used
    is_spare = spare >= 0
    item_b = jnp.where(is_spare, used + spare // 2, item_b)
    item_oj = jnp.where(is_spare, spare % 2, item_j)
    return dict(dest=dest, row_src=row_src, blk_e=blk_e, blk_b=blk, blk_new=blk_new,
                n_blocks=used.reshape(1).astype(jnp.int32), item_e=item_e, item_j=item_j,
                item_b=item_b, item_oj=item_oj, item_new=item_new,
                n_items=(2 * used).reshape(1).astype(jnp.int32))


def _combine_kernel(x_ref, y_ref, rt_ref, g2_ref, gf_ref, o_ref, *, final_norm):
    bb, tt, d = x_ref.shape
    rows = bb * tt
    nck = d // LANES
    rt = rt_ref[...]
    gate_a, gate_b = rt[:, 2:3], rt[:, 3:4]
    ssq = jnp.zeros((bb, tt, 1), F32)
    for c in range(nck):
        sl = slice(c * LANES, (c + 1) * LANES)
        y = (gate_a * _load_token_major(y_ref.at[0], rows, c, nck)
             + gate_b * _load_token_major(y_ref.at[1], rows, c, nck))
        x = x_ref[:, :, sl] + g2_ref[:, :, sl] * y.reshape(bb, tt, LANES)
        o_ref[:, :, sl] = x
        ssq = ssq + jnp.sum(x * x, axis=-1, keepdims=True)
    if final_norm:
        o_ref[...] = o_ref[...] * lax.rsqrt(ssq * (1.0 / d) + EPS) * gf_ref[...]


def _combine(x, y_tok, row0, route, gate, g_final, final_norm, max_rows):
    b, t, d = x.shape
    bb, tt = _row_tiling(b, t, max_rows)
    rows = bb * tt
    nt = t // tt
    nck = d // LANES
    assert row0 % rows == 0
    blk0 = row0 // rows
    xspec = pl.BlockSpec((bb, tt, d), lambda i, k: (i, k, 0))
    kern = functools.partial(_combine_kernel, final_norm=final_norm)
    return pl.pallas_call(
        kern,
        grid=(b // bb, nt),
        in_specs=[xspec,
                  pl.BlockSpec((TOP_K, rows * nck, LANES), lambda i, k: (0, blk0 + i * nt + k, 0)),
                  pl.BlockSpec((rows, LANES), lambda i, k: (i * nt + k, 0)),
                  pl.BlockSpec((bb, 1, d), lambda i, k: (i, 0, 0)),
                  pl.BlockSpec((1, 1, d), lambda i, k: (0, 0, 0))],
        out_specs=xspec,
        out_shape=jax.ShapeDtypeStruct(x.shape, F32),
        compiler_params=_cparams(("arbitrary", "arbitrary")),
        name="combine",
    )(x, y_tok, route, gate, g_final.reshape(1, 1, d))


def _layer_params(l, w):
    d = w["w_in"].shape[1]
    nh = w["w_q"].shape[1]
    da = w["ln_v_g"].shape[1]
    db = w["w_conv"].shape[2]
    n_main = 2 * da + 2 * db
    w_in = w["w_in"][l]
    w_gate = jnp.zeros((d, 2 * LANES), BF16)
    w_gate = w_gate.at[:, :nh].set(w_in[:, n_main:n_main + nh].astype(BF16))
    w_gate = w_gate.at[:, LANES:LANES + nh].set(w_in[:, n_main + nh:n_main + 2 * nh].astype(BF16))
    n_groups = w["w_r1"].shape[2]
    per_group = w["w_r2"].shape[3]
    w_r = jnp.concatenate([w["w_r1"][l], jnp.transpose(w["w_r2"][l], (1, 0, 2)).reshape(d, n_groups * per_group)],
                          axis=1)
    n_r = w_r.shape[1]
    w_r = jnp.pad(w_r, ((0, 0), (0, LANES - n_r)))
    w_r_hi = w_r.astype(BF16)
    w_r_lo = (w_r - w_r_hi.astype(F32)).astype(BF16)
    b_r = jnp.pad(jnp.concatenate([w["b_r1"][l], w["b_r2"][l].reshape(-1)]), (0, LANES - n_r)).reshape(1, LANES)
    return dict(
        g_norm1=w["g_norm1"][l], g_norm2=w["g_norm2"][l],
        w_main=w_in[:, :n_main].astype(BF16), w_gate=w_gate,
        ln_g=w["ln_v_g"][l], ln_b=w["ln_v_b"][l],
        w_conv=w["w_conv"][l], b_conv=w["b_conv"][l].reshape(1, db),
        w_q=w["w_q"][l].astype(BF16), w_k=w["w_k"][l].astype(BF16),
        w_kt=jnp.swapaxes(w["w_k"][l], 1, 2).astype(BF16), w_v=w["w_v"][l].astype(BF16),
        b_i=jnp.pad(w["b_i"][l], (0, LANES - nh)).reshape(1, LANES),
        b_f=jnp.pad(w["b_f"][l], (0, LANES - nh)).reshape(1, LANES),
        g_mh=w["g_mh"][l].reshape(1, db), k_scale=1.0 / math.sqrt(LANES),
        w_out=w["w_out"][l].astype(BF16),
        w_router=jnp.stack([w_r_hi, w_r_lo]), b_router=b_r,
        n_groups=n_groups, per_group=per_group,
    )


def _mix_weights(w_spatial, b_spatial, L, da):
    nh = w_spatial.shape[0]
    causal = jnp.tril(jnp.ones((L, L), dtype=bool))
    w = jnp.where(causal[None], w_spatial[:, :L, :L], 0.0)
    reps = TILE // L
    if reps > 1:
        eye = jnp.eye(reps, dtype=F32)
        w = jnp.einsum("ab,hts->hatbs", eye, w).reshape(nh, TILE, TILE)
    bias = jnp.tile(jnp.transpose(b_spatial[:, :L]), (reps, 1))
    bias_full = jnp.repeat(bias, da // nh, axis=1)
    return w.astype(BF16), bias_full


def kernel(x_prompt, x_sample, c_prompt, c_sample, state_mlstm_C, state_mlstm_n, state_mlstm_m, state_conv, w_ada, b_ada, g_norm1, g_norm2, w_in, ln_v_g, ln_v_b, w_spatial, b_spatial, w_conv, b_conv, w_q, w_k, w_v, b_i, b_f, g_mh, w_out, w_r1, b_r1, w_r2, b_r2, w_e_gate, w_e_up, w_e_down, g_final):
    weights = dict(w_in=w_in, ln_v_g=ln_v_g, ln_v_b=ln_v_b, w_conv=w_conv, b_conv=b_conv, w_q=w_q, w_k=w_k,
                   w_v=w_v, b_i=b_i, b_f=b_f, g_mh=g_mh, w_out=w_out, w_r1=w_r1, b_r1=b_r1, w_r2=w_r2, b_r2=b_r2,
                   g_norm1=g_norm1, g_norm2=g_norm2)
    depth = w_in.shape[0]
    bp, tp, d = x_prompt.shape
    bs, ts, _ = x_sample.shape
    nh = w_q.shape[1]
    da = ln_v_g.shape[1]
    db = w_conv.shape[2]
    n_mod = w_ada.shape[2] // d
    n_experts = w_e_gate.shape[1]
    n_p, n_s = bp * tp, bs * ts
    assert da == db and da % LANES == 0 and nh * LANES == db and tp % TILE == 0 and TILE % ts == 0
    assert ts >= CONV_W - 1 and ts % SUBLANES == 0 and n_s % TILE == 0
    xm_col, zo_col = 2 * da // db, 2 * da // db + 1

    mod = _ada(jnp.concatenate([c_prompt, c_sample], axis=0), w_ada, b_ada).reshape(depth, bp + bs, n_mod, 1, d)

    xp, xs = x_prompt, x_sample
    outs = {k: [] for k in ("cp", "np", "mp", "bp", "cs", "ns", "ms", "bs", "vs")}
    for l in range(depth):
        p = _layer_params(l, weights)
        mods_p = [mod[l, :bp, i] for i in range(n_mod)]
        mods_s = [mod[l, bp:, i] for i in range(n_mod)]

        proj_p, gates_p = _inproj(xp, mods_p[1], mods_p[0], p["g_norm1"], p["w_main"], p["w_gate"], 1024)
        w_mix_p, bias_p = _mix_weights(w_spatial[l], b_spatial[l], min(tp, TILE), da)
        (ya_p,) = _grpa(proj_p, p["ln_g"], p["ln_b"], w_mix_p, bias_p, da, False)
        yb_p, c_p, n_pn, m_p = _mlstm_prompt(proj_p, gates_p, p, bp, tp, db, nh, xm_col, zo_col)
        outs["cp"].append(c_p)
        outs["np"].append(n_pn.reshape(bp, nh, LANES))
        outs["mp"].append(m_p[:, 0, :nh])
        outs["bp"].append(proj_p.reshape(bp, tp, -1)[:, tp - (CONV_W - 1):, 2 * da:2 * da + db])

        proj_s, gates_s = _inproj(xs, mods_s[1], mods_s[0], p["g_norm1"], p["w_main"], p["w_gate"], 512)
        w_mix_s, bias_s = _mix_weights(w_spatial[l], b_spatial[l], min(ts, TILE), da)
        ya_s, v_s = _grpa(proj_s, p["ln_g"], p["ln_b"], w_mix_s, bias_s, da, True)
        prev_rows = jnp.pad(state_conv[l], ((0, 0), (ts - (CONV_W - 1), 0), (0, 0))).reshape(n_s, db)
        n0_rows = jnp.repeat(state_mlstm_n[l].reshape(bs, db), ts, axis=0)
        m0_rows = jnp.pad(jnp.repeat(state_mlstm_m[l], ts, axis=0), ((0, 0), (0, LANES - nh)))
        yb_s, c_s, n_rows_s, m_rows_s = _mlstm_sample(proj_s, gates_s, prev_rows, state_mlstm_C, l, n0_rows, m0_rows,
                                                      p, math.gcd(ts, TILE), db, nh, xm_col, zo_col)
        outs["cs"].append(c_s)
        outs["ns"].append(n_rows_s.reshape(bs, ts, nh, LANES)[:, ts - 1])
        outs["ms"].append(m_rows_s.reshape(bs, ts, LANES)[:, ts - 1, :nh])
        outs["bs"].append(proj_s.reshape(bs, ts, -1)[:, ts - (CONV_W - 1):, 2 * da:2 * da + db])
        outs["vs"].append(v_s.reshape(bs, ts, da))

        rargs = (p["w_router"], p["b_router"], p["n_groups"], p["per_group"])
        xp, h2_p, rt_p = _outproj(xp, ya_p, yb_p, p["w_out"], mods_p[2], mods_p[4], mods_p[3], p["g_norm2"], *rargs, 256)
        xs, h2_s, rt_s = _outproj(xs, ya_s, yb_s, p["w_out"], mods_s[2], mods_s[4], mods_s[3], p["g_norm2"], *rargs, 256)

        route = jnp.concatenate([rt_p, rt_s], axis=0)
        plan = _moe_plan(route[:, :TOP_K].astype(jnp.int32), n_experts)
        nck = d // LANES
        x_buf = _gather_rows(plan["row_src"], h2_p, h2_s, nck)
        h_buf = _expert_up(plan, x_buf, w_e_gate, w_e_up, l)
        y_buf = _expert_down(plan, h_buf, w_e_down, l)
        dest_t = jnp.transpose(plan["dest"].reshape(n_p + n_s, TOP_K)).reshape(-1)
        y_tok = _gather_rows(dest_t, y_buf, y_buf, nck).reshape(TOP_K, (n_p + n_s) * nck, LANES)
        last = l == depth - 1
        xp = _combine(xp, y_tok, 0, rt_p, mods_p[5], g_final, last, 512)
        xs = _combine(xs, y_tok, n_p, rt_s, mods_s[5], g_final, last, 256)

    st = jnp.stack
    return (xp, xs, st(outs["cp"]), st(outs["np"]), st(outs["mp"]), st(outs["bp"]),
            st(outs["cs"]), st(outs["ns"]), st(outs["ms"]), st(outs["bs"]), st(outs["vs"]))
```

```python
import functools
import math

import jax
import jax.numpy as jnp
from jax import lax
from jax.experimental import pallas as pl
from jax.experimental.pallas import tpu as pltpu

F32 = jnp.float32
BF16 = jnp.bfloat16
EPS = 1e-6
LANES = 128
SUBLANES = 8
TILE = 128
CONV_W = 4
TOP_K = 2
VMEM_LIMIT_BYTES = 56 * 1024 * 1024
MOE_BLOCK_ROWS = 256
GATHER_CHUNK = 512
NEG_INF = float("-inf")


def _cparams(sem):
    return pltpu.CompilerParams(dimension_semantics=sem, vmem_limit_bytes=VMEM_LIMIT_BYTES)


def _split3(x):
    x1 = x.astype(BF16)
    r1 = x - x1.astype(F32)
    x2 = r1.astype(BF16)
    r2 = r1 - x2.astype(F32)
    return x1, x2, r2.astype(BF16)


def _ada_kernel(c_ref, w_ref, b_ref, o_ref):
    c = c_ref[...]
    a = c * jax.nn.sigmoid(c)
    o_ref[0] = jnp.dot(a.astype(BF16), w_ref[0].astype(BF16), preferred_element_type=F32) + b_ref[0]


def _ada(c_all, w_ada, b_ada):
    depth, d, n = w_ada.shape
    bc = c_all.shape[0]
    tn = min(1024, n)
    return pl.pallas_call(
        _ada_kernel,
        grid=(depth, n // tn),
        in_specs=[pl.BlockSpec((bc, d), lambda l, j: (0, 0)),
                  pl.BlockSpec((1, d, tn), lambda l, j: (l, 0, j)),
                  pl.BlockSpec((1, 1, tn), lambda l, j: (l, 0, j))],
        out_specs=pl.BlockSpec((1, bc, tn), lambda l, j: (l, 0, j)),
        out_shape=jax.ShapeDtypeStruct((depth, bc, n), F32),
        compiler_params=_cparams(("arbitrary", "arbitrary")),
        name="ada",
    )(c_all, w_ada, b_ada.reshape(depth, 1, n))


def _inproj_kernel(x_ref, sc_ref, sh_ref, g_ref, w_ref, wg_ref, o_ref, og_ref, hn_ref):
    @pl.when(pl.program_id(2) == 0)
    def _():
        x = x_ref[...]
        ms = jnp.mean(x * x, axis=-1, keepdims=True)
        h = x * lax.rsqrt(ms + EPS) * g_ref[...]
        h = h * (1.0 + sc_ref[...]) + sh_ref[...]
        hb = h.reshape(hn_ref.shape).astype(BF16)
        hn_ref[...] = hb
        og_ref[...] = jnp.dot(hb, wg_ref[...], preferred_element_type=F32)

    o_ref[...] = jnp.dot(hn_ref[...], w_ref[...], preferred_element_type=F32)


def _row_tiling(b, t, max_rows):
    if t >= max_rows:
        assert t % max_rows == 0
        return 1, max_rows
    bb = min(b, max_rows // t)
    assert b % bb == 0 and t % SUBLANES == 0
    return bb, t


def _inproj(x, scale, shift, g_norm, w_main, w_gate, max_rows):
    b, t, d = x.shape
    n = w_main.shape[1]
    ng = w_gate.shape[1]
    bb, tt = _row_tiling(b, t, max_rows)
    rows = bb * tt
    tn = min(512, n)
    nt = t // tt
    return pl.pallas_call(
        _inproj_kernel,
        grid=(b // bb, nt, n // tn),
        in_specs=[pl.BlockSpec((bb, tt, d), lambda i, k, j: (i, k, 0)),
                  pl.BlockSpec((bb, 1, d), lambda i, k, j: (i, 0, 0)),
                  pl.BlockSpec((bb, 1, d), lambda i, k, j: (i, 0, 0)),
                  pl.BlockSpec((1, 1, d), lambda i, k, j: (0, 0, 0)),
                  pl.BlockSpec((d, tn), lambda i, k, j: (0, j)),
                  pl.BlockSpec((d, ng), lambda i, k, j: (0, 0))],
        out_specs=[pl.BlockSpec((rows, tn), lambda i, k, j: (i * nt + k, j)),
                   pl.BlockSpec((rows, ng), lambda i, k, j: (i * nt + k, 0))],
        out_shape=[jax.ShapeDtypeStruct((b * t, n), F32), jax.ShapeDtypeStruct((b * t, ng), F32)],
        scratch_shapes=[pltpu.VMEM((rows, d), BF16)],
        compiler_params=_cparams(("arbitrary", "arbitrary", "arbitrary")),
        name="inproj",
    )(x, scale, shift, g_norm.reshape(1, 1, d), w_main, w_gate)


def _grpa_kernel(u_ref, v_ref, lg_ref, lb_ref, w_ref, bias_ref, ya_ref, *maybe_v_out):
    rows, da = u_ref.shape
    nh = w_ref.shape[0]
    g = jax.nn.gelu(v_ref[...])
    mu = jnp.mean(g, axis=-1, keepdims=True)
    gc = g - mu
    var = jnp.mean(gc * gc, axis=-1, keepdims=True)
    v = gc * lax.rsqrt(var + EPS) * lg_ref[...] + lb_ref[...]
    if maybe_v_out:
        maybe_v_out[0][...] = v
    vb = v.astype(BF16)
    for c in range(rows // TILE):
        r0 = c * TILE
        for h in range(nh):
            c0 = h * LANES
            z = jnp.dot(w_ref[h], vb[r0:r0 + TILE, c0:c0 + LANES], preferred_element_type=F32)
            u = jax.nn.gelu(u_ref[r0:r0 + TILE, c0:c0 + LANES])
            y = u * (z + bias_ref[:, c0:c0 + LANES])
            ya_ref[r0:r0 + TILE, c0:c0 + LANES] = y.astype(BF16)


def _grpa(proj, ln_g, ln_b, w_mix, bias_full, da, want_v):
    n_rows = proj.shape[0]
    rows = min(512, n_rows)
    nh = w_mix.shape[0]
    out_specs = [pl.BlockSpec((rows, da), lambda i: (i, 0))]
    out_shape = [jax.ShapeDtypeStruct((n_rows, da), BF16)]
    if want_v:
        out_specs.append(pl.BlockSpec((rows, da), lambda i: (i, 0)))
        out_shape.append(jax.ShapeDtypeStruct((n_rows, da), F32))
    return pl.pallas_call(
        _grpa_kernel,
        grid=(n_rows // rows,),
        in_specs=[pl.BlockSpec((rows, da), lambda i: (i, 0)),
                  pl.BlockSpec((rows, da), lambda i: (i, 1)),
                  pl.BlockSpec((1, da), lambda i: (0, 0)),
                  pl.BlockSpec((1, da), lambda i: (0, 0)),
                  pl.BlockSpec((nh, TILE, TILE), lambda i: (0, 0, 0)),
                  pl.BlockSpec((TILE, da), lambda i: (0, 0))],
        out_specs=out_specs,
        out_shape=out_shape,
        compiler_params=_cparams(("arbitrary",)),
        name="grpa",
    )(proj, proj, ln_g.reshape(1, da), ln_b.reshape(1, da), w_mix, bias_full)


def _group_last(x, L):
    if L == TILE:
        return jnp.broadcast_to(x[TILE - 1:TILE, :], x.shape)
    g = TILE // L
    x3 = x.reshape(g, L, x.shape[1])
    return jnp.broadcast_to(x3[:, L - 1:L, :], x3.shape).reshape(x.shape)


def _group_sum(x, L):
    if L == TILE:
        return jnp.broadcast_to(jnp.sum(x, axis=0, keepdims=True), x.shape)
    g = TILE // L
    x3 = x.reshape(g, L, x.shape[1])
    return jnp.broadcast_to(jnp.sum(x3, axis=1, keepdims=True), x3.shape).reshape(x.shape)


def _mlstm_kernel(*refs, L, carry, nh, k_scale):
    if carry:
        (xm_ref, zo_ref, g_ref, wconv_ref, bconv_ref, wq_ref, wk_ref, wkt_ref, wv_ref, bi_ref, bf_ref, gmh_ref,
         yb_ref, c_ref, n_ref, m_ref, prev_scr) = refs
        c_in_ref = c_ref
        first = pl.program_id(1) == 0

        @pl.when(first)
        def _():
            c_ref[...] = jnp.zeros(c_ref.shape, F32)
            n_ref[...] = jnp.zeros(n_ref.shape, F32)
            m_ref[...] = jnp.zeros(m_ref.shape, F32)
            prev_scr[...] = jnp.zeros(prev_scr.shape, F32)

        prev = prev_scr[...]
        n0_rows = jnp.broadcast_to(n_ref[0], (TILE, n_ref.shape[2]))
        m0 = jnp.broadcast_to(m_ref[0], (TILE, LANES))
    else:
        (xm_ref, zo_ref, g_ref, prev_ref, c_layer_ref, n0_ref, m0_ref,
         wconv_ref, bconv_ref, wq_ref, wk_ref, wkt_ref, wv_ref, bi_ref, bf_ref, gmh_ref,
         yb_ref, c_ref, n_ref, m_ref, q_scr, qc_scr, dec_scr) = refs
        c_in_ref = c_layer_ref.at[0]
        prev = prev_ref[...]
        n0_rows = n0_ref[...]
        m0 = m0_ref[...]

    groups = TILE // L
    x = xm_ref[...]
    row_in_group = lax.broadcasted_iota(jnp.int32, (TILE, 1), 0) % L

    acc = bconv_ref[...] + wconv_ref[CONV_W - 1:CONV_W, :] * x
    for j in range(1, CONV_W):
        from_prev = pltpu.roll(prev, (j - L) % TILE, 0)
        shifted = jnp.where(row_in_group < j, from_prev, pltpu.roll(x, j, 0))
        acc = acc + wconv_ref[CONV_W - 1 - j:CONV_W - j, :] * shifted
    xc = acc * jax.nn.sigmoid(acc)
    if carry:
        prev_scr[...] = x

    gates = g_ref[...]
    gi = gates[:, :LANES] + bi_ref[...]
    gf = gates[:, LANES:] + bf_ref[...]
    lf = jnp.minimum(gf, 0.0) - jnp.log1p(jnp.exp(-jnp.abs(gf)))

    r_i = lax.broadcasted_iota(jnp.int32, (TILE, TILE), 0)
    c_i = lax.broadcasted_iota(jnp.int32, (TILE, TILE), 1)
    mask = (c_i <= r_i) & ((c_i // L) == (r_i // L)) if L != TILE else (c_i <= r_i)
    mask_b = jnp.where(mask, 1.0, 0.0).astype(BF16)
    lf1, lf2, lf3 = _split3(lf)
    cum = (jnp.dot(mask_b, lf1, preferred_element_type=F32) + jnp.dot(mask_b, lf2, preferred_element_type=F32)
           + jnp.dot(mask_b, lf3, preferred_element_type=F32))
    cum_t = cum.T
    gi_t = gi.T
    log_inter = cum + m0
    lane = lax.broadcasted_iota(jnp.int32, (TILE, LANES), 1)

    def d_matrix(h):
        d = cum[:, h:h + 1] - cum_t[h:h + 1, :] + gi_t[h:h + 1, :]
        return jnp.where(mask, d, NEG_INF)

    mt = jnp.zeros((TILE, LANES), F32)
    for h in range(nh):
        mt_h = jnp.maximum(log_inter[:, h:h + 1], jnp.max(d_matrix(h), axis=1, keepdims=True))
        mt = jnp.where(lane == h, mt_h, mt)

    m_last = _group_last(mt, L)
    w_inter = jnp.exp(log_inter - mt)
    w_last = jnp.exp(_group_last(cum, L) - cum + gi - m_last)
    w_last_t = w_last.T
    inv_floor = jnp.exp(-mt)

    if not carry:
        dec_scr[...] = w_inter

    for h in range(nh):
        c0 = h * LANES
        xc_h = xc[:, c0:c0 + LANES].astype(BF16)
        q = jnp.dot(xc_h, wq_ref[h], preferred_element_type=F32)
        k = jnp.dot(xc_h, wk_ref[h], preferred_element_type=F32) * k_scale
        k_t = lax.dot_general(wkt_ref[h], xc_h, (((1,), (1,)), ((), ())), preferred_element_type=F32) * k_scale
        v = jnp.dot(x[:, c0:c0 + LANES].astype(BF16), wv_ref[h], preferred_element_type=F32)
        qb = q.astype(BF16)
        vb = v.astype(BF16)
        mt_h = mt[:, h:h + 1]
        wi_h = w_inter[:, h:h + 1]
        w_intra = jnp.exp(d_matrix(h) - mt_h)
        s = jnp.dot(qb, k_t.astype(BF16), preferred_element_type=F32) * w_intra

        if carry:
            qc = jnp.dot(qb, c_in_ref[0, h].astype(BF16), preferred_element_type=F32)
        else:
            q_scr[...] = q

            def qc_body(g, _, h=h):
                r0 = pl.multiple_of(g * L, L)
                qg = q_scr[pl.ds(r0, L), :].astype(BF16)
                qc_scr[pl.ds(r0, L), :] = jnp.dot(qg, c_in_ref[g, h].astype(BF16), preferred_element_type=F32)
                return 0

            lax.fori_loop(0, groups, qc_body, 0)
            qc = qc_scr[...]

        num = wi_h * qc + jnp.dot(s.astype(BF16), vb, preferred_element_type=F32)
        qn = jnp.sum(q * n0_rows[:, c0:c0 + LANES], axis=1, keepdims=True)
        den = wi_h * qn + jnp.sum(s, axis=1, keepdims=True)
        hh = num / jnp.maximum(jnp.abs(den), inv_floor[:, h:h + 1])
        hh = hh * lax.rsqrt(jnp.mean(hh * hh, axis=1, keepdims=True) + EPS) * gmh_ref[:, c0:c0 + LANES]
        yb_ref[:, c0:c0 + LANES] = (jax.nn.sigmoid(zo_ref[:, c0:c0 + LANES]) * hh).astype(BF16)

        kw_t = k_t * w_last_t[h:h + 1, :]
        n_rows = wi_h * n0_rows[:, c0:c0 + LANES] + _group_sum(w_last[:, h:h + 1] * k, L)
        if carry:
            decay = jnp.broadcast_to(w_inter[TILE - 1:TILE, h:h + 1], (LANES, LANES))
            c_ref[0, h] = decay * c_in_ref[0, h] + jnp.dot(kw_t.astype(BF16), vb, preferred_element_type=F32)
            n_ref[0, :, c0:c0 + LANES] = n_rows[TILE - 1:TILE, :]
        else:
            n_ref[:, c0:c0 + LANES] = n_rows

            def c_body(g, _, h=h, kw_t=kw_t, vb=vb):
                kw_g = jnp.where((c_i // L) == g, kw_t, 0.0).astype(BF16)
                upd = jnp.dot(kw_g, vb, preferred_element_type=F32)
                drow = dec_scr[pl.ds(g * L + L - 1, 1), :]
                decay = jnp.broadcast_to(drow[:, h:h + 1], (LANES, LANES))
                c_ref[g, h] = decay * c_in_ref[g, h] + upd
                return 0

            lax.fori_loop(0, groups, c_body, 0)

    if carry:
        m_ref[0] = mt[TILE - 1:TILE, :]
    else:
        m_ref[...] = mt


def _mlstm_weights(p):
    return [p["w_conv"], p["b_conv"], p["w_q"], p["w_k"], p["w_kt"], p["w_v"], p["b_i"], p["b_f"], p["g_mh"]]


def _mlstm_weight_specs(nh, db, nidx):
    z2 = (lambda i, c: (0, 0)) if nidx == 2 else (lambda i: (0, 0))
    z3 = (lambda i, c: (0, 0, 0)) if nidx == 2 else (lambda i: (0, 0, 0))
    hw = pl.BlockSpec((nh, LANES, LANES), z3)
    return [pl.BlockSpec((CONV_W, db), z2), pl.BlockSpec((1, db), z2), hw, hw, hw, hw,
            pl.BlockSpec((1, LANES), z2), pl.BlockSpec((1, LANES), z2), pl.BlockSpec((1, db), z2)]


def _mlstm_prompt(proj, gates, p, b, t, db, nh, xm_col, zo_col):
    nt = t // TILE
    kern = functools.partial(_mlstm_kernel, L=TILE, carry=True, nh=nh, k_scale=p["k_scale"])
    return pl.pallas_call(
        kern,
        grid=(b, nt),
        in_specs=[pl.BlockSpec((TILE, db), lambda i, c: (i * nt + c, xm_col)),
                  pl.BlockSpec((TILE, db), lambda i, c: (i * nt + c, zo_col)),
                  pl.BlockSpec((TILE, 2 * LANES), lambda i, c: (i * nt + c, 0))] + _mlstm_weight_specs(nh, db, 2),
        out_specs=[pl.BlockSpec((TILE, db), lambda i, c: (i * nt + c, 0)),
                   pl.BlockSpec((1, nh, LANES, LANES), lambda i, c: (i, 0, 0, 0)),
                   pl.BlockSpec((1, 1, db), lambda i, c: (i, 0, 0)),
                   pl.BlockSpec((1, 1, LANES), lambda i, c: (i, 0, 0))],
        out_shape=[jax.ShapeDtypeStruct((b * t, db), BF16),
                   jax.ShapeDtypeStruct((b, nh, LANES, LANES), F32),
                   jax.ShapeDtypeStruct((b, 1, db), F32),
                   jax.ShapeDtypeStruct((b, 1, LANES), F32)],
        scratch_shapes=[pltpu.VMEM((TILE, db), F32)],
        compiler_params=_cparams(("arbitrary", "arbitrary")),
        name="mlstm_prompt",
    )(proj, proj, gates, *_mlstm_weights(p))


def _mlstm_sample(proj, gates, prev_rows, c0, layer, n0_rows, m0_rows, p, L, db, nh, xm_col, zo_col):
    n_rows = proj.shape[0]
    groups = TILE // L
    kern = functools.partial(_mlstm_kernel, L=L, carry=False, nh=nh, k_scale=p["k_scale"])
    row_spec = pl.BlockSpec((TILE, db), lambda i: (i, 0))
    return pl.pallas_call(
        kern,
        grid=(n_rows // TILE,),
        in_specs=[pl.BlockSpec((TILE, db), lambda i: (i, xm_col)),
                  pl.BlockSpec((TILE, db), lambda i: (i, zo_col)),
                  pl.BlockSpec((TILE, 2 * LANES), lambda i: (i, 0)),
                  row_spec,
                  pl.BlockSpec((1, groups, nh, LANES, LANES), lambda i: (layer, i, 0, 0, 0)),
                  row_spec,
                  pl.BlockSpec((TILE, LANES), lambda i: (i, 0))] + _mlstm_weight_specs(nh, db, 1),
        out_specs=[row_spec,
                   pl.BlockSpec((groups, nh, LANES, LANES), lambda i: (i, 0, 0, 0)),
                   row_spec,
                   pl.BlockSpec((TILE, LANES), lambda i: (i, 0))],
        out_shape=[jax.ShapeDtypeStruct((n_rows, db), BF16),
                   jax.ShapeDtypeStruct(c0.shape[1:], F32),
                   jax.ShapeDtypeStruct((n_rows, db), F32),
                   jax.ShapeDtypeStruct((n_rows, LANES), F32)],
        scratch_shapes=[pltpu.VMEM((TILE, LANES), F32), pltpu.VMEM((TILE, LANES), F32),
                        pltpu.VMEM((TILE, LANES), F32)],
        compiler_params=_cparams(("arbitrary",)),
        name="mlstm_sample",
    )(proj, proj, gates, prev_rows, c0, n0_rows, m0_rows, *_mlstm_weights(p))


def _store_token_major(ref, val):
    n, d = val.shape
    nck = d // LANES
    for c in range(nck):
        ref[pl.ds(c, n, stride=nck), :] = val[:, c * LANES:(c + 1) * LANES]


def _load_token_major(ref, n, c, nck):
    return ref[pl.ds(c, n, stride=nck), :]


def _route(lg, n_groups, per_group):
    lane_i = lax.broadcasted_iota(jnp.int32, lg.shape, 1)
    lane = lane_i.astype(F32)
    far = float(LANES)
    is_g = lane_i < n_groups
    lgg = jnp.where(is_g, lg, NEG_INF)
    mg = jnp.max(lgg, axis=1, keepdims=True)
    g_sel = jnp.min(jnp.where(lgg == mg, lane, far), axis=1, keepdims=True)
    p_g = 1.0 / jnp.sum(jnp.where(is_g, jnp.exp(lgg - mg), 0.0), axis=1, keepdims=True)
    lo = n_groups + g_sel * per_group
    sel = (lane >= lo) & (lane < lo + per_group)
    le = jnp.where(sel, lg, NEG_INF)
    m1 = jnp.max(le, axis=1, keepdims=True)
    i1 = jnp.min(jnp.where(le == m1, lane, far), axis=1, keepdims=True)
    le2 = jnp.where(lane == i1, NEG_INF, le)
    m2 = jnp.max(le2, axis=1, keepdims=True)
    i2 = jnp.min(jnp.where(le2 == m2, lane, far), axis=1, keepdims=True)
    ratio = jnp.exp(m2 - m1)
    gate1 = p_g / (1.0 + ratio)
    gate2 = p_g * ratio / (1.0 + ratio)
    out = jnp.where(lane_i == 0, i1 - n_groups, 0.0)
    out = jnp.where(lane_i == 1, i2 - n_groups, out)
    out = jnp.where(lane_i == 2, gate1, out)
    return jnp.where(lane_i == 3, gate2, out)


def _outproj_kernel(x_ref, ya_ref, yb_ref, w_ref, g1_ref, sc_ref, sh_ref, gn_ref, wr_ref, br_ref,
                    xn_ref, h2_ref, rt_ref, *, n_groups, per_group):
    da = ya_ref.shape[1]
    mix = (jnp.dot(ya_ref[...], w_ref[:da, :], preferred_element_type=F32)
           + jnp.dot(yb_ref[...], w_ref[da:, :], preferred_element_type=F32))
    x = x_ref[...] + g1_ref[...] * mix.reshape(x_ref.shape)
    xn_ref[...] = x
    ms = jnp.mean(x * x, axis=-1, keepdims=True)
    h = x * lax.rsqrt(ms + EPS) * gn_ref[...]
    rows = ya_ref.shape[0]
    h = (h * (1.0 + sc_ref[...]) + sh_ref[...]).reshape(rows, x_ref.shape[2])
    _store_token_major(h2_ref, h)
    h_hi = h.astype(BF16)
    h_lo = (h - h_hi.astype(F32)).astype(BF16)
    lg = (jnp.dot(h_hi, wr_ref[0], preferred_element_type=F32) + jnp.dot(h_lo, wr_ref[0], preferred_element_type=F32)
          + jnp.dot(h_hi, wr_ref[1], preferred_element_type=F32) + br_ref[...])
    rt_ref[...] = _route(lg, n_groups, per_group)


def _outproj(x, ya, yb, w_out, gate, scale, shift, g_norm, w_router, b_router, n_groups, per_group, max_rows):
    b, t, d = x.shape
    bb, tt = _row_tiling(b, t, max_rows)
    rows = bb * tt
    nt = t // tt
    da = ya.shape[1]
    xspec = pl.BlockSpec((bb, tt, d), lambda i, k: (i, k, 0))
    mspec = pl.BlockSpec((bb, 1, d), lambda i, k: (i, 0, 0))
    kern = functools.partial(_outproj_kernel, n_groups=n_groups, per_group=per_group)
    return pl.pallas_call(
        kern,
        grid=(b // bb, nt),
        in_specs=[xspec,
                  pl.BlockSpec((rows, da), lambda i, k: (i * nt + k, 0)),
                  pl.BlockSpec((rows, ya.shape[1]), lambda i, k: (i * nt + k, 0)),
                  pl.BlockSpec(w_out.shape, lambda i, k: (0, 0)),
                  mspec, mspec, mspec,
                  pl.BlockSpec((1, 1, d), lambda i, k: (0, 0, 0)),
                  pl.BlockSpec(w_router.shape, lambda i, k: (0, 0, 0)),
                  pl.BlockSpec((1, LANES), lambda i, k: (0, 0))],
        out_specs=[xspec,
                   pl.BlockSpec((rows * (d // LANES), LANES), lambda i, k: (i * nt + k, 0)),
                   pl.BlockSpec((rows, LANES), lambda i, k: (i * nt + k, 0))],
        out_shape=[jax.ShapeDtypeStruct(x.shape, F32),
                   jax.ShapeDtypeStruct((b * t * (d // LANES), LANES), F32),
                   jax.ShapeDtypeStruct((b * t, LANES), F32)],
        compiler_params=_cparams(("arbitrary", "arbitrary")),
        name="outproj",
    )(x, ya, yb, w_out, gate, scale, shift, g_norm.reshape(1, 1, d), w_router, b_router)


def _gather_kernel(idx_ref, src_a_ref, src_b_ref, out_ref, buf, sem, *, n_a, chunk, nck):
    i = pl.program_id(0)

    def issue_step(step, slot):
        def issue(r, _):
            t = idx_ref[step * chunk + r]
            dst = buf.at[slot, pl.ds(pl.multiple_of(r * nck, nck), nck)]

            @pl.when(t < n_a)
            def _():
                src = src_a_ref.at[pl.ds(pl.multiple_of(t * nck, nck), nck)]
                pltpu.make_async_copy(src, dst, sem.at[slot]).start()

            @pl.when(t >= n_a)
            def _():
                src = src_b_ref.at[pl.ds(pl.multiple_of((t - n_a) * nck, nck), nck)]
                pltpu.make_async_copy(src, dst, sem.at[slot]).start()

            return 0

        lax.fori_loop(0, chunk, issue, 0)

    @pl.when(i == 0)
    def _():
        issue_step(0, 0)

    @pl.when(i + 1 < pl.num_programs(0))
    def _():
        issue_step(i + 1, (i + 1) % 2)

    slot = i % 2
    pltpu.make_async_copy(buf.at[slot], buf.at[slot], sem.at[slot]).wait()
    out_ref[...] = buf[slot]


def _gather_rows(idx, src_a, src_b, nck):
    n_out = idx.shape[0]
    chunk = math.gcd(GATHER_CHUNK, n_out)
    d = src_a.shape[1]
    kern = functools.partial(_gather_kernel, n_a=src_a.shape[0] // nck, chunk=chunk, nck=nck)
    return pl.pallas_call(
        kern,
        grid_spec=pltpu.PrefetchScalarGridSpec(
            num_scalar_prefetch=1,
            grid=(n_out // chunk,),
            in_specs=[pl.BlockSpec(memory_space=pl.ANY), pl.BlockSpec(memory_space=pl.ANY)],
            out_specs=pl.BlockSpec((chunk * nck, d), lambda i, idx: (i, 0)),
            scratch_shapes=[pltpu.VMEM((2, chunk * nck, d), src_a.dtype), pltpu.SemaphoreType.DMA((2,))]),
        out_shape=jax.ShapeDtypeStruct((n_out * nck, d), src_a.dtype),
        compiler_params=_cparams(("arbitrary",)),
        name="gather_rows",
    )(idx, src_a, src_b)


def _expert_up_kernel(ie_ref, ij_ref, ib_ref, ioj_ref, inew_ref, n_ref, x_ref, wg_ref, wu_ref, o_ref,
                      wgb_scr, wub_scr, xb_scr):
    i = pl.program_id(0)

    @pl.when(i >= n_ref[0])
    def _():
        o_ref[...] = jnp.zeros(o_ref.shape, o_ref.dtype)

    @pl.when(i < n_ref[0])
    def _():
        @pl.when(inew_ref[i] == 1)
        def _():
            wgb_scr[...] = wg_ref[0, 0].astype(BF16)
            wub_scr[...] = wu_ref[0, 0].astype(BF16)

        bm, d = xb_scr.shape
        nck = d // LANES
        for c in range(nck):
            xb_scr[:, c * LANES:(c + 1) * LANES] = _load_token_major(x_ref, bm, c, nck).astype(BF16)
        xb = xb_scr[...]
        a = jnp.dot(xb, wgb_scr[...], preferred_element_type=F32)
        u = jnp.dot(xb, wub_scr[...], preferred_element_type=F32)
        o_ref[...] = (a * jax.nn.sigmoid(a) * u).astype(BF16)


def _expert_up(plan, x_buf, w_gate, w_up, layer):
    d, de = w_gate.shape[2], w_gate.shape[3]
    nck = d // LANES
    n_rows = x_buf.shape[0] // nck
    tn = de // 2
    bm = MOE_BLOCK_ROWS
    n_items = plan["item_e"].shape[0]
    wspec = pl.BlockSpec((1, 1, d, tn), lambda i, ie, ij, ib, ioj, inew, n: (layer, ie[i], 0, ij[i]))
    return pl.pallas_call(
        _expert_up_kernel,
        grid_spec=pltpu.PrefetchScalarGridSpec(
            num_scalar_prefetch=6,
            grid=(n_items,),
            in_specs=[pl.BlockSpec((bm * nck, LANES), lambda i, ie, ij, ib, ioj, inew, n: (ib[i], 0)),
                      wspec, wspec],
            out_specs=pl.BlockSpec((bm, tn), lambda i, ie, ij, ib, ioj, inew, n: (ib[i], ioj[i])),
            scratch_shapes=[pltpu.VMEM((d, tn), BF16), pltpu.VMEM((d, tn), BF16), pltpu.VMEM((bm, d), BF16)]),
        out_shape=jax.ShapeDtypeStruct((n_rows, de), BF16),
        compiler_params=_cparams(("arbitrary",)),
        name="expert_up",
    )(plan["item_e"], plan["item_j"], plan["item_b"], plan["item_oj"], plan["item_new"], plan["n_items"],
      x_buf, w_gate, w_up)


def _expert_down_kernel(be_ref, bb_ref, bnew_ref, n_ref, h_ref, wd_ref, o_ref, wdb_scr):
    i = pl.program_id(0)

    @pl.when(i >= n_ref[0])
    def _():
        o_ref[...] = jnp.zeros(o_ref.shape, o_ref.dtype)

    @pl.when(i < n_ref[0])
    def _():
        @pl.when(bnew_ref[i] == 1)
        def _():
            wdb_scr[...] = wd_ref[0, 0].astype(BF16)

        _store_token_major(o_ref, jnp.dot(h_ref[...], wdb_scr[...], preferred_element_type=F32))


def _expert_down(plan, h_buf, w_down, layer):
    n_rows, de = h_buf.shape
    d = w_down.shape[3]
    nck = d // LANES
    bm = MOE_BLOCK_ROWS
    n_blocks = plan["blk_e"].shape[0]
    return pl.pallas_call(
        _expert_down_kernel,
        grid_spec=pltpu.PrefetchScalarGridSpec(
            num_scalar_prefetch=4,
            grid=(n_blocks,),
            in_specs=[pl.BlockSpec((bm, de), lambda i, be, bb, bnew, n: (bb[i], 0)),
                      pl.BlockSpec((1, 1, de, d), lambda i, be, bb, bnew, n: (layer, be[i], 0, 0))],
            out_specs=pl.BlockSpec((bm * nck, LANES), lambda i, be, bb, bnew, n: (bb[i], 0)),
            scratch_shapes=[pltpu.VMEM((de, d), BF16)]),
        out_shape=jax.ShapeDtypeStruct((n_rows * nck, LANES), F32),
        compiler_params=_cparams(("arbitrary",)),
        name="expert_down",
    )(plan["blk_e"], plan["blk_b"], plan["blk_new"], plan["n_blocks"], h_buf, w_down)


def _moe_plan(expert, n_experts):
    bm = MOE_BLOCK_ROWS
    n_tok = expert.shape[0]
    n_asg = n_tok * TOP_K
    e_flat = expert.reshape(-1)
    onehot = (e_flat[:, None] == jnp.arange(n_experts, dtype=jnp.int32)[None, :]).astype(jnp.int32)
    csum = jnp.cumsum(onehot, axis=0)
    counts = csum[-1]
    rank = jnp.take_along_axis(csum, e_flat[:, None], axis=1)[:, 0] - 1
    nblk = (counts + bm - 1) // bm
    bend = jnp.cumsum(nblk)
    bstart = bend - nblk
    dest = bstart[e_flat] * bm + rank
    n_blocks = (n_asg + n_experts * (bm - 1) + bm - 1) // bm
    n_rows = -(-n_blocks * bm // GATHER_CHUNK) * GATHER_CHUNK
    tok = jnp.arange(n_asg, dtype=jnp.int32) // TOP_K
    row_src = jnp.zeros((n_rows,), jnp.int32).at[dest].set(tok, unique_indices=True)
    used = bend[-1]

    def expert_of_block(b):
        return jnp.minimum(jnp.sum((bend[None, :] <= b[:, None]).astype(jnp.int32), axis=1), n_experts - 1)

    blk = jnp.arange(n_blocks, dtype=jnp.int32)
    blk_e = expert_of_block(jnp.minimum(blk, used - 1))
    blk_new = (blk == bstart[blk_e]).astype(jnp.int32)
    n_items = 2 * n_blocks
    idx = jnp.arange(n_items, dtype=jnp.int32)
    it = jnp.minimum(idx, 2 * used - 1)
    item_e = expert_of_block(it // 2)
    off = it - 2 * bstart[item_e]
    item_j = (off // nblk[item_e]).astype(jnp.int32)
    item_b = (bstart[item_e] + off % nblk[item_e]).astype(jnp.int32)
    item_new = (item_b == bstart[item_e]).astype(jnp.int32)
    spare = idx - 2 * used
    is_spare = spare >= 0
    item_b = jnp.where(is_spare, used + spare // 2, item_b)
    item_oj = jnp.where(is_spare, spare % 2, item_j)
    return dict(dest=dest, row_src=row_src, blk_e=blk_e, blk_b=blk, blk_new=blk_new,
                n_blocks=used.reshape(1).astype(jnp.int32), item_e=item_e, item_j=item_j,
                item_b=item_b, item_oj=item_oj, item_new=item_new,
                n_items=(2 * used).reshape(1).astype(jnp.int32))


def _combine_kernel(x_ref, y_ref, rt_ref, g2_ref, gf_ref, o_ref, *, final_norm):
    bb, tt, d = x_ref.shape
    rows = bb * tt
    nck = d // LANES
    rt = rt_ref[...]
    gate_a, gate_b = rt[:, 2:3], rt[:, 3:4]
    ssq = jnp.zeros((bb, tt, 1), F32)
    for c in range(nck):
        sl = slice(c * LANES, (c + 1) * LANES)
        y = (gate_a * _load_token_major(y_ref.at[0], rows, c, nck)
             + gate_b * _load_token_major(y_ref.at[1], rows, c, nck))
        x = x_ref[:, :, sl] + g2_ref[:, :, sl] * y.reshape(bb, tt, LANES)
        o_ref[:, :, sl] = x
        ssq = ssq + jnp.sum(x * x, axis=-1, keepdims=True)
    if final_norm:
        o_ref[...] = o_ref[...] * lax.rsqrt(ssq * (1.0 / d) + EPS) * gf_ref[...]


def _combine(x, y_tok, row0, route, gate, g_final, final_norm, max_rows):
    b, t, d = x.shape
    bb, tt = _row_tiling(b, t, max_rows)
    rows = bb * tt
    nt = t // tt
    nck = d // LANES
    assert row0 % rows == 0
    blk0 = row0 // rows
    xspec = pl.BlockSpec((bb, tt, d), lambda i, k: (i, k, 0))
    kern = functools.partial(_combine_kernel, final_norm=final_norm)
    return pl.pallas_call(
        kern,
        grid=(b // bb, nt),
        in_specs=[xspec,
                  pl.BlockSpec((TOP_K, rows * nck, LANES), lambda i, k: (0, blk0 + i * nt + k, 0)),
                  pl.BlockSpec((rows, LANES), lambda i, k: (i * nt + k, 0)),
                  pl.BlockSpec((bb, 1, d), lambda i, k: (i, 0, 0)),
                  pl.BlockSpec((1, 1, d), lambda i, k: (0, 0, 0))],
        out_specs=xspec,
        out_shape=jax.ShapeDtypeStruct(x.shape, F32),
        compiler_params=_cparams(("arbitrary", "arbitrary")),
        name="combine",
    )(x, y_tok, route, gate, g_final.reshape(1, 1, d))


def _layer_params(l, w):
    d = w["w_in"].shape[1]
    nh = w["w_q"].shape[1]
    da = w["ln_v_g"].shape[1]
    db = w["w_conv"].shape[2]
    n_main = 2 * da + 2 * db
    w_in = w["w_in"][l]
    w_gate = jnp.zeros((d, 2 * LANES), BF16)
    w_gate = w_gate.at[:, :nh].set(w_in[:, n_main:n_main + nh].astype(BF16))
    w_gate = w_gate.at[:, LANES:LANES + nh].set(w_in[:, n_main + nh:n_main + 2 * nh].astype(BF16))
    n_groups = w["w_r1"].shape[2]
    per_group = w["w_r2"].shape[3]
    w_r = jnp.concatenate([w["w_r1"][l], jnp.transpose(w["w_r2"][l], (1, 0, 2)).reshape(d, n_groups * per_group)],
                          axis=1)
    n_r = w_r.shape[1]
    w_r = jnp.pad(w_r, ((0, 0), (0, LANES - n_r)))
    w_r_hi = w_r.astype(BF16)
    w_r_lo = (w_r - w_r_hi.astype(F32)).astype(BF16)
    b_r = jnp.pad(jnp.concatenate([w["b_r1"][l], w["b_r2"][l].reshape(-1)]), (0, LANES - n_r)).reshape(1, LANES)
    return dict(
        g_norm1=w["g_norm1"][l], g_norm2=w["g_norm2"][l],
        w_main=w_in[:, :n_main].astype(BF16), w_gate=w_gate,
        ln_g=w["ln_v_g"][l], ln_b=w["ln_v_b"][l],
        w_conv=w["w_conv"][l], b_conv=w["b_conv"][l].reshape(1, db),
        w_q=w["w_q"][l].astype(BF16), w_k=w["w_k"][l].astype(BF16),
        w_kt=jnp.swapaxes(w["w_k"][l], 1, 2).astype(BF16), w_v=w["w_v"][l].astype(BF16),
        b_i=jnp.pad(w["b_i"][l], (0, LANES - nh)).reshape(1, LANES),
        b_f=jnp.pad(w["b_f"][l], (0, LANES - nh)).reshape(1, LANES),
        g_mh=w["g_mh"][l].reshape(1, db), k_scale=1.0 / math.sqrt(LANES),
        w_out=w["w_out"][l].astype(BF16),
        w_router=jnp.stack([w_r_hi, w_r_lo]), b_router=b_r,
        n_groups=n_groups, per_group=per_group,
    )


def _mix_weights(w_spatial, b_spatial, L, da):
    nh = w_spatial.shape[0]
    causal = jnp.tril(jnp.ones((L, L), dtype=bool))
    w = jnp.where(causal[None], w_spatial[:, :L, :L], 0.0)
    reps = TILE // L
    if reps > 1:
        eye = jnp.eye(reps, dtype=F32)
        w = jnp.einsum("ab,hts->hatbs", eye, w).reshape(nh, TILE, TILE)
    bias = jnp.tile(jnp.transpose(b_spatial[:, :L]), (reps, 1))
    bias_full = jnp.repeat(bias, da // nh, axis=1)
    return w.astype(BF16), bias_full


def kernel(x_prompt, x_sample, c_prompt, c_sample, state_mlstm_C, state_mlstm_n, state_mlstm_m, state_conv, w_ada, b_ada, g_norm1, g_norm2, w_in, ln_v_g, ln_v_b, w_spatial, b_spatial, w_conv, b_conv, w_q, w_k, w_v, b_i, b_f, g_mh, w_out, w_r1, b_r1, w_r2, b_r2, w_e_gate, w_e_up, w_e_down, g_final):
    weights = dict(w_in=w_in, ln_v_g=ln_v_g, ln_v_b=ln_v_b, w_conv=w_conv, b_conv=b_conv, w_q=w_q, w_k=w_k,
                   w_v=w_v, b_i=b_i, b_f=b_f, g_mh=g_mh, w_out=w_out, w_r1=w_r1, b_r1=b_r1, w_r2=w_r2, b_r2=b_r2,
                   g_norm1=g_norm1, g_norm2=g_norm2)
    depth = w_in.shape[0]
    bp, tp, d = x_prompt.shape
    bs, ts, _ = x_sample.shape
    nh = w_q.shape[1]
    da = ln_v_g.shape[1]
    db = w_conv.shape[2]
    n_mod = w_ada.shape[2] // d
    n_experts = w_e_gate.shape[1]
    n_p, n_s = bp * tp, bs * ts
    assert da == db and da % LANES == 0 and nh * LANES == db and tp % TILE == 0 and TILE % ts == 0
    assert ts >= CONV_W - 1 and ts % SUBLANES == 0 and n_s % TILE == 0
    xm_col, zo_col = 2 * da // db, 2 * da // db + 1

    mod = _ada(jnp.concatenate([c_prompt, c_sample], axis=0), w_ada, b_ada).reshape(depth, bp + bs, n_mod, 1, d)

    xp, xs = x_prompt, x_sample
    outs = {k: [] for k in ("cp", "np", "mp", "bp", "cs", "ns", "ms", "bs", "vs")}
    for l in range(depth):
        p = _layer_params(l, weights)
        mods_p = [mod[l, :bp, i] for i in range(n_mod)]
        mods_s = [mod[l, bp:, i] for i in range(n_mod)]

        proj_p, gates_p = _inproj(xp, mods_p[1], mods_p[0], p["g_norm1"], p["w_main"], p["w_gate"], 1024)
        w_mix_p, bias_p = _mix_weights(w_spatial[l], b_spatial[l], min(tp, TILE), da)
        (ya_p,) = _grpa(proj_p, p["ln_g"], p["ln_b"], w_mix_p, bias_p, da, False)
        yb_p, c_p, n_pn, m_p = _mlstm_prompt(proj_p, gates_p, p, bp, tp, db, nh, xm_col, zo_col)
        outs["cp"].append(c_p)
        outs["np"].append(n_pn.reshape(bp, nh, LANES))
        outs["mp"].append(m_p[:, 0, :nh])
        outs["bp"].append(proj_p.reshape(bp, tp, -1)[:, tp - (CONV_W - 1):, 2 * da:2 * da + db])

        proj_s, gates_s = _inproj(xs, mods_s[1], mods_s[0], p["g_norm1"], p["w_main"], p["w_gate"], 512)
        w_mix_s, bias_s = _mix_weights(w_spatial[l], b_spatial[l], min(ts, TILE), da)
        ya_s, v_s = _grpa(proj_s, p["ln_g"], p["ln_b"], w_mix_s, bias_s, da, True)
        prev_rows = jnp.pad(state_conv[l], ((0, 0), (ts - (CONV_W - 1), 0), (0, 0))).reshape(n_s, db)
        n0_rows = jnp.repeat(state_mlstm_n[l].reshape(bs, db), ts, axis=0)
        m0_rows = jnp.pad(jnp.repeat(state_mlstm_m[l], ts, axis=0), ((0, 0), (0, LANES - nh)))
        yb_s, c_s, n_rows_s, m_rows_s = _mlstm_sample(proj_s, gates_s, prev_rows, state_mlstm_C, l, n0_rows, m0_rows,
                                                      p, math.gcd(ts, TILE), db, nh, xm_col, zo_col)
        outs["cs"].append(c_s)
        outs["ns"].append(n_rows_s.reshape(bs, ts, nh, LANES)[:, ts - 1])
        outs["ms"].append(m_rows_s.reshape(bs, ts, LANES)[:, ts - 1, :nh])
        outs["bs"].append(proj_s.reshape(bs, ts, -1)[:, ts - (CONV_W - 1):, 2 * da:2 * da + db])
        outs["vs"].append(v_s.reshape(bs, ts, da))

        rargs = (p["w_router"], p["b_router"], p["n_groups"], p["per_group"])
        xp, h2_p, rt_p = _outproj(xp, ya_p, yb_p, p["w_out"], mods_p[2], mods_p[4], mods_p[3], p["g_norm2"], *rargs, 256)
        xs, h2_s, rt_s = _outproj(xs, ya_s, yb_s, p["w_out"], mods_s[2], mods_s[4], mods_s[3], p["g_norm2"], *rargs, 256)

        route = jnp.concatenate([rt_p, rt_s], axis=0)
        plan = _moe_plan(route[:, :TOP_K].astype(jnp.int32), n_experts)
        nck = d // LANES
        x_buf = _gather_rows(plan["row_src"], h2_p, h2_s, nck)
        h_buf = _expert_up(plan, x_buf, w_e_gate, w_e_up, l)
        y_buf = _expert_down(plan, h_buf, w_e_down, l)
        dest_t = jnp.transpose(plan["dest"].reshape(n_p + n_s, TOP_K)).reshape(-1)
        y_tok = _gather_rows(dest_t, y_buf, y_buf, nck).reshape(TOP_K, (n_p + n_s) * nck, LANES)
        last = l == depth - 1
        xp = _combine(xp, y_tok, 0, rt_p, mods_p[5], g_final, last, 512)
        xs = _combine(xs, y_tok, n_p, rt_s, mods_s[5], g_final, last, 256)

    st = jnp.stack
    return (xp, xs, st(outs["cp"]), st(outs["np"]), st(outs["mp"]), st(outs["bp"]),
            st(outs["cs"]), st(outs["ns"]), st(outs["ms"]), st(outs["bs"]), st(outs["vs"]))
```

```python
import functools
import math

import jax
import jax.numpy as jnp
from jax import lax
from jax.experimental import pallas as pl
from jax.experimental.pallas import tpu as pltpu

F32 = jnp.float32
BF16 = jnp.bfloat16
EPS = 1e-6
LANES = 128
SUBLANES = 8
TILE = 128
CONV_W = 4
TOP_K = 2
VMEM_LIMIT_BYTES = 56 * 1024 * 1024
MOE_BLOCK_ROWS = 256
GATHER_CHUNK = 512
NEG_INF = float("-inf")


def _cparams(sem):
    return pltpu.CompilerParams(dimension_semantics=sem, vmem_limit_bytes=VMEM_LIMIT_BYTES)


def _split3(x):
    x1 = x.astype(BF16)
    r1 = x - x1.astype(F32)
    x2 = r1.astype(BF16)
    r2 = r1 - x2.astype(F32)
    return x1, x2, r2.astype(BF16)


def _ada_kernel(c_ref, w_ref, b_ref, o_ref):
    c = c_ref[...]
    a = c * jax.nn.sigmoid(c)
    o_ref[0] = jnp.dot(a.astype(BF16), w_ref[0].astype(BF16), preferred_element_type=F32) + b_ref[0]


def _ada(c_all, w_ada, b_ada):
    depth, d, n = w_ada.shape
    bc = c_all.shape[0]
    tn = min(1024, n)
    return pl.pallas_call(
        _ada_kernel,
        grid=(depth, n // tn),
        in_specs=[pl.BlockSpec((bc, d), lambda l, j: (0, 0)),
                  pl.BlockSpec((1, d, tn), lambda l, j: (l, 0, j)),
                  pl.BlockSpec((1, 1, tn), lambda l, j: (l, 0, j))],
        out_specs=pl.BlockSpec((1, bc, tn), lambda l, j: (l, 0, j)),
        out_shape=jax.ShapeDtypeStruct((depth, bc, n), F32),
        compiler_params=_cparams(("arbitrary", "arbitrary")),
        name="ada",
    )(c_all, w_ada, b_ada.reshape(depth, 1, n))


def _inproj_kernel(x_ref, sc_ref, sh_ref, g_ref, w_ref, wg_ref, o_ref, og_ref, hn_ref):
    @pl.when(pl.program_id(2) == 0)
    def _():
        x = x_ref[...]
        ms = jnp.mean(x * x, axis=-1, keepdims=True)
        h = x * lax.rsqrt(ms + EPS) * g_ref[...]
        h = h * (1.0 + sc_ref[...]) + sh_ref[...]
        hb = h.reshape(hn_ref.shape).astype(BF16)
        hn_ref[...] = hb
        og_ref[...] = jnp.dot(hb, wg_ref[...], preferred_element_type=F32)

    o_ref[...] = jnp.dot(hn_ref[...], w_ref[...], preferred_element_type=F32)


def _row_tiling(b, t, max_rows):
    if t >= max_rows:
        assert t % max_rows == 0
        return 1, max_rows
    bb = min(b, max_rows // t)
    assert b % bb == 0 and t % SUBLANES == 0
    return bb, t


def _inproj(x, scale, shift, g_norm, w_main, w_gate, max_rows):
    b, t, d = x.shape
    n = w_main.shape[1]
    ng = w_gate.shape[1]
    bb, tt = _row_tiling(b, t, max_rows)
    rows = bb * tt
    tn = min(512, n)
    nt = t // tt
    return pl.pallas_call(
        _inproj_kernel,
        grid=(b // bb, nt, n // tn),
        in_specs=[pl.BlockSpec((bb, tt, d), lambda i, k, j: (i, k, 0)),
                  pl.BlockSpec((bb, 1, d), lambda i, k, j: (i, 0, 0)),
                  pl.BlockSpec((bb, 1, d), lambda i, k, j: (i, 0, 0)),
                  pl.BlockSpec((1, 1, d), lambda i, k, j: (0, 0, 0)),
                  pl.BlockSpec((d, tn), lambda i, k, j: (0, j)),
                  pl.BlockSpec((d, ng), lambda i, k, j: (0, 0))],
        out_specs=[pl.BlockSpec((rows, tn), lambda i, k, j: (i * nt + k, j)),
                   pl.BlockSpec((rows, ng), lambda i, k, j: (i * nt + k, 0))],
        out_shape=[jax.ShapeDtypeStruct((b * t, n), F32), jax.ShapeDtypeStruct((b * t, ng), F32)],
        scratch_shapes=[pltpu.VMEM((rows, d), BF16)],
        compiler_params=_cparams(("arbitrary", "arbitrary", "arbitrary")),
        name="inproj",
    )(x, scale, shift, g_norm.reshape(1, 1, d), w_main, w_gate)


def _grpa_kernel(u_ref, v_ref, lg_ref, lb_ref, w_ref, bias_ref, ya_ref, *maybe_v_out):
    rows, da = u_ref.shape
    nh = w_ref.shape[0]
    g = jax.nn.gelu(v_ref[...])
    mu = jnp.mean(g, axis=-1, keepdims=True)
    gc = g - mu
    var = jnp.mean(gc * gc, axis=-1, keepdims=True)
    v = gc * lax.rsqrt(var + EPS) * lg_ref[...] + lb_ref[...]
    if maybe_v_out:
        maybe_v_out[0][...] = v
    vb = v.astype(BF16)
    for c in range(rows // TILE):
        r0 = c * TILE
        for h in range(nh):
            c0 = h * LANES
            z = jnp.dot(w_ref[h], vb[r0:r0 + TILE, c0:c0 + LANES], preferred_element_type=F32)
            u = jax.nn.gelu(u_ref[r0:r0 + TILE, c0:c0 + LANES])
            y = u * (z + bias_ref[:, c0:c0 + LANES])
            ya_ref[r0:r0 + TILE, c0:c0 + LANES] = y.astype(BF16)


def _grpa(proj, ln_g, ln_b, w_mix, bias_full, da, want_v):
    n_rows = proj.shape[0]
    rows = min(512, n_rows)
    nh = w_mix.shape[0]
    out_specs = [pl.BlockSpec((rows, da), lambda i: (i, 0))]
    out_shape = [jax.ShapeDtypeStruct((n_rows, da), BF16)]
    if want_v:
        out_specs.append(pl.BlockSpec((rows, da), lambda i: (i, 0)))
        out_shape.append(jax.ShapeDtypeStruct((n_rows, da), F32))
    return pl.pallas_call(
        _grpa_kernel,
        grid=(n_rows // rows,),
        in_specs=[pl.BlockSpec((rows, da), lambda i: (i, 0)),
                  pl.BlockSpec((rows, da), lambda i: (i, 1)),
                  pl.BlockSpec((1, da), lambda i: (0, 0)),
                  pl.BlockSpec((1, da), lambda i: (0, 0)),
                  pl.BlockSpec((nh, TILE, TILE), lambda i: (0, 0, 0)),
                  pl.BlockSpec((TILE, da), lambda i: (0, 0))],
        out_specs=out_specs,
        out_shape=out_shape,
        compiler_params=_cparams(("arbitrary",)),
        name="grpa",
    )(proj, proj, ln_g.reshape(1, da), ln_b.reshape(1, da), w_mix, bias_full)


def _group_last(x, L):
    if L == TILE:
        return jnp.broadcast_to(x[TILE - 1:TILE, :], x.shape)
    g = TILE // L
    x3 = x.reshape(g, L, x.shape[1])
    return jnp.broadcast_to(x3[:, L - 1:L, :], x3.shape).reshape(x.shape)


def _group_sum(x, L):
    if L == TILE:
        return jnp.broadcast_to(jnp.sum(x, axis=0, keepdims=True), x.shape)
    g = TILE // L
    x3 = x.reshape(g, L, x.shape[1])
    return jnp.broadcast_to(jnp.sum(x3, axis=1, keepdims=True), x3.shape).reshape(x.shape)


def _mlstm_kernel(*refs, L, carry, nh, k_scale):
    if carry:
        (xm_ref, zo_ref, g_ref, wconv_ref, bconv_ref, wq_ref, wk_ref, wkt_ref, wv_ref, bi_ref, bf_ref, gmh_ref,
         yb_ref, c_ref, n_ref, m_ref, prev_scr) = refs
        c_in_ref = c_ref
        first = pl.program_id(1) == 0

        @pl.when(first)
        def _():
            c_ref[...] = jnp.zeros(c_ref.shape, F32)
            n_ref[...] = jnp.zeros(n_ref.shape, F32)
            m_ref[...] = jnp.zeros(m_ref.shape, F32)
            prev_scr[...] = jnp.zeros(prev_scr.shape, F32)

        prev = prev_scr[...]
        n0_rows = jnp.broadcast_to(n_ref[0], (TILE, n_ref.shape[2]))
        m0 = jnp.broadcast_to(m_ref[0], (TILE, LANES))
    else:
        (xm_ref, zo_ref, g_ref, prev_ref, c_layer_ref, n0_ref, m0_ref,
         wconv_ref, bconv_ref, wq_ref, wk_ref, wkt_ref, wv_ref, bi_ref, bf_ref, gmh_ref,
         yb_ref, c_ref, n_ref, m_ref) = refs
        c_in_ref = c_layer_ref.at[0]
        prev = prev_ref[...]
        n0_rows = n0_ref[...]
        m0 = m0_ref[...]

    groups = TILE // L
    x = xm_ref[...]
    row_in_group = lax.broadcasted_iota(jnp.int32, (TILE, 1), 0) % L

    acc = bconv_ref[...] + wconv_ref[CONV_W - 1:CONV_W, :] * x
    for j in range(1, CONV_W):
        from_prev = pltpu.roll(prev, (j - L) % TILE, 0)
        shifted = jnp.where(row_in_group < j, from_prev, pltpu.roll(x, j, 0))
        acc = acc + wconv_ref[CONV_W - 1 - j:CONV_W - j, :] * shifted
    xc = acc * jax.nn.sigmoid(acc)
    if carry:
        prev_scr[...] = x

    gates = g_ref[...]
    gi = gates[:, :LANES] + bi_ref[...]
    gf = gates[:, LANES:] + bf_ref[...]
    lf = jnp.minimum(gf, 0.0) - jnp.log1p(jnp.exp(-jnp.abs(gf)))

    r_i = lax.broadcasted_iota(jnp.int32, (TILE, TILE), 0)
    c_i = lax.broadcasted_iota(jnp.int32, (TILE, TILE), 1)
    mask = (c_i <= r_i) & ((c_i // L) == (r_i // L)) if L != TILE else (c_i <= r_i)
    mask_b = jnp.where(mask, 1.0, 0.0).astype(BF16)
    lf1, lf2, lf3 = _split3(lf)
    cum = (jnp.dot(mask_b, lf1, preferred_element_type=F32) + jnp.dot(mask_b, lf2, preferred_element_type=F32)
           + jnp.dot(mask_b, lf3, preferred_element_type=F32))
    cum_t = cum.T
    gi_t = gi.T
    log_inter = cum + m0
    lane = lax.broadcasted_iota(jnp.int32, (TILE, LANES), 1)

    def d_matrix(h):
        d = cum[:, h:h + 1] - cum_t[h:h + 1, :] + gi_t[h:h + 1, :]
        return jnp.where(mask, d, NEG_INF)

    mt = jnp.zeros((TILE, LANES), F32)
    for h in range(nh):
        mt_h = jnp.maximum(log_inter[:, h:h + 1], jnp.max(d_matrix(h), axis=1, keepdims=True))
        mt = jnp.where(lane == h, mt_h, mt)

    m_last = _group_last(mt, L)
    w_inter = jnp.exp(log_inter - mt)
    w_last = jnp.exp(_group_last(cum, L) - cum + gi - m_last)
    w_last_t = w_last.T
    inv_floor = jnp.exp(-mt)

    for h in range(nh):
        c0 = h * LANES
        xc_h = xc[:, c0:c0 + LANES].astype(BF16)
        q = jnp.dot(xc_h, wq_ref[h], preferred_element_type=F32)
        k = jnp.dot(xc_h, wk_ref[h], preferred_element_type=F32) * k_scale
        k_t = lax.dot_general(wkt_ref[h], xc_h, (((1,), (1,)), ((), ())), preferred_element_type=F32) * k_scale
        v = jnp.dot(x[:, c0:c0 + LANES].astype(BF16), wv_ref[h], preferred_element_type=F32)
        qb = q.astype(BF16)
        vb = v.astype(BF16)
        mt_h = mt[:, h:h + 1]
        wi_h = w_inter[:, h:h + 1]
        w_intra = jnp.exp(d_matrix(h) - mt_h)
        s = jnp.dot(qb, k_t.astype(BF16), preferred_element_type=F32) * w_intra

        if carry:
            qc = jnp.dot(qb, c_in_ref[0, h].astype(BF16), preferred_element_type=F32)
        else:
            qc = jnp.concatenate(
                [jnp.dot(q[g * L:(g + 1) * L, :].astype(BF16), c_in_ref[g, h].astype(BF16),
                         preferred_element_type=F32)
                 for g in range(groups)], axis=0)

        num = wi_h * qc + jnp.dot(s.astype(BF16), vb, preferred_element_type=F32)
        qn = jnp.sum(q * n0_rows[:, c0:c0 + LANES], axis=1, keepdims=True)
        den = wi_h * qn + jnp.sum(s, axis=1, keepdims=True)
        hh = num / jnp.maximum(jnp.abs(den), inv_floor[:, h:h + 1])
        hh = hh * lax.rsqrt(jnp.mean(hh * hh, axis=1, keepdims=True) + EPS) * gmh_ref[:, c0:c0 + LANES]
        yb_ref[:, c0:c0 + LANES] = (jax.nn.sigmoid(zo_ref[:, c0:c0 + LANES]) * hh).astype(BF16)

        kw_t = k_t * w_last_t[h:h + 1, :]
        n_rows = wi_h * n0_rows[:, c0:c0 + LANES] + _group_sum(w_last[:, h:h + 1] * k, L)
        if carry:
            decay = jnp.broadcast_to(w_inter[TILE - 1:TILE, h:h + 1], (LANES, LANES))
            c_ref[0, h] = decay * c_in_ref[0, h] + jnp.dot(kw_t.astype(BF16), vb, preferred_element_type=F32)
            n_ref[0, :, c0:c0 + LANES] = n_rows[TILE - 1:TILE, :]
        else:
            n_ref[:, c0:c0 + LANES] = n_rows
            for g in range(groups):
                kw_g = jnp.where((c_i // L) == g, kw_t, 0.0).astype(BF16)
                upd = jnp.dot(kw_g, vb, preferred_element_type=F32)
                last = g * L + L - 1
                decay = jnp.broadcast_to(w_inter[last:last + 1, h:h + 1], (LANES, LANES))
                c_ref[g, h] = decay * c_in_ref[g, h] + upd

    if carry:
        m_ref[0] = mt[TILE - 1:TILE, :]
    else:
        m_ref[...] = mt


def _mlstm_weights(p):
    return [p["w_conv"], p["b_conv"], p["w_q"], p["w_k"], p["w_kt"], p["w_v"], p["b_i"], p["b_f"], p["g_mh"]]


def _mlstm_weight_specs(nh, db, nidx):
    z2 = (lambda i, c: (0, 0)) if nidx == 2 else (lambda i: (0, 0))
    z3 = (lambda i, c: (0, 0, 0)) if nidx == 2 else (lambda i: (0, 0, 0))
    hw = pl.BlockSpec((nh, LANES, LANES), z3)
    return [pl.BlockSpec((CONV_W, db), z2), pl.BlockSpec((1, db), z2), hw, hw, hw, hw,
            pl.BlockSpec((1, LANES), z2), pl.BlockSpec((1, LANES), z2), pl.BlockSpec((1, db), z2)]


def _mlstm_prompt(proj, gates, p, b, t, db, nh, xm_col, zo_col):
    nt = t // TILE
    kern = functools.partial(_mlstm_kernel, L=TILE, carry=True, nh=nh, k_scale=p["k_scale"])
    return pl.pallas_call(
        kern,
        grid=(b, nt),
        in_specs=[pl.BlockSpec((TILE, db), lambda i, c: (i * nt + c, xm_col)),
                  pl.BlockSpec((TILE, db), lambda i, c: (i * nt + c, zo_col)),
                  pl.BlockSpec((TILE, 2 * LANES), lambda i, c: (i * nt + c, 0))] + _mlstm_weight_specs(nh, db, 2),
        out_specs=[pl.BlockSpec((TILE, db), lambda i, c: (i * nt + c, 0)),
                   pl.BlockSpec((1, nh, LANES, LANES), lambda i, c: (i, 0, 0, 0)),
                   pl.BlockSpec((1, 1, db), lambda i, c: (i, 0, 0)),
                   pl.BlockSpec((1, 1, LANES), lambda i, c: (i, 0, 0))],
        out_shape=[jax.ShapeDtypeStruct((b * t, db), BF16),
                   jax.ShapeDtypeStruct((b, nh, LANES, LANES), F32),
                   jax.ShapeDtypeStruct((b, 1, db), F32),
                   jax.ShapeDtypeStruct((b, 1, LANES), F32)],
        scratch_shapes=[pltpu.VMEM((TILE, db), F32)],
        compiler_params=_cparams(("arbitrary", "arbitrary")),
        name="mlstm_prompt",
    )(proj, proj, gates, *_mlstm_weights(p))


def _mlstm_sample(proj, gates, prev_rows, c0, layer, n0_rows, m0_rows, p, L, db, nh, xm_col, zo_col):
    n_rows = proj.shape[0]
    groups = TILE // L
    kern = functools.partial(_mlstm_kernel, L=L, carry=False, nh=nh, k_scale=p["k_scale"])
    row_spec = pl.BlockSpec((TILE, db), lambda i: (i, 0))
    return pl.pallas_call(
        kern,
        grid=(n_rows // TILE,),
        in_specs=[pl.BlockSpec((TILE, db), lambda i: (i, xm_col)),
                  pl.BlockSpec((TILE, db), lambda i: (i, zo_col)),
                  pl.BlockSpec((TILE, 2 * LANES), lambda i: (i, 0)),
                  row_spec,
                  pl.BlockSpec((1, groups, nh, LANES, LANES), lambda i: (layer, i, 0, 0, 0)),
                  row_spec,
                  pl.BlockSpec((TILE, LANES), lambda i: (i, 0))] + _mlstm_weight_specs(nh, db, 1),
        out_specs=[row_spec,
                   pl.BlockSpec((groups, nh, LANES, LANES), lambda i: (i, 0, 0, 0)),
                   row_spec,
                   pl.BlockSpec((TILE, LANES), lambda i: (i, 0))],
        out_shape=[jax.ShapeDtypeStruct((n_rows, db), BF16),
                   jax.ShapeDtypeStruct(c0.shape[1:], F32),
                   jax.ShapeDtypeStruct((n_rows, db), F32),
                   jax.ShapeDtypeStruct((n_rows, LANES), F32)],
        compiler_params=_cparams(("arbitrary",)),
        name="mlstm_sample",
    )(proj, proj, gates, prev_rows, c0, n0_rows, m0_rows, *_mlstm_weights(p))


def _store_token_major(ref, val):
    n, d = val.shape
    nck = d // LANES
    for c in range(nck):
        ref[pl.ds(c, n, stride=nck), :] = val[:, c * LANES:(c + 1) * LANES]


def _load_token_major(ref, n, c, nck):
    return ref[pl.ds(c, n, stride=nck), :]


def _route(lg, n_groups, per_group):
    lane_i = lax.broadcasted_iota(jnp.int32, lg.shape, 1)
    lane = lane_i.astype(F32)
    far = float(LANES)
    is_g = lane_i < n_groups
    lgg = jnp.where(is_g, lg, NEG_INF)
    mg = jnp.max(lgg, axis=1, keepdims=True)
    g_sel = jnp.min(jnp.where(lgg == mg, lane, far), axis=1, keepdims=True)
    p_g = 1.0 / jnp.sum(jnp.where(is_g, jnp.exp(lgg - mg), 0.0), axis=1, keepdims=True)
    lo = n_groups + g_sel * per_group
    sel = (lane >= lo) & (lane < lo + per_group)
    le = jnp.where(sel, lg, NEG_INF)
    m1 = jnp.max(le, axis=1, keepdims=True)
    i1 = jnp.min(jnp.where(le == m1, lane, far), axis=1, keepdims=True)
    le2 = jnp.where(lane == i1, NEG_INF, le)
    m2 = jnp.max(le2, axis=1, keepdims=True)
    i2 = jnp.min(jnp.where(le2 == m2, lane, far), axis=1, keepdims=True)
    ratio = jnp.exp(m2 - m1)
    gate1 = p_g / (1.0 + ratio)
    gate2 = p_g * ratio / (1.0 + ratio)
    out = jnp.where(lane_i == 0, i1 - n_groups, 0.0)
    out = jnp.where(lane_i == 1, i2 - n_groups, out)
    out = jnp.where(lane_i == 2, gate1, out)
    return jnp.where(lane_i == 3, gate2, out)


def _outproj_kernel(x_ref, ya_ref, yb_ref, w_ref, g1_ref, sc_ref, sh_ref, gn_ref, wr_ref, br_ref,
                    xn_ref, h2_ref, rt_ref, *, n_groups, per_group):
    da = ya_ref.shape[1]
    mix = (jnp.dot(ya_ref[...], w_ref[:da, :], preferred_element_type=F32)
           + jnp.dot(yb_ref[...], w_ref[da:, :], preferred_element_type=F32))
    x = x_ref[...] + g1_ref[...] * mix.reshape(x_ref.shape)
    xn_ref[...] = x
    ms = jnp.mean(x * x, axis=-1, keepdims=True)
    h = x * lax.rsqrt(ms + EPS) * gn_ref[...]
    rows = ya_ref.shape[0]
    h = (h * (1.0 + sc_ref[...]) + sh_ref[...]).reshape(rows, x_ref.shape[2])
    _store_token_major(h2_ref, h)
    h_hi = h.astype(BF16)
    h_lo = (h - h_hi.astype(F32)).astype(BF16)
    lg = (jnp.dot(h_hi, wr_ref[0], preferred_element_type=F32) + jnp.dot(h_lo, wr_ref[0], preferred_element_type=F32)
          + jnp.dot(h_hi, wr_ref[1], preferred_element_type=F32) + br_ref[...])
    rt_ref[...] = _route(lg, n_groups, per_group)


def _outproj(x, ya, yb, w_out, gate, scale, shift, g_norm, w_router, b_router, n_groups, per_group, max_rows):
    b, t, d = x.shape
    bb, tt = _row_tiling(b, t, max_rows)
    rows = bb * tt
    nt = t // tt
    da = ya.shape[1]
    xspec = pl.BlockSpec((bb, tt, d), lambda i, k: (i, k, 0))
    mspec = pl.BlockSpec((bb, 1, d), lambda i, k: (i, 0, 0))
    kern = functools.partial(_outproj_kernel, n_groups=n_groups, per_group=per_group)
    return pl.pallas_call(
        kern,
        grid=(b // bb, nt),
        in_specs=[xspec,
                  pl.BlockSpec((rows, da), lambda i, k: (i * nt + k, 0)),
                  pl.BlockSpec((rows, ya.shape[1]), lambda i, k: (i * nt + k, 0)),
                  pl.BlockSpec(w_out.shape, lambda i, k: (0, 0)),
                  mspec, mspec, mspec,
                  pl.BlockSpec((1, 1, d), lambda i, k: (0, 0, 0)),
                  pl.BlockSpec(w_router.shape, lambda i, k: (0, 0, 0)),
                  pl.BlockSpec((1, LANES), lambda i, k: (0, 0))],
        out_specs=[xspec,
                   pl.BlockSpec((rows * (d // LANES), LANES), lambda i, k: (i * nt + k, 0)),
                   pl.BlockSpec((rows, LANES), lambda i, k: (i * nt + k, 0))],
        out_shape=[jax.ShapeDtypeStruct(x.shape, F32),
                   jax.ShapeDtypeStruct((b * t * (d // LANES), LANES), F32),
                   jax.ShapeDtypeStruct((b * t, LANES), F32)],
        compiler_params=_cparams(("arbitrary", "arbitrary")),
        name="outproj",
    )(x, ya, yb, w_out, gate, scale, shift, g_norm.reshape(1, 1, d), w_router, b_router)


def _gather_kernel(idx_ref, src_a_ref, src_b_ref, out_ref, buf, sem, *, n_a, chunk, nck):
    i = pl.program_id(0)

    def issue_step(step, slot):
        def issue(r, _):
            t = idx_ref[step * chunk + r]
            dst = buf.at[slot, pl.ds(pl.multiple_of(r * nck, nck), nck)]

            @pl.when(t < n_a)
            def _():
                src = src_a_ref.at[pl.ds(pl.multiple_of(t * nck, nck), nck)]
                pltpu.make_async_copy(src, dst, sem.at[slot]).start()

            @pl.when(t >= n_a)
            def _():
                src = src_b_ref.at[pl.ds(pl.multiple_of((t - n_a) * nck, nck), nck)]
                pltpu.make_async_copy(src, dst, sem.at[slot]).start()

            return 0

        lax.fori_loop(0, chunk, issue, 0, unroll=8)

    @pl.when(i == 0)
    def _():
        issue_step(0, 0)

    @pl.when(i + 1 < pl.num_programs(0))
    def _():
        issue_step(i + 1, (i + 1) % 2)

    slot = i % 2
    pltpu.make_async_copy(buf.at[slot], buf.at[slot], sem.at[slot]).wait()
    for c in range(nck):
        out_ref[:, c * LANES:(c + 1) * LANES] = _load_token_major(buf.at[slot], chunk, c, nck).astype(out_ref.dtype)


def _gather_rows(idx, src_a, src_b, nck, out_dtype):
    n_out = idx.shape[0]
    chunk = math.gcd(GATHER_CHUNK, n_out)
    d = src_a.shape[1]
    kern = functools.partial(_gather_kernel, n_a=src_a.shape[0] // nck, chunk=chunk, nck=nck)
    return pl.pallas_call(
        kern,
        grid_spec=pltpu.PrefetchScalarGridSpec(
            num_scalar_prefetch=1,
            grid=(n_out // chunk,),
            in_specs=[pl.BlockSpec(memory_space=pl.ANY), pl.BlockSpec(memory_space=pl.ANY)],
            out_specs=pl.BlockSpec((chunk, nck * d), lambda i, idx: (i, 0)),
            scratch_shapes=[pltpu.VMEM((2, chunk * nck, d), src_a.dtype), pltpu.SemaphoreType.DMA((2,))]),
        out_shape=jax.ShapeDtypeStruct((n_out, nck * d), out_dtype),
        compiler_params=_cparams(("arbitrary",)),
        name="gather_rows",
    )(idx, src_a, src_b)


def _expert_up_kernel(ie_ref, ij_ref, ib_ref, ioj_ref, inew_ref, islot_ref, ine_ref, inj_ref, ihas_ref, n_ref,
                      x_ref, wg_hbm, wu_hbm, o_ref, stage, wgb_scr, wub_scr, sem, *, layer):
    i = pl.program_id(0)
    tn = wgb_scr.shape[1]

    def weight_copies(e, j, slot):
        col = pl.multiple_of(j * tn, tn)
        return (pltpu.make_async_copy(wg_hbm.at[layer, e, :, pl.ds(col, tn)], stage.at[slot, 0], sem.at[slot, 0]),
                pltpu.make_async_copy(wu_hbm.at[layer, e, :, pl.ds(col, tn)], stage.at[slot, 1], sem.at[slot, 1]))

    @pl.when(i == 0)
    def _():
        for cp in weight_copies(ie_ref[0], ij_ref[0], islot_ref[0]):
            cp.start()

    @pl.when(i >= n_ref[0])
    def _():
        o_ref[...] = jnp.zeros(o_ref.shape, o_ref.dtype)

    @pl.when(i < n_ref[0])
    def _():
        @pl.when(inew_ref[i] == 1)
        def _():
            slot = islot_ref[i]

            @pl.when(ihas_ref[i] == 1)
            def _():
                for cp in weight_copies(ine_ref[i], inj_ref[i], 1 - slot):
                    cp.start()

            for cp in weight_copies(ie_ref[i], ij_ref[i], slot):
                cp.wait()
            wgb_scr[...] = stage[slot, 0].astype(BF16)
            wub_scr[...] = stage[slot, 1].astype(BF16)

        xb = x_ref[...]
        a = jnp.dot(xb, wgb_scr[...], preferred_element_type=F32)
        u = jnp.dot(xb, wub_scr[...], preferred_element_type=F32)
        o_ref[...] = (a * jax.nn.sigmoid(a) * u).astype(BF16)


def _expert_up(plan, x_buf, w_gate, w_up, layer):
    d, de = w_gate.shape[2], w_gate.shape[3]
    n_rows = x_buf.shape[0]
    tn = de // 2
    bm = MOE_BLOCK_ROWS
    up = plan["up"]
    n_items = up["e"].shape[0]
    names = ("e", "j", "b", "oj", "new", "slot", "next_e", "next_j", "has_next", "n")
    return pl.pallas_call(
        functools.partial(_expert_up_kernel, layer=layer),
        grid_spec=pltpu.PrefetchScalarGridSpec(
            num_scalar_prefetch=len(names),
            grid=(n_items,),
            in_specs=[pl.BlockSpec((bm, d), lambda i, *s: (s[2][i], 0)),
                      pl.BlockSpec(memory_space=pl.ANY), pl.BlockSpec(memory_space=pl.ANY)],
            out_specs=pl.BlockSpec((bm, tn), lambda i, *s: (s[2][i], s[3][i])),
            scratch_shapes=[pltpu.VMEM((2, 2, d, tn), F32), pltpu.VMEM((d, tn), BF16), pltpu.VMEM((d, tn), BF16),
                            pltpu.SemaphoreType.DMA((2, 2))]),
        out_shape=jax.ShapeDtypeStruct((n_rows, de), BF16),
        compiler_params=_cparams(("arbitrary",)),
        name="expert_up",
    )(*[up[k] for k in names], x_buf, w_gate, w_up)


def _expert_down_kernel(be_ref, bnew_ref, bslot_ref, bne_ref, bhas_ref, n_ref, h_ref, wd_hbm, o_ref,
                        stage, wdb_scr, sem, *, layer):
    i = pl.program_id(0)

    def weight_copy(e, slot):
        return pltpu.make_async_copy(wd_hbm.at[layer, e], stage.at[slot], sem.at[slot])

    @pl.when(i == 0)
    def _():
        weight_copy(be_ref[0], bslot_ref[0]).start()

    @pl.when(i >= n_ref[0])
    def _():
        o_ref[...] = jnp.zeros(o_ref.shape, o_ref.dtype)

    @pl.when(i < n_ref[0])
    def _():
        @pl.when(bnew_ref[i] == 1)
        def _():
            slot = bslot_ref[i]

            @pl.when(bhas_ref[i] == 1)
            def _():
                weight_copy(bne_ref[i], 1 - slot).start()

            weight_copy(be_ref[i], slot).wait()
            wdb_scr[...] = stage[slot].astype(BF16)

        _store_token_major(o_ref, jnp.dot(h_ref[...], wdb_scr[...], preferred_element_type=F32))


def _expert_down(plan, h_buf, w_down, layer):
    n_rows, de = h_buf.shape
    d = w_down.shape[3]
    nck = d // LANES
    bm = MOE_BLOCK_ROWS
    down = plan["down"]
    n_blocks = down["e"].shape[0]
    names = ("e", "new", "slot", "next_e", "has_next", "n")
    return pl.pallas_call(
        functools.partial(_expert_down_kernel, layer=layer),
        grid_spec=pltpu.PrefetchScalarGridSpec(
            num_scalar_prefetch=len(names),
            grid=(n_blocks,),
            in_specs=[pl.BlockSpec((bm, de), lambda i, *s: (i, 0)),
                      pl.BlockSpec(memory_space=pl.ANY)],
            out_specs=pl.BlockSpec((bm * nck, LANES), lambda i, *s: (i, 0)),
            scratch_shapes=[pltpu.VMEM((2, de, d), F32), pltpu.VMEM((de, d), BF16), pltpu.SemaphoreType.DMA((2,))]),
        out_shape=jax.ShapeDtypeStruct((n_rows * nck, LANES), F32),
        compiler_params=_cparams(("arbitrary",)),
        name="expert_down",
    )(*[down[k] for k in names], h_buf, w_down)


def _tile_schedule(new, valid, coords):
    n = new.shape[0]
    idx = jnp.arange(n, dtype=jnp.int32)
    starts = jnp.where((new == 1) & valid, idx, n)
    first_at_or_after = lax.cummin(starts[::-1])[::-1]
    nxt = jnp.concatenate([first_at_or_after[1:], jnp.full((1,), n, jnp.int32)])
    has_next = (nxt < n).astype(jnp.int32)
    nxt_c = jnp.minimum(nxt, n - 1)
    slot = ((jnp.cumsum(jnp.where(valid, new, 0)) - 1) % 2).astype(jnp.int32)
    return slot, has_next, [c[nxt_c] for c in coords]


def _moe_plan(expert, n_experts):
    bm = MOE_BLOCK_ROWS
    n_tok = expert.shape[0]
    n_asg = n_tok * TOP_K
    e_flat = expert.reshape(-1)
    onehot = (e_flat[:, None] == jnp.arange(n_experts, dtype=jnp.int32)[None, :]).astype(jnp.int32)
    csum = jnp.cumsum(onehot, axis=0)
    counts = csum[-1]
    rank = jnp.take_along_axis(csum, e_flat[:, None], axis=1)[:, 0] - 1
    nblk = (counts + bm - 1) // bm
    bend = jnp.cumsum(nblk)
    bstart = bend - nblk
    dest = bstart[e_flat] * bm + rank
    n_blocks = (n_asg + n_experts * (bm - 1) + bm - 1) // bm
    n_rows = -(-n_blocks * bm // GATHER_CHUNK) * GATHER_CHUNK
    tok = jnp.arange(n_asg, dtype=jnp.int32) // TOP_K
    row_src = jnp.zeros((n_rows,), jnp.int32).at[dest].set(tok, unique_indices=True)
    used = bend[-1]

    def expert_of_block(b):
        return jnp.minimum(jnp.sum((bend[None, :] <= b[:, None]).astype(jnp.int32), axis=1), n_experts - 1)

    blk = jnp.arange(n_blocks, dtype=jnp.int32)
    blk_e = expert_of_block(jnp.minimum(blk, used - 1))
    blk_new = (blk == bstart[blk_e]).astype(jnp.int32)
    n_items = 2 * n_blocks
    idx = jnp.arange(n_items, dtype=jnp.int32)
    it = jnp.minimum(idx, 2 * used - 1)
    item_e = expert_of_block(it // 2)
    off = it - 2 * bstart[item_e]
    item_j = (off // nblk[item_e]).astype(jnp.int32)
    item_b = (bstart[item_e] + off % nblk[item_e]).astype(jnp.int32)
    item_new = (item_b == bstart[item_e]).astype(jnp.int32)
    spare = idx - 2 * used
    is_spare = spare >= 0
    item_b = jnp.where(is_spare, used + spare // 2, item_b)
    item_oj = jnp.where(is_spare, spare % 2, item_j)
    d_slot, d_has, (d_ne,) = _tile_schedule(blk_new, blk < used, [blk_e])
    u_slot, u_has, (u_ne, u_nj) = _tile_schedule(item_new, ~is_spare, [item_e, item_j])
    down = dict(e=blk_e, new=blk_new, slot=d_slot, next_e=d_ne, has_next=d_has,
                n=used.reshape(1).astype(jnp.int32))
    up = dict(e=item_e, j=item_j, b=item_b, oj=item_oj, new=item_new, slot=u_slot, next_e=u_ne, next_j=u_nj,
              has_next=u_has, n=(2 * used).reshape(1).astype(jnp.int32))
    return dict(dest=dest, row_src=row_src, up=up, down=down)


def _combine_kernel(x_ref, y_ref, rt_ref, g2_ref, gf_ref, o_ref, *, final_norm):
    rt = rt_ref[...]
    y = rt[:, 2:3] * y_ref[0] + rt[:, 3:4] * y_ref[1]
    x = x_ref[...] + g2_ref[...] * y.reshape(x_ref.shape)
    if final_norm:
        ms = jnp.mean(x * x, axis=-1, keepdims=True)
        x = x * lax.rsqrt(ms + EPS) * gf_ref[...]
    o_ref[...] = x


def _combine(x, y_tok, row0, route, gate, g_final, final_norm, max_rows):
    b, t, d = x.shape
    bb, tt = _row_tiling(b, t, max_rows)
    rows = bb * tt
    nt = t // tt
    assert row0 % rows == 0
    blk0 = row0 // rows
    xspec = pl.BlockSpec((bb, tt, d), lambda i, k: (i, k, 0))
    kern = functools.partial(_combine_kernel, final_norm=final_norm)
    return pl.pallas_call(
        kern,
        grid=(b // bb, nt),
        in_specs=[xspec,
                  pl.BlockSpec((TOP_K, rows, d), lambda i, k: (0, blk0 + i * nt + k, 0)),
                  pl.BlockSpec((rows, LANES), lambda i, k: (i * nt + k, 0)),
                  pl.BlockSpec((bb, 1, d), lambda i, k: (i, 0, 0)),
                  pl.BlockSpec((1, 1, d), lambda i, k: (0, 0, 0))],
        out_specs=xspec,
        out_shape=jax.ShapeDtypeStruct(x.shape, F32),
        compiler_params=_cparams(("arbitrary", "arbitrary")),
        name="combine",
    )(x, y_tok, route, gate, g_final.reshape(1, 1, d))


def _layer_params(l, w):
    d = w["w_in"].shape[1]
    nh = w["w_q"].shape[1]
    da = w["ln_v_g"].shape[1]
    db = w["w_conv"].shape[2]
    n_main = 2 * da + 2 * db
    w_in = w["w_in"][l]
    w_gate = jnp.zeros((d, 2 * LANES), BF16)
    w_gate = w_gate.at[:, :nh].set(w_in[:, n_main:n_main + nh].astype(BF16))
    w_gate = w_gate.at[:, LANES:LANES + nh].set(w_in[:, n_main + nh:n_main + 2 * nh].astype(BF16))
    n_groups = w["w_r1"].shape[2]
    per_group = w["w_r2"].shape[3]
    w_r = jnp.concatenate([w["w_r1"][l], jnp.transpose(w["w_r2"][l], (1, 0, 2)).reshape(d, n_groups * per_group)],
                          axis=1)
    n_r = w_r.shape[1]
    w_r = jnp.pad(w_r, ((0, 0), (0, LANES - n_r)))
    w_r_hi = w_r.astype(BF16)
    w_r_lo = (w_r - w_r_hi.astype(F32)).astype(BF16)
    b_r = jnp.pad(jnp.concatenate([w["b_r1"][l], w["b_r2"][l].reshape(-1)]), (0, LANES - n_r)).reshape(1, LANES)
    return dict(
        g_norm1=w["g_norm1"][l], g_norm2=w["g_norm2"][l],
        w_main=w_in[:, :n_main].astype(BF16), w_gate=w_gate,
        ln_g=w["ln_v_g"][l], ln_b=w["ln_v_b"][l],
        w_conv=w["w_conv"][l], b_conv=w["b_conv"][l].reshape(1, db),
        w_q=w["w_q"][l].astype(BF16), w_k=w["w_k"][l].astype(BF16),
        w_kt=jnp.swapaxes(w["w_k"][l], 1, 2).astype(BF16), w_v=w["w_v"][l].astype(BF16),
        b_i=jnp.pad(w["b_i"][l], (0, LANES - nh)).reshape(1, LANES),
        b_f=jnp.pad(w["b_f"][l], (0, LANES - nh)).reshape(1, LANES),
        g_mh=w["g_mh"][l].reshape(1, db), k_scale=1.0 / math.sqrt(LANES),
        w_out=w["w_out"][l].astype(BF16),
        w_router=jnp.stack([w_r_hi, w_r_lo]), b_router=b_r,
        n_groups=n_groups, per_group=per_group,
    )


def _mix_weights(w_spatial, b_spatial, L, da):
    nh = w_spatial.shape[0]
    causal = jnp.tril(jnp.ones((L, L), dtype=bool))
    w = jnp.where(causal[None], w_spatial[:, :L, :L], 0.0)
    reps = TILE // L
    if reps > 1:
        eye = jnp.eye(reps, dtype=F32)
        w = jnp.einsum("ab,hts->hatbs", eye, w).reshape(nh, TILE, TILE)
    bias = jnp.tile(jnp.transpose(b_spatial[:, :L]), (reps, 1))
    bias_full = jnp.repeat(bias, da // nh, axis=1)
    return w.astype(BF16), bias_full


def kernel(x_prompt, x_sample, c_prompt, c_sample, state_mlstm_C, state_mlstm_n, state_mlstm_m, state_conv, w_ada, b_ada, g_norm1, g_norm2, w_in, ln_v_g, ln_v_b, w_spatial, b_spatial, w_conv, b_conv, w_q, w_k, w_v, b_i, b_f, g_mh, w_out, w_r1, b_r1, w_r2, b_r2, w_e_gate, w_e_up, w_e_down, g_final):
    weights = dict(w_in=w_in, ln_v_g=ln_v_g, ln_v_b=ln_v_b, w_conv=w_conv, b_conv=b_conv, w_q=w_q, w_k=w_k,
                   w_v=w_v, b_i=b_i, b_f=b_f, g_mh=g_mh, w_out=w_out, w_r1=w_r1, b_r1=b_r1, w_r2=w_r2, b_r2=b_r2,
                   g_norm1=g_norm1, g_norm2=g_norm2)
    depth = w_in.shape[0]
    bp, tp, d = x_prompt.shape
    bs, ts, _ = x_sample.shape
    nh = w_q.shape[1]
    da = ln_v_g.shape[1]
    db = w_conv.shape[2]
    n_mod = w_ada.shape[2] // d
    n_experts = w_e_gate.shape[1]
    n_p, n_s = bp * tp, bs * ts
    assert da == db and da % LANES == 0 and nh * LANES == db and tp % TILE == 0 and TILE % ts == 0
    assert ts >= CONV_W - 1 and ts % SUBLANES == 0 and n_s % TILE == 0
    xm_col, zo_col = 2 * da // db, 2 * da // db + 1

    mod = _ada(jnp.concatenate([c_prompt, c_sample], axis=0), w_ada, b_ada).reshape(depth, bp + bs, n_mod, 1, d)

    xp, xs = x_prompt, x_sample
    outs = {k: [] for k in ("cp", "np", "mp", "bp", "cs", "ns", "ms", "bs", "vs")}
    for l in range(depth):
        p = _layer_params(l, weights)
        mods_p = [mod[l, :bp, i] for i in range(n_mod)]
        mods_s = [mod[l, bp:, i] for i in range(n_mod)]

        proj_p, gates_p = _inproj(xp, mods_p[1], mods_p[0], p["g_norm1"], p["w_main"], p["w_gate"], 1024)
        w_mix_p, bias_p = _mix_weights(w_spatial[l], b_spatial[l], min(tp, TILE), da)
        (ya_p,) = _grpa(proj_p, p["ln_g"], p["ln_b"], w_mix_p, bias_p, da, False)
        yb_p, c_p, n_pn, m_p = _mlstm_prompt(proj_p, gates_p, p, bp, tp, db, nh, xm_col, zo_col)
        outs["cp"].append(c_p)
        outs["np"].append(n_pn.reshape(bp, nh, LANES))
        outs["mp"].append(m_p[:, 0, :nh])
        outs["bp"].append(proj_p.reshape(bp, tp, -1)[:, tp - (CONV_W - 1):, 2 * da:2 * da + db])

        proj_s, gates_s = _inproj(xs, mods_s[1], mods_s[0], p["g_norm1"], p["w_main"], p["w_gate"], 512)
        w_mix_s, bias_s = _mix_weights(w_spatial[l], b_spatial[l], min(ts, TILE), da)
        ya_s, v_s = _grpa(proj_s, p["ln_g"], p["ln_b"], w_mix_s, bias_s, da, True)
        prev_rows = jnp.pad(state_conv[l], ((0, 0), (ts - (CONV_W - 1), 0), (0, 0))).reshape(n_s, db)
        n0_rows = jnp.repeat(state_mlstm_n[l].reshape(bs, db), ts, axis=0)
        m0_rows = jnp.pad(jnp.repeat(state_mlstm_m[l], ts, axis=0), ((0, 0), (0, LANES - nh)))
        yb_s, c_s, n_rows_s, m_rows_s = _mlstm_sample(proj_s, gates_s, prev_rows, state_mlstm_C, l, n0_rows, m0_rows,
                                                      p, math.gcd(ts, TILE), db, nh, xm_col, zo_col)
        outs["cs"].append(c_s)
        outs["ns"].append(n_rows_s.reshape(bs, ts, nh, LANES)[:, ts - 1])
        outs["ms"].append(m_rows_s.reshape(bs, ts, LANES)[:, ts - 1, :nh])
        outs["bs"].append(proj_s.reshape(bs, ts, -1)[:, ts - (CONV_W - 1):, 2 * da:2 * da + db])
        outs["vs"].append(v_s.reshape(bs, ts, da))

        rargs = (p["w_router"], p["b_router"], p["n_groups"], p["per_group"])
        xp, h2_p, rt_p = _outproj(xp, ya_p, yb_p, p["w_out"], mods_p[2], mods_p[4], mods_p[3], p["g_norm2"], *rargs, 256)
        xs, h2_s, rt_s = _outproj(xs, ya_s, yb_s, p["w_out"], mods_s[2], mods_s[4], mods_s[3], p["g_norm2"], *rargs, 256)

        route = jnp.concatenate([rt_p, rt_s], axis=0)
        plan = _moe_plan(route[:, :TOP_K].astype(jnp.int32), n_experts)
        nck = d // LANES
        x_buf = _gather_rows(plan["row_src"], h2_p, h2_s, nck, BF16)
        h_buf = _expert_up(plan, x_buf, w_e_gate, w_e_up, l)
        y_buf = _expert_down(plan, h_buf, w_e_down, l)
        dest_t = jnp.transpose(plan["dest"].reshape(n_p + n_s, TOP_K)).reshape(-1)
        y_tok = _gather_rows(dest_t, y_buf, y_buf, nck, F32).reshape(TOP_K, n_p + n_s, d)
        last = l == depth - 1
        xp = _combine(xp, y_tok, 0, rt_p, mods_p[5], g_final, last, 512)
        xs = _combine(xs, y_tok, n_p, rt_s, mods_s[5], g_final, last, 256)

    st = jnp.stack
    return (xp, xs, st(outs["cp"]), st(outs["np"]), st(outs["mp"]), st(outs["bp"]),
            st(outs["cs"]), st(outs["ns"]), st(outs["ms"]), st(outs["bs"]), st(outs["vs"]))
```

```python
import functools
import math

import jax
import jax.numpy as jnp
from jax import lax
from jax.experimental import pallas as pl
from jax.experimental.pallas import tpu as pltpu

F32 = jnp.float32
BF16 = jnp.bfloat16
EPS = 1e-6
LANES = 128
SUBLANES = 8
TILE = 128
CONV_W = 4
TOP_K = 2
VMEM_LIMIT_BYTES = 56 * 1024 * 1024
MOE_BLOCK_ROWS = 256
GATHER_CHUNK = 512
CAST_CHUNK = 256
NEG_INF = float("-inf")


def _cparams(sem):
    return pltpu.CompilerParams(dimension_semantics=sem, vmem_limit_bytes=VMEM_LIMIT_BYTES)


def _split3(x):
    x1 = x.astype(BF16)
    r1 = x - x1.astype(F32)
    x2 = r1.astype(BF16)
    r2 = r1 - x2.astype(F32)
    return x1, x2, r2.astype(BF16)


def _ada_kernel(c_ref, w_ref, b_ref, o_ref):
    c = c_ref[...]
    a = c * jax.nn.sigmoid(c)
    o_ref[0] = jnp.dot(a.astype(BF16), w_ref[0].astype(BF16), preferred_element_type=F32) + b_ref[0]


def _ada(c_all, w_ada, b_ada):
    depth, d, n = w_ada.shape
    bc = c_all.shape[0]
    tn = min(1024, n)
    return pl.pallas_call(
        _ada_kernel,
        grid=(depth, n // tn),
        in_specs=[pl.BlockSpec((bc, d), lambda l, j: (0, 0)),
                  pl.BlockSpec((1, d, tn), lambda l, j: (l, 0, j)),
                  pl.BlockSpec((1, 1, tn), lambda l, j: (l, 0, j))],
        out_specs=pl.BlockSpec((1, bc, tn), lambda l, j: (l, 0, j)),
        out_shape=jax.ShapeDtypeStruct((depth, bc, n), F32),
        compiler_params=_cparams(("arbitrary", "arbitrary")),
        name="ada",
    )(c_all, w_ada, b_ada.reshape(depth, 1, n))


def _inproj_kernel(x_ref, sc_ref, sh_ref, g_ref, w_ref, wg_ref, o_ref, og_ref, hn_ref):
    @pl.when(pl.program_id(2) == 0)
    def _():
        x = x_ref[...]
        ms = jnp.mean(x * x, axis=-1, keepdims=True)
        h = x * lax.rsqrt(ms + EPS) * g_ref[...]
        h = h * (1.0 + sc_ref[...]) + sh_ref[...]
        hb = h.reshape(hn_ref.shape).astype(BF16)
        hn_ref[...] = hb
        og_ref[...] = jnp.dot(hb, wg_ref[...], preferred_element_type=F32)

    o_ref[...] = jnp.dot(hn_ref[...], w_ref[...], preferred_element_type=F32)


def _row_tiling(b, t, max_rows):
    if t >= max_rows:
        assert t % max_rows == 0
        return 1, max_rows
    bb = min(b, max_rows // t)
    assert b % bb == 0 and t % SUBLANES == 0
    return bb, t


def _inproj(x, scale, shift, g_norm, w_main, n, w_gate, max_rows):
    b, t, d = x.shape
    ng = w_gate.shape[1]
    bb, tt = _row_tiling(b, t, max_rows)
    rows = bb * tt
    tn = min(512, n)
    nt = t // tt
    return pl.pallas_call(
        _inproj_kernel,
        grid=(b // bb, nt, n // tn),
        in_specs=[pl.BlockSpec((bb, tt, d), lambda i, k, j: (i, k, 0)),
                  pl.BlockSpec((bb, 1, d), lambda i, k, j: (i, 0, 0)),
                  pl.BlockSpec((bb, 1, d), lambda i, k, j: (i, 0, 0)),
                  pl.BlockSpec((1, 1, d), lambda i, k, j: (0, 0, 0)),
                  pl.BlockSpec((d, tn), lambda i, k, j: (0, j)),
                  pl.BlockSpec((d, ng), lambda i, k, j: (0, 0))],
        out_specs=[pl.BlockSpec((rows, tn), lambda i, k, j: (i * nt + k, j)),
                   pl.BlockSpec((rows, ng), lambda i, k, j: (i * nt + k, 0))],
        out_shape=[jax.ShapeDtypeStruct((b * t, n), F32), jax.ShapeDtypeStruct((b * t, ng), F32)],
        scratch_shapes=[pltpu.VMEM((rows, d), BF16)],
        compiler_params=_cparams(("arbitrary", "arbitrary", "arbitrary")),
        name="inproj",
    )(x, scale, shift, g_norm.reshape(1, 1, d), w_main, w_gate)


def _grpa_kernel(u_ref, v_ref, lg_ref, lb_ref, w_ref, bias_ref, ya_ref, *maybe_v_out):
    rows, da = u_ref.shape
    nh = w_ref.shape[0]
    g = jax.nn.gelu(v_ref[...])
    mu = jnp.mean(g, axis=-1, keepdims=True)
    gc = g - mu
    var = jnp.mean(gc * gc, axis=-1, keepdims=True)
    v = gc * lax.rsqrt(var + EPS) * lg_ref[...] + lb_ref[...]
    if maybe_v_out:
        maybe_v_out[0][...] = v
    vb = v.astype(BF16)
    for c in range(rows // TILE):
        r0 = c * TILE
        for h in range(nh):
            c0 = h * LANES
            z = jnp.dot(w_ref[h], vb[r0:r0 + TILE, c0:c0 + LANES], preferred_element_type=F32)
            u = jax.nn.gelu(u_ref[r0:r0 + TILE, c0:c0 + LANES])
            y = u * (z + bias_ref[:, c0:c0 + LANES])
            ya_ref[r0:r0 + TILE, c0:c0 + LANES] = y.astype(BF16)


def _grpa(proj, ln_g, ln_b, w_mix, bias_full, da, want_v):
    n_rows = proj.shape[0]
    rows = min(512, n_rows)
    nh = w_mix.shape[0]
    out_specs = [pl.BlockSpec((rows, da), lambda i: (i, 0))]
    out_shape = [jax.ShapeDtypeStruct((n_rows, da), BF16)]
    if want_v:
        out_specs.append(pl.BlockSpec((rows, da), lambda i: (i, 0)))
        out_shape.append(jax.ShapeDtypeStruct((n_rows, da), F32))
    return pl.pallas_call(
        _grpa_kernel,
        grid=(n_rows // rows,),
        in_specs=[pl.BlockSpec((rows, da), lambda i: (i, 0)),
                  pl.BlockSpec((rows, da), lambda i: (i, 1)),
                  pl.BlockSpec((1, da), lambda i: (0, 0)),
                  pl.BlockSpec((1, da), lambda i: (0, 0)),
                  pl.BlockSpec((nh, TILE, TILE), lambda i: (0, 0, 0)),
                  pl.BlockSpec((TILE, da), lambda i: (0, 0))],
        out_specs=out_specs,
        out_shape=out_shape,
        compiler_params=_cparams(("arbitrary",)),
        name="grpa",
    )(proj, proj, ln_g.reshape(1, da), ln_b.reshape(1, da), w_mix, bias_full)


def _group_last(x, L):
    if L == TILE:
        return jnp.broadcast_to(x[TILE - 1:TILE, :], x.shape)
    g = TILE // L
    x3 = x.reshape(g, L, x.shape[1])
    return jnp.broadcast_to(x3[:, L - 1:L, :], x3.shape).reshape(x.shape)


def _group_sum(x, L):
    if L == TILE:
        return jnp.broadcast_to(jnp.sum(x, axis=0, keepdims=True), x.shape)
    g = TILE // L
    x3 = x.reshape(g, L, x.shape[1])
    return jnp.broadcast_to(jnp.sum(x3, axis=1, keepdims=True), x3.shape).reshape(x.shape)


def _mlstm_kernel(*refs, L, carry, nh, k_scale):
    if carry:
        (xm_ref, zo_ref, g_ref, wconv_ref, bconv_ref, wq_ref, wk_ref, wkt_ref, wv_ref, bi_ref, bf_ref, gmh_ref,
         yb_ref, c_ref, n_ref, m_ref, prev_scr) = refs
        c_in_ref = c_ref
        first = pl.program_id(1) == 0

        @pl.when(first)
        def _():
            c_ref[...] = jnp.zeros(c_ref.shape, F32)
            n_ref[...] = jnp.zeros(n_ref.shape, F32)
            m_ref[...] = jnp.zeros(m_ref.shape, F32)
            prev_scr[...] = jnp.zeros(prev_scr.shape, F32)

        prev = prev_scr[...]
        n0_rows = jnp.broadcast_to(n_ref[0], (TILE, n_ref.shape[2]))
        m0 = jnp.broadcast_to(m_ref[0], (TILE, LANES))
    else:
        (xm_ref, zo_ref, g_ref, prev_ref, c_layer_ref, n0_ref, m0_ref,
         wconv_ref, bconv_ref, wq_ref, wk_ref, wkt_ref, wv_ref, bi_ref, bf_ref, gmh_ref,
         yb_ref, c_ref, n_ref, m_ref) = refs
        c_in_ref = c_layer_ref.at[0]
        prev = prev_ref[...]
        n0_rows = n0_ref[...]
        m0 = m0_ref[...]

    groups = TILE // L
    x = xm_ref[...]
    row_in_group = lax.broadcasted_iota(jnp.int32, (TILE, 1), 0) % L

    acc = bconv_ref[...] + wconv_ref[CONV_W - 1:CONV_W, :] * x
    for j in range(1, CONV_W):
        from_prev = pltpu.roll(prev, (j - L) % TILE, 0)
        shifted = jnp.where(row_in_group < j, from_prev, pltpu.roll(x, j, 0))
        acc = acc + wconv_ref[CONV_W - 1 - j:CONV_W - j, :] * shifted
    xc = acc * jax.nn.sigmoid(acc)
    if carry:
        prev_scr[...] = x

    gates = g_ref[...]
    gi = gates[:, :LANES] + bi_ref[...]
    gf = gates[:, LANES:] + bf_ref[...]
    lf = jnp.minimum(gf, 0.0) - jnp.log1p(jnp.exp(-jnp.abs(gf)))

    r_i = lax.broadcasted_iota(jnp.int32, (TILE, TILE), 0)
    c_i = lax.broadcasted_iota(jnp.int32, (TILE, TILE), 1)
    mask = (c_i <= r_i) & ((c_i // L) == (r_i // L)) if L != TILE else (c_i <= r_i)
    mask_b = jnp.where(mask, 1.0, 0.0).astype(BF16)
    lf1, lf2, lf3 = _split3(lf)
    cum = (jnp.dot(mask_b, lf1, preferred_element_type=F32) + jnp.dot(mask_b, lf2, preferred_element_type=F32)
           + jnp.dot(mask_b, lf3, preferred_element_type=F32))
    cum_t = cum.T
    gi_t = gi.T
    log_inter = cum + m0
    lane = lax.broadcasted_iota(jnp.int32, (TILE, LANES), 1)

    def d_matrix(h):
        d = cum[:, h:h + 1] - cum_t[h:h + 1, :] + gi_t[h:h + 1, :]
        return jnp.where(mask, d, NEG_INF)

    mt = jnp.zeros((TILE, LANES), F32)
    for h in range(nh):
        mt_h = jnp.maximum(log_inter[:, h:h + 1], jnp.max(d_matrix(h), axis=1, keepdims=True))
        mt = jnp.where(lane == h, mt_h, mt)

    m_last = _group_last(mt, L)
    w_inter = jnp.exp(log_inter - mt)
    w_last = jnp.exp(_group_last(cum, L) - cum + gi - m_last)
    w_last_t = w_last.T
    inv_floor = jnp.exp(-mt)

    for h in range(nh):
        c0 = h * LANES
        xc_h = xc[:, c0:c0 + LANES].astype(BF16)
        q = jnp.dot(xc_h, wq_ref[h], preferred_element_type=F32)
        k = jnp.dot(xc_h, wk_ref[h], preferred_element_type=F32) * k_scale
        k_t = lax.dot_general(wkt_ref[h], xc_h, (((1,), (1,)), ((), ())), preferred_element_type=F32) * k_scale
        v = jnp.dot(x[:, c0:c0 + LANES].astype(BF16), wv_ref[h], preferred_element_type=F32)
        qb = q.astype(BF16)
        vb = v.astype(BF16)
        mt_h = mt[:, h:h + 1]
        wi_h = w_inter[:, h:h + 1]
        w_intra = jnp.exp(d_matrix(h) - mt_h)
        s = jnp.dot(qb, k_t.astype(BF16), preferred_element_type=F32) * w_intra

        if carry:
            qc = jnp.dot(qb, c_in_ref[0, h].astype(BF16), preferred_element_type=F32)
        else:
            qc = jnp.concatenate(
                [jnp.dot(q[g * L:(g + 1) * L, :].astype(BF16), c_in_ref[g, h].astype(BF16),
                         preferred_element_type=F32)
                 for g in range(groups)], axis=0)

        num = wi_h * qc + jnp.dot(s.astype(BF16), vb, preferred_element_type=F32)
        qn = jnp.sum(q * n0_rows[:, c0:c0 + LANES], axis=1, keepdims=True)
        den = wi_h * qn + jnp.sum(s, axis=1, keepdims=True)
        hh = num / jnp.maximum(jnp.abs(den), inv_floor[:, h:h + 1])
        hh = hh * lax.rsqrt(jnp.mean(hh * hh, axis=1, keepdims=True) + EPS) * gmh_ref[:, c0:c0 + LANES]
        yb_ref[:, c0:c0 + LANES] = (jax.nn.sigmoid(zo_ref[:, c0:c0 + LANES]) * hh).astype(BF16)

        kw_t = k_t * w_last_t[h:h + 1, :]
        n_rows = wi_h * n0_rows[:, c0:c0 + LANES] + _group_sum(w_last[:, h:h + 1] * k, L)
        if carry:
            decay = jnp.broadcast_to(w_inter[TILE - 1:TILE, h:h + 1], (LANES, LANES))
            c_ref[0, h] = decay * c_in_ref[0, h] + jnp.dot(kw_t.astype(BF16), vb, preferred_element_type=F32)
            n_ref[0, :, c0:c0 + LANES] = n_rows[TILE - 1:TILE, :]
        else:
            n_ref[:, c0:c0 + LANES] = n_rows
            for g in range(groups):
                kw_g = jnp.where((c_i // L) == g, kw_t, 0.0).astype(BF16)
                upd = jnp.dot(kw_g, vb, preferred_element_type=F32)
                last = g * L + L - 1
                decay = jnp.broadcast_to(w_inter[last:last + 1, h:h + 1], (LANES, LANES))
                c_ref[g, h] = decay * c_in_ref[g, h] + upd

    if carry:
        m_ref[0] = mt[TILE - 1:TILE, :]
    else:
        m_ref[...] = mt


def _mlstm_weights(p):
    return [p["w_conv"], p["b_conv"], p["w_q"], p["w_k"], p["w_kt"], p["w_v"], p["b_i"], p["b_f"], p["g_mh"]]


def _mlstm_weight_specs(nh, db, nidx):
    z2 = (lambda i, c: (0, 0)) if nidx == 2 else (lambda i: (0, 0))
    z3 = (lambda i, c: (0, 0, 0)) if nidx == 2 else (lambda i: (0, 0, 0))
    hw = pl.BlockSpec((nh, LANES, LANES), z3)
    return [pl.BlockSpec((CONV_W, db), z2), pl.BlockSpec((1, db), z2), hw, hw, hw, hw,
            pl.BlockSpec((1, LANES), z2), pl.BlockSpec((1, LANES), z2), pl.BlockSpec((1, db), z2)]


def _mlstm_prompt(proj, gates, p, b, t, db, nh, xm_col, zo_col):
    nt = t // TILE
    kern = functools.partial(_mlstm_kernel, L=TILE, carry=True, nh=nh, k_scale=p["k_scale"])
    return pl.pallas_call(
        kern,
        grid=(b, nt),
        in_specs=[pl.BlockSpec((TILE, db), lambda i, c: (i * nt + c, xm_col)),
                  pl.BlockSpec((TILE, db), lambda i, c: (i * nt + c, zo_col)),
                  pl.BlockSpec((TILE, 2 * LANES), lambda i, c: (i * nt + c, 0))] + _mlstm_weight_specs(nh, db, 2),
        out_specs=[pl.BlockSpec((TILE, db), lambda i, c: (i * nt + c, 0)),
                   pl.BlockSpec((1, nh, LANES, LANES), lambda i, c: (i, 0, 0, 0)),
                   pl.BlockSpec((1, 1, db), lambda i, c: (i, 0, 0)),
                   pl.BlockSpec((1, 1, LANES), lambda i, c: (i, 0, 0))],
        out_shape=[jax.ShapeDtypeStruct((b * t, db), BF16),
                   jax.ShapeDtypeStruct((b, nh, LANES, LANES), F32),
                   jax.ShapeDtypeStruct((b, 1, db), F32),
                   jax.ShapeDtypeStruct((b, 1, LANES), F32)],
        scratch_shapes=[pltpu.VMEM((TILE, db), F32)],
        compiler_params=_cparams(("arbitrary", "arbitrary")),
        name="mlstm_prompt",
    )(proj, proj, gates, *_mlstm_weights(p))


def _mlstm_sample(proj, gates, prev_rows, c0, layer, n0_rows, m0_rows, p, L, db, nh, xm_col, zo_col):
    n_rows = proj.shape[0]
    groups = TILE // L
    kern = functools.partial(_mlstm_kernel, L=L, carry=False, nh=nh, k_scale=p["k_scale"])
    row_spec = pl.BlockSpec((TILE, db), lambda i: (i, 0))
    return pl.pallas_call(
        kern,
        grid=(n_rows // TILE,),
        in_specs=[pl.BlockSpec((TILE, db), lambda i: (i, xm_col)),
                  pl.BlockSpec((TILE, db), lambda i: (i, zo_col)),
                  pl.BlockSpec((TILE, 2 * LANES), lambda i: (i, 0)),
                  row_spec,
                  pl.BlockSpec((1, groups, nh, LANES, LANES), lambda i: (layer, i, 0, 0, 0)),
                  row_spec,
                  pl.BlockSpec((TILE, LANES), lambda i: (i, 0))] + _mlstm_weight_specs(nh, db, 1),
        out_specs=[row_spec,
                   pl.BlockSpec((groups, nh, LANES, LANES), lambda i: (i, 0, 0, 0)),
                   row_spec,
                   pl.BlockSpec((TILE, LANES), lambda i: (i, 0))],
        out_shape=[jax.ShapeDtypeStruct((n_rows, db), BF16),
                   jax.ShapeDtypeStruct(c0.shape[1:], F32),
                   jax.ShapeDtypeStruct((n_rows, db), F32),
                   jax.ShapeDtypeStruct((n_rows, LANES), F32)],
        compiler_params=_cparams(("arbitrary",)),
        name="mlstm_sample",
    )(proj, proj, gates, prev_rows, c0, n0_rows, m0_rows, *_mlstm_weights(p))


def _store_token_major(ref, val):
    n, d = val.shape
    nck = d // LANES
    for c in range(nck):
        ref[pl.ds(c, n, stride=nck), :] = val[:, c * LANES:(c + 1) * LANES]


def _load_token_major(ref, n, c, nck):
    return ref[pl.ds(c, n, stride=nck), :]


def _route(lg, n_groups, per_group):
    lane_i = lax.broadcasted_iota(jnp.int32, lg.shape, 1)
    lane = lane_i.astype(F32)
    far = float(LANES)
    is_g = lane_i < n_groups
    lgg = jnp.where(is_g, lg, NEG_INF)
    mg = jnp.max(lgg, axis=1, keepdims=True)
    g_sel = jnp.min(jnp.where(lgg == mg, lane, far), axis=1, keepdims=True)
    p_g = 1.0 / jnp.sum(jnp.where(is_g, jnp.exp(lgg - mg), 0.0), axis=1, keepdims=True)
    lo = n_groups + g_sel * per_group
    sel = (lane >= lo) & (lane < lo + per_group)
    le = jnp.where(sel, lg, NEG_INF)
    m1 = jnp.max(le, axis=1, keepdims=True)
    i1 = jnp.min(jnp.where(le == m1, lane, far), axis=1, keepdims=True)
    le2 = jnp.where(lane == i1, NEG_INF, le)
    m2 = jnp.max(le2, axis=1, keepdims=True)
    i2 = jnp.min(jnp.where(le2 == m2, lane, far), axis=1, keepdims=True)
    ratio = jnp.exp(m2 - m1)
    gate1 = p_g / (1.0 + ratio)
    gate2 = p_g * ratio / (1.0 + ratio)
    out = jnp.where(lane_i == 0, i1 - n_groups, 0.0)
    out = jnp.where(lane_i == 1, i2 - n_groups, out)
    out = jnp.where(lane_i == 2, gate1, out)
    return jnp.where(lane_i == 3, gate2, out)


def _outproj_kernel(x_ref, ya_ref, yb_ref, w_ref, g1_ref, sc_ref, sh_ref, gn_ref, wr_ref, br_ref,
                    xn_ref, h2_ref, rt_ref, *, n_groups, per_group):
    da = ya_ref.shape[1]
    mix = (jnp.dot(ya_ref[...], w_ref[:da, :], preferred_element_type=F32)
           + jnp.dot(yb_ref[...], w_ref[da:, :], preferred_element_type=F32))
    x = x_ref[...] + g1_ref[...] * mix.reshape(x_ref.shape)
    xn_ref[...] = x
    ms = jnp.mean(x * x, axis=-1, keepdims=True)
    h = x * lax.rsqrt(ms + EPS) * gn_ref[...]
    rows = ya_ref.shape[0]
    h = (h * (1.0 + sc_ref[...]) + sh_ref[...]).reshape(rows, x_ref.shape[2])
    _store_token_major(h2_ref, h)
    h_hi = h.astype(BF16)
    h_lo = (h - h_hi.astype(F32)).astype(BF16)
    lg = (jnp.dot(h_hi, wr_ref[0], preferred_element_type=F32) + jnp.dot(h_lo, wr_ref[0], preferred_element_type=F32)
          + jnp.dot(h_hi, wr_ref[1], preferred_element_type=F32) + br_ref[...])
    rt_ref[...] = _route(lg, n_groups, per_group)


def _outproj(x, ya, yb, w_out, gate, scale, shift, g_norm, w_router, b_router, n_groups, per_group, max_rows):
    b, t, d = x.shape
    bb, tt = _row_tiling(b, t, max_rows)
    rows = bb * tt
    nt = t // tt
    da = ya.shape[1]
    xspec = pl.BlockSpec((bb, tt, d), lambda i, k: (i, k, 0))
    mspec = pl.BlockSpec((bb, 1, d), lambda i, k: (i, 0, 0))
    kern = functools.partial(_outproj_kernel, n_groups=n_groups, per_group=per_group)
    return pl.pallas_call(
        kern,
        grid=(b // bb, nt),
        in_specs=[xspec,
                  pl.BlockSpec((rows, da), lambda i, k: (i * nt + k, 0)),
                  pl.BlockSpec((rows, ya.shape[1]), lambda i, k: (i * nt + k, 0)),
                  pl.BlockSpec(w_out.shape, lambda i, k: (0, 0)),
                  mspec, mspec, mspec,
                  pl.BlockSpec((1, 1, d), lambda i, k: (0, 0, 0)),
                  pl.BlockSpec(w_router.shape, lambda i, k: (0, 0, 0)),
                  pl.BlockSpec((1, LANES), lambda i, k: (0, 0))],
        out_specs=[xspec,
                   pl.BlockSpec((rows * (d // LANES), LANES), lambda i, k: (i * nt + k, 0)),
                   pl.BlockSpec((rows, LANES), lambda i, k: (i * nt + k, 0))],
        out_shape=[jax.ShapeDtypeStruct(x.shape, F32),
                   jax.ShapeDtypeStruct((b * t * (d // LANES), LANES), F32),
                   jax.ShapeDtypeStruct((b * t, LANES), F32)],
        compiler_params=_cparams(("arbitrary", "arbitrary")),
        name="outproj",
    )(x, ya, yb, w_out, gate, scale, shift, g_norm.reshape(1, 1, d), w_router, b_router)


def _gather_kernel(idx_ref, src_a_ref, src_b_ref, out_ref, buf, sem, *, n_a, chunk, nck):
    i = pl.program_id(0)

    def token(ref, t):
        return ref.at[pl.ds(pl.multiple_of(t * nck, nck), nck)] if nck > 1 else ref.at[pl.ds(t, 1)]

    def issue_step(step, slot):
        def issue(r, _):
            t = idx_ref[step * chunk + r]
            dst = token(buf.at[slot], r)

            @pl.when(t < n_a)
            def _():
                pltpu.make_async_copy(token(src_a_ref, t), dst, sem.at[slot]).start()

            @pl.when(t >= n_a)
            def _():
                pltpu.make_async_copy(token(src_b_ref, t - n_a), dst, sem.at[slot]).start()

            return 0

        lax.fori_loop(0, chunk, issue, 0, unroll=8)

    @pl.when(i == 0)
    def _():
        issue_step(0, 0)

    @pl.when(i + 1 < pl.num_programs(0))
    def _():
        issue_step(i + 1, (i + 1) % 2)

    slot = i % 2
    pltpu.make_async_copy(buf.at[slot], buf.at[slot], sem.at[slot]).wait()
    if nck > 1:
        for c in range(nck):
            chunk_c = _load_token_major(buf.at[slot], chunk, c, nck)
            out_ref[:, c * LANES:(c + 1) * LANES] = chunk_c.astype(out_ref.dtype)
    else:
        out_ref[...] = buf[slot].astype(out_ref.dtype)


def _gather_rows(idx, src_a, src_b, nck, out_dtype):
    n_out = idx.shape[0]
    chunk = math.gcd(GATHER_CHUNK, n_out)
    d = src_a.shape[1]
    kern = functools.partial(_gather_kernel, n_a=src_a.shape[0] // nck, chunk=chunk, nck=nck)
    return pl.pallas_call(
        kern,
        grid_spec=pltpu.PrefetchScalarGridSpec(
            num_scalar_prefetch=1,
            grid=(n_out // chunk,),
            in_specs=[pl.BlockSpec(memory_space=pl.ANY), pl.BlockSpec(memory_space=pl.ANY)],
            out_specs=pl.BlockSpec((chunk, nck * d), lambda i, idx: (i, 0)),
            scratch_shapes=[pltpu.VMEM((2, chunk * nck, d), src_a.dtype), pltpu.SemaphoreType.DMA((2,))]),
        out_shape=jax.ShapeDtypeStruct((n_out, nck * d), out_dtype),
        compiler_params=_cparams(("arbitrary",)),
        name="gather_rows",
    )(idx, src_a, src_b)


def _expert_up_kernel(ie_ref, ij_ref, ib_ref, ioj_ref, inew_ref, islot_ref, ine_ref, inj_ref, ihas_ref, n_ref,
                      x_ref, wg_hbm, wu_hbm, o_ref, stage, wgb_scr, wub_scr, sem, *, layer):
    i = pl.program_id(0)
    tn = wgb_scr.shape[1]

    def weight_copies(e, j, slot):
        col = pl.multiple_of(j * tn, tn)
        return (pltpu.make_async_copy(wg_hbm.at[layer, e, :, pl.ds(col, tn)], stage.at[slot, 0], sem.at[slot, 0]),
                pltpu.make_async_copy(wu_hbm.at[layer, e, :, pl.ds(col, tn)], stage.at[slot, 1], sem.at[slot, 1]))

    @pl.when(i == 0)
    def _():
        for cp in weight_copies(ie_ref[0], ij_ref[0], islot_ref[0]):
            cp.start()

    @pl.when(i >= n_ref[0])
    def _():
        o_ref[...] = jnp.zeros(o_ref.shape, o_ref.dtype)

    @pl.when(i < n_ref[0])
    def _():
        @pl.when(inew_ref[i] == 1)
        def _():
            slot = islot_ref[i]

            @pl.when(ihas_ref[i] == 1)
            def _():
                for cp in weight_copies(ine_ref[i], inj_ref[i], 1 - slot):
                    cp.start()

            for cp in weight_copies(ie_ref[i], ij_ref[i], slot):
                cp.wait()
            a = u = None
            for k0 in range(0, x_ref.shape[1], CAST_CHUNK):
                ks = slice(k0, k0 + CAST_CHUNK)
                wg_k = stage[slot, 0, ks, :].astype(BF16)
                wu_k = stage[slot, 1, ks, :].astype(BF16)
                wgb_scr[ks, :] = wg_k
                wub_scr[ks, :] = wu_k
                a_k = jnp.dot(x_ref[:, ks], wg_k, preferred_element_type=F32)
                u_k = jnp.dot(x_ref[:, ks], wu_k, preferred_element_type=F32)
                a = a_k if a is None else a + a_k
                u = u_k if u is None else u + u_k
            o_ref[...] = (a * jax.nn.sigmoid(a) * u).astype(BF16)

        @pl.when(inew_ref[i] != 1)
        def _():
            xb = x_ref[...]
            a = jnp.dot(xb, wgb_scr[...], preferred_element_type=F32)
            u = jnp.dot(xb, wub_scr[...], preferred_element_type=F32)
            o_ref[...] = (a * jax.nn.sigmoid(a) * u).astype(BF16)


def _expert_up(plan, x_buf, w_gate, w_up, layer):
    d, de = w_gate.shape[2], w_gate.shape[3]
    n_rows = x_buf.shape[0]
    tn = de // 2
    bm = MOE_BLOCK_ROWS
    up = plan["up"]
    n_items = up["e"].shape[0]
    names = ("e", "j", "b", "oj", "new", "slot", "next_e", "next_j", "has_next", "n")
    return pl.pallas_call(
        functools.partial(_expert_up_kernel, layer=layer),
        grid_spec=pltpu.PrefetchScalarGridSpec(
            num_scalar_prefetch=len(names),
            grid=(n_items,),
            in_specs=[pl.BlockSpec((bm, d), lambda i, *s: (s[2][i], 0)),
                      pl.BlockSpec(memory_space=pl.ANY), pl.BlockSpec(memory_space=pl.ANY)],
            out_specs=pl.BlockSpec((bm, tn), lambda i, *s: (s[2][i], s[3][i])),
            scratch_shapes=[pltpu.VMEM((2, 2, d, tn), F32), pltpu.VMEM((d, tn), BF16), pltpu.VMEM((d, tn), BF16),
                            pltpu.SemaphoreType.DMA((2, 2))]),
        out_shape=jax.ShapeDtypeStruct((n_rows, de), BF16),
        compiler_params=_cparams(("arbitrary",)),
        name="expert_up",
    )(*[up[k] for k in names], x_buf, w_gate, w_up)


def _expert_down_kernel(be_ref, bnew_ref, bslot_ref, bne_ref, bhas_ref, n_ref, h_ref, wd_hbm, o_ref,
                        stage, wdb_scr, sem, *, layer):
    i = pl.program_id(0)

    def weight_copy(e, slot):
        return pltpu.make_async_copy(wd_hbm.at[layer, e], stage.at[slot], sem.at[slot])

    @pl.when(i == 0)
    def _():
        weight_copy(be_ref[0], bslot_ref[0]).start()

    @pl.when(i >= n_ref[0])
    def _():
        o_ref[...] = jnp.zeros(o_ref.shape, o_ref.dtype)

    @pl.when(i < n_ref[0])
    def _():
        @pl.when(bnew_ref[i] == 1)
        def _():
            slot = bslot_ref[i]

            @pl.when(bhas_ref[i] == 1)
            def _():
                weight_copy(bne_ref[i], 1 - slot).start()

            weight_copy(be_ref[i], slot).wait()
            y = None
            for k0 in range(0, h_ref.shape[1], CAST_CHUNK):
                ks = slice(k0, k0 + CAST_CHUNK)
                wd_k = stage[slot, ks, :].astype(BF16)
                wdb_scr[ks, :] = wd_k
                y_k = jnp.dot(h_ref[:, ks], wd_k, preferred_element_type=F32)
                y = y_k if y is None else y + y_k
            o_ref[...] = y

        @pl.when(bnew_ref[i] != 1)
        def _():
            o_ref[...] = jnp.dot(h_ref[...], wdb_scr[...], preferred_element_type=F32)


def _expert_down(plan, h_buf, w_down, layer):
    n_rows, de = h_buf.shape
    d = w_down.shape[3]
    bm = MOE_BLOCK_ROWS
    down = plan["down"]
    n_blocks = down["e"].shape[0]
    names = ("e", "new", "slot", "next_e", "has_next", "n")
    return pl.pallas_call(
        functools.partial(_expert_down_kernel, layer=layer),
        grid_spec=pltpu.PrefetchScalarGridSpec(
            num_scalar_prefetch=len(names),
            grid=(n_blocks,),
            in_specs=[pl.BlockSpec((bm, de), lambda i, *s: (i, 0)),
                      pl.BlockSpec(memory_space=pl.ANY)],
            out_specs=pl.BlockSpec((bm, d), lambda i, *s: (i, 0)),
            scratch_shapes=[pltpu.VMEM((2, de, d), F32), pltpu.VMEM((de, d), BF16), pltpu.SemaphoreType.DMA((2,))]),
        out_shape=jax.ShapeDtypeStruct((n_rows, d), F32),
        compiler_params=_cparams(("arbitrary",)),
        name="expert_down",
    )(*[down[k] for k in names], h_buf, w_down)


def _tile_schedule(new, valid, coords):
    n = new.shape[0]
    idx = jnp.arange(n, dtype=jnp.int32)
    starts = jnp.where((new == 1) & valid, idx, n)
    first_at_or_after = lax.cummin(starts[::-1])[::-1]
    nxt = jnp.concatenate([first_at_or_after[1:], jnp.full((1,), n, jnp.int32)])
    has_next = (nxt < n).astype(jnp.int32)
    nxt_c = jnp.minimum(nxt, n - 1)
    slot = ((jnp.cumsum(jnp.where(valid, new, 0)) - 1) % 2).astype(jnp.int32)
    return slot, has_next, [c[nxt_c] for c in coords]


def _moe_plan(expert, n_experts):
    bm = MOE_BLOCK_ROWS
    n_tok = expert.shape[0]
    n_asg = n_tok * TOP_K
    e_flat = expert.reshape(-1)
    onehot = (e_flat[:, None] == jnp.arange(n_experts, dtype=jnp.int32)[None, :]).astype(jnp.int32)
    csum = jnp.cumsum(onehot, axis=0)
    counts = csum[-1]
    rank = jnp.take_along_axis(csum, e_flat[:, None], axis=1)[:, 0] - 1
    nblk = (counts + bm - 1) // bm
    bend = jnp.cumsum(nblk)
    bstart = bend - nblk
    dest = bstart[e_flat] * bm + rank
    n_blocks = (n_asg + n_experts * (bm - 1) + bm - 1) // bm
    n_rows = -(-n_blocks * bm // GATHER_CHUNK) * GATHER_CHUNK
    tok = jnp.arange(n_asg, dtype=jnp.int32) // TOP_K
    row_src = (jnp.arange(n_rows, dtype=jnp.int32) % n_tok).at[dest].set(tok, unique_indices=True)
    used = bend[-1]

    def expert_of_block(b):
        return jnp.minimum(jnp.sum((bend[None, :] <= b[:, None]).astype(jnp.int32), axis=1), n_experts - 1)

    blk = jnp.arange(n_blocks, dtype=jnp.int32)
    blk_e = expert_of_block(jnp.minimum(blk, used - 1))
    blk_new = (blk == bstart[blk_e]).astype(jnp.int32)
    n_items = 2 * n_blocks
    idx = jnp.arange(n_items, dtype=jnp.int32)
    it = jnp.minimum(idx, 2 * used - 1)
    item_e = expert_of_block(it // 2)
    off = it - 2 * bstart[item_e]
    item_j = (off // nblk[item_e]).astype(jnp.int32)
    item_b = (bstart[item_e] + off % nblk[item_e]).astype(jnp.int32)
    item_new = (item_b == bstart[item_e]).astype(jnp.int32)
    spare = idx - 2 * used
    is_spare = spare >= 0
    item_b = jnp.where(is_spare, used + spare // 2, item_b)
    item_oj = jnp.where(is_spare, spare % 2, item_j)
    d_slot, d_has, (d_ne,) = _tile_schedule(blk_new, blk < used, [blk_e])
    u_slot, u_has, (u_ne, u_nj) = _tile_schedule(item_new, ~is_spare, [item_e, item_j])
    down = dict(e=blk_e, new=blk_new, slot=d_slot, next_e=d_ne, has_next=d_has,
                n=used.reshape(1).astype(jnp.int32))
    up = dict(e=item_e, j=item_j, b=item_b, oj=item_oj, new=item_new, slot=u_slot, next_e=u_ne, next_j=u_nj,
              has_next=u_has, n=(2 * used).reshape(1).astype(jnp.int32))
    return dict(dest=dest, row_src=row_src, up=up, down=down)


def _combine_kernel(x_ref, y_ref, rt_ref, g2_ref, gf_ref, o_ref, *, final_norm):
    rt = rt_ref[...]
    y = rt[:, 2:3] * y_ref[0] + rt[:, 3:4] * y_ref[1]
    x = x_ref[...] + g2_ref[...] * y.reshape(x_ref.shape)
    if final_norm:
        ms = jnp.mean(x * x, axis=-1, keepdims=True)
        x = x * lax.rsqrt(ms + EPS) * gf_ref[...]
    o_ref[...] = x


def _combine(x, y_tok, row0, route, gate, g_final, final_norm, max_rows):
    b, t, d = x.shape
    bb, tt = _row_tiling(b, t, max_rows)
    rows = bb * tt
    nt = t // tt
    assert row0 % rows == 0
    blk0 = row0 // rows
    xspec = pl.BlockSpec((bb, tt, d), lambda i, k: (i, k, 0))
    kern = functools.partial(_combine_kernel, final_norm=final_norm)
    return pl.pallas_call(
        kern,
        grid=(b // bb, nt),
        in_specs=[xspec,
                  pl.BlockSpec((TOP_K, rows, d), lambda i, k: (0, blk0 + i * nt + k, 0)),
                  pl.BlockSpec((rows, LANES), lambda i, k: (i * nt + k, 0)),
                  pl.BlockSpec((bb, 1, d), lambda i, k: (i, 0, 0)),
                  pl.BlockSpec((1, 1, d), lambda i, k: (0, 0, 0))],
        out_specs=xspec,
        out_shape=jax.ShapeDtypeStruct(x.shape, F32),
        compiler_params=_cparams(("arbitrary", "arbitrary")),
        name="combine",
    )(x, y_tok, route, gate, g_final.reshape(1, 1, d))


def _layer_params(l, w):
    d = w["w_in"].shape[1]
    nh = w["w_q"].shape[1]
    da = w["ln_v_g"].shape[1]
    db = w["w_conv"].shape[2]
    n_main = 2 * da + 2 * db
    w_in = w["w_in"][l]
    w_gate = jnp.zeros((d, 2 * LANES), BF16)
    w_gate = w_gate.at[:, :nh].set(w_in[:, n_main:n_main + nh].astype(BF16))
    w_gate = w_gate.at[:, LANES:LANES + nh].set(w_in[:, n_main + nh:n_main + 2 * nh].astype(BF16))
    n_groups = w["w_r1"].shape[2]
    per_group = w["w_r2"].shape[3]
    w_r = jnp.concatenate([w["w_r1"][l], jnp.transpose(w["w_r2"][l], (1, 0, 2)).reshape(d, n_groups * per_group)],
                          axis=1)
    n_r = w_r.shape[1]
    w_r = jnp.pad(w_r, ((0, 0), (0, LANES - n_r)))
    w_r_hi = w_r.astype(BF16)
    w_r_lo = (w_r - w_r_hi.astype(F32)).astype(BF16)
    b_r = jnp.pad(jnp.concatenate([w["b_r1"][l], w["b_r2"][l].reshape(-1)]), (0, LANES - n_r)).reshape(1, LANES)
    return dict(
        g_norm1=w["g_norm1"][l], g_norm2=w["g_norm2"][l],
        w_main=w_in.astype(BF16), n_main=n_main, w_gate=w_gate,
        ln_g=w["ln_v_g"][l], ln_b=w["ln_v_b"][l],
        w_conv=w["w_conv"][l], b_conv=w["b_conv"][l].reshape(1, db),
        w_q=w["w_q"][l].astype(BF16), w_k=w["w_k"][l].astype(BF16),
        w_kt=jnp.swapaxes(w["w_k"][l], 1, 2).astype(BF16), w_v=w["w_v"][l].astype(BF16),
        b_i=jnp.pad(w["b_i"][l], (0, LANES - nh)).reshape(1, LANES),
        b_f=jnp.pad(w["b_f"][l], (0, LANES - nh)).reshape(1, LANES),
        g_mh=w["g_mh"][l].reshape(1, db), k_scale=1.0 / math.sqrt(LANES),
        w_out=w["w_out"][l].astype(BF16),
        w_router=jnp.stack([w_r_hi, w_r_lo]), b_router=b_r,
        n_groups=n_groups, per_group=per_group,
    )


def _mix_weights(w_spatial, b_spatial, L, da):
    nh = w_spatial.shape[0]
    causal = jnp.tril(jnp.ones((L, L), dtype=bool))
    w = jnp.where(causal[None], w_spatial[:, :L, :L], 0.0)
    reps = TILE // L
    if reps > 1:
        eye = jnp.eye(reps, dtype=F32)
        w = jnp.einsum("ab,hts->hatbs", eye, w).reshape(nh, TILE, TILE)
    bias = jnp.tile(jnp.transpose(b_spatial[:, :L]), (reps, 1))
    bias_full = jnp.repeat(bias, da // nh, axis=1)
    return w.astype(BF16), bias_full


def kernel(x_prompt, x_sample, c_prompt, c_sample, state_mlstm_C, state_mlstm_n, state_mlstm_m, state_conv, w_ada, b_ada, g_norm1, g_norm2, w_in, ln_v_g, ln_v_b, w_spatial, b_spatial, w_conv, b_conv, w_q, w_k, w_v, b_i, b_f, g_mh, w_out, w_r1, b_r1, w_r2, b_r2, w_e_gate, w_e_up, w_e_down, g_final):
    weights = dict(w_in=w_in, ln_v_g=ln_v_g, ln_v_b=ln_v_b, w_conv=w_conv, b_conv=b_conv, w_q=w_q, w_k=w_k,
                   w_v=w_v, b_i=b_i, b_f=b_f, g_mh=g_mh, w_out=w_out, w_r1=w_r1, b_r1=b_r1, w_r2=w_r2, b_r2=b_r2,
                   g_norm1=g_norm1, g_norm2=g_norm2)
    depth = w_in.shape[0]
    bp, tp, d = x_prompt.shape
    bs, ts, _ = x_sample.shape
    nh = w_q.shape[1]
    da = ln_v_g.shape[1]
    db = w_conv.shape[2]
    n_mod = w_ada.shape[2] // d
    n_experts = w_e_gate.shape[1]
    n_p, n_s = bp * tp, bs * ts
    assert da == db and da % LANES == 0 and nh * LANES == db and tp % TILE == 0 and TILE % ts == 0
    assert ts >= CONV_W - 1 and ts % SUBLANES == 0 and n_s % TILE == 0
    xm_col, zo_col = 2 * da // db, 2 * da // db + 1

    mod = _ada(jnp.concatenate([c_prompt, c_sample], axis=0), w_ada, b_ada).reshape(depth, bp + bs, n_mod, 1, d)

    xp, xs = x_prompt, x_sample
    outs = {k: [] for k in ("cp", "np", "mp", "bp", "cs", "ns", "ms", "bs", "vs")}
    for l in range(depth):
        p = _layer_params(l, weights)
        mods_p = [mod[l, :bp, i] for i in range(n_mod)]
        mods_s = [mod[l, bp:, i] for i in range(n_mod)]

        proj_p, gates_p = _inproj(xp, mods_p[1], mods_p[0], p["g_norm1"], p["w_main"], p["n_main"], p["w_gate"], 1024)
        w_mix_p, bias_p = _mix_weights(w_spatial[l], b_spatial[l], min(tp, TILE), da)
        (ya_p,) = _grpa(proj_p, p["ln_g"], p["ln_b"], w_mix_p, bias_p, da, False)
        yb_p, c_p, n_pn, m_p = _mlstm_prompt(proj_p, gates_p, p, bp, tp, db, nh, xm_col, zo_col)
        outs["cp"].append(c_p)
        outs["np"].append(n_pn.reshape(bp, nh, LANES))
        outs["mp"].append(m_p[:, 0, :nh])
        outs["bp"].append(proj_p.reshape(bp, tp, -1)[:, tp - (CONV_W - 1):, 2 * da:2 * da + db])

        proj_s, gates_s = _inproj(xs, mods_s[1], mods_s[0], p["g_norm1"], p["w_main"], p["n_main"], p["w_gate"], 512)
        w_mix_s, bias_s = _mix_weights(w_spatial[l], b_spatial[l], min(ts, TILE), da)
        ya_s, v_s = _grpa(proj_s, p["ln_g"], p["ln_b"], w_mix_s, bias_s, da, True)
        prev_rows = jnp.pad(state_conv[l], ((0, 0), (ts - (CONV_W - 1), 0), (0, 0))).reshape(n_s, db)
        n0_rows = jnp.repeat(state_mlstm_n[l].reshape(bs, db), ts, axis=0)
        m0_rows = jnp.pad(jnp.repeat(state_mlstm_m[l], ts, axis=0), ((0, 0), (0, LANES - nh)))
        yb_s, c_s, n_rows_s, m_rows_s = _mlstm_sample(proj_s, gates_s, prev_rows, state_mlstm_C, l, n0_rows, m0_rows,
                                                      p, math.gcd(ts, TILE), db, nh, xm_col, zo_col)
        outs["cs"].append(c_s)
        outs["ns"].append(n_rows_s.reshape(bs, ts, nh, LANES)[:, ts - 1])
        outs["ms"].append(m_rows_s.reshape(bs, ts, LANES)[:, ts - 1, :nh])
        outs["bs"].append(proj_s.reshape(bs, ts, -1)[:, ts - (CONV_W - 1):, 2 * da:2 * da + db])
        outs["vs"].append(v_s.reshape(bs, ts, da))

        rargs = (p["w_router"], p["b_router"], p["n_groups"], p["per_group"])
        xp, h2_p, rt_p = _outproj(xp, ya_p, yb_p, p["w_out"], mods_p[2], mods_p[4], mods_p[3], p["g_norm2"], *rargs, 256)
        xs, h2_s, rt_s = _outproj(xs, ya_s, yb_s, p["w_out"], mods_s[2], mods_s[4], mods_s[3], p["g_norm2"], *rargs, 256)

        route = jnp.concatenate([rt_p, rt_s], axis=0)
        plan = _moe_plan(route[:, :TOP_K].astype(jnp.int32), n_experts)
        nck = d // LANES
        x_buf = _gather_rows(plan["row_src"], h2_p, h2_s, nck, BF16)
        h_buf = _expert_up(plan, x_buf, w_e_gate, w_e_up, l)
        y_buf = _expert_down(plan, h_buf, w_e_down, l)
        dest_t = jnp.transpose(plan["dest"].reshape(n_p + n_s, TOP_K)).reshape(-1)
        y_tok = _gather_rows(dest_t, y_buf, y_buf, 1, F32).reshape(TOP_K, n_p + n_s, d)
        last = l == depth - 1
        xp = _combine(xp, y_tok, 0, rt_p, mods_p[5], g_final, last, 512)
        xs = _combine(xs, y_tok, n_p, rt_s, mods_s[5], g_final, last, 256)

    st = jnp.stack
    return (xp, xs, st(outs["cp"]), st(outs["np"]), st(outs["mp"]), st(outs["bp"]),
            st(outs["cs"]), st(outs["ns"]), st(outs["ms"]), st(outs["bs"]), st(outs["vs"]))
```

```python
import functools
import math

import jax
import jax.numpy as jnp
from jax import lax
from jax.experimental import pallas as pl
from jax.experimental.pallas import tpu as pltpu

F32 = jnp.float32
BF16 = jnp.bfloat16
EPS = 1e-6
LANES = 128
SUBLANES = 8
TILE = 128
CONV_W = 4
TOP_K = 2
VMEM_LIMIT_BYTES = 56 * 1024 * 1024
MOE_BLOCK_ROWS = 256
GATHER_CHUNK = 512
CAST_CHUNK = 256
NEG_INF = float("-inf")


def _cparams(sem):
    return pltpu.CompilerParams(dimension_semantics=sem, vmem_limit_bytes=VMEM_LIMIT_BYTES)


def _split3(x):
    x1 = x.astype(BF16)
    r1 = x - x1.astype(F32)
    x2 = r1.astype(BF16)
    r2 = r1 - x2.astype(F32)
    return x1, x2, r2.astype(BF16)


def _ada_kernel(c_ref, w_ref, b_ref, o_ref):
    c = c_ref[...]
    a = c * jax.nn.sigmoid(c)
    o_ref[0] = jnp.dot(a.astype(BF16), w_ref[0].astype(BF16), preferred_element_type=F32) + b_ref[0]


def _ada(c_all, w_ada, b_ada):
    depth, d, n = w_ada.shape
    bc = c_all.shape[0]
    tn = min(1024, n)
    return pl.pallas_call(
        _ada_kernel,
        grid=(depth, n // tn),
        in_specs=[pl.BlockSpec((bc, d), lambda l, j: (0, 0)),
                  pl.BlockSpec((1, d, tn), lambda l, j: (l, 0, j)),
                  pl.BlockSpec((1, 1, tn), lambda l, j: (l, 0, j))],
        out_specs=pl.BlockSpec((1, bc, tn), lambda l, j: (l, 0, j)),
        out_shape=jax.ShapeDtypeStruct((depth, bc, n), F32),
        compiler_params=_cparams(("arbitrary", "arbitrary")),
        name="ada",
    )(c_all, w_ada, b_ada.reshape(depth, 1, n))


def _inproj_kernel(x_ref, sc_ref, sh_ref, g_ref, w_ref, wg_ref, o_ref, og_ref, hn_ref):
    @pl.when(pl.program_id(2) == 0)
    def _():
        x = x_ref[...]
        ms = jnp.mean(x * x, axis=-1, keepdims=True)
        h = x * lax.rsqrt(ms + EPS) * g_ref[...]
        h = h * (1.0 + sc_ref[...]) + sh_ref[...]
        hb = h.reshape(hn_ref.shape).astype(BF16)
        hn_ref[...] = hb
        og_ref[...] = jnp.dot(hb, wg_ref[...], preferred_element_type=F32)

    o_ref[...] = jnp.dot(hn_ref[...], w_ref[...], preferred_element_type=F32)


def _row_tiling(b, t, max_rows):
    if t >= max_rows:
        assert t % max_rows == 0
        return 1, max_rows
    bb = min(b, max_rows // t)
    assert b % bb == 0 and t % SUBLANES == 0
    return bb, t


def _inproj(x, scale, shift, g_norm, w_main, n, w_gate, max_rows):
    b, t, d = x.shape
    ng = w_gate.shape[1]
    bb, tt = _row_tiling(b, t, max_rows)
    rows = bb * tt
    tn = min(512, n)
    nt = t // tt
    return pl.pallas_call(
        _inproj_kernel,
        grid=(b // bb, nt, n // tn),
        in_specs=[pl.BlockSpec((bb, tt, d), lambda i, k, j: (i, k, 0)),
                  pl.BlockSpec((bb, 1, d), lambda i, k, j: (i, 0, 0)),
                  pl.BlockSpec((bb, 1, d), lambda i, k, j: (i, 0, 0)),
                  pl.BlockSpec((1, 1, d), lambda i, k, j: (0, 0, 0)),
                  pl.BlockSpec((d, tn), lambda i, k, j: (0, j)),
                  pl.BlockSpec((d, ng), lambda i, k, j: (0, 0))],
        out_specs=[pl.BlockSpec((rows, tn), lambda i, k, j: (i * nt + k, j)),
                   pl.BlockSpec((rows, ng), lambda i, k, j: (i * nt + k, 0))],
        out_shape=[jax.ShapeDtypeStruct((b * t, n), F32), jax.ShapeDtypeStruct((b * t, ng), F32)],
        scratch_shapes=[pltpu.VMEM((rows, d), BF16)],
        compiler_params=_cparams(("arbitrary", "arbitrary", "arbitrary")),
        name="inproj",
    )(x, scale, shift, g_norm.reshape(1, 1, d), w_main, w_gate)


def _grpa_kernel(u_ref, v_ref, lg_ref, lb_ref, w_ref, bias_ref, ya_ref, *maybe_v_out):
    rows, da = u_ref.shape
    nh = w_ref.shape[0]
    g = jax.nn.gelu(v_ref[...])
    mu = jnp.mean(g, axis=-1, keepdims=True)
    gc = g - mu
    var = jnp.mean(gc * gc, axis=-1, keepdims=True)
    v = gc * lax.rsqrt(var + EPS) * lg_ref[...] + lb_ref[...]
    if maybe_v_out:
        maybe_v_out[0][...] = v
    vb = v.astype(BF16)
    for c in range(rows // TILE):
        r0 = c * TILE
        for h in range(nh):
            c0 = h * LANES
            z = jnp.dot(w_ref[h], vb[r0:r0 + TILE, c0:c0 + LANES], preferred_element_type=F32)
            u = jax.nn.gelu(u_ref[r0:r0 + TILE, c0:c0 + LANES])
            y = u * (z + bias_ref[:, c0:c0 + LANES])
            ya_ref[r0:r0 + TILE, c0:c0 + LANES] = y.astype(BF16)


def _grpa(proj, ln_g, ln_b, w_mix, bias_full, da, want_v):
    n_rows = proj.shape[0]
    rows = min(512, n_rows)
    nh = w_mix.shape[0]
    out_specs = [pl.BlockSpec((rows, da), lambda i: (i, 0))]
    out_shape = [jax.ShapeDtypeStruct((n_rows, da), BF16)]
    if want_v:
        out_specs.append(pl.BlockSpec((rows, da), lambda i: (i, 0)))
        out_shape.append(jax.ShapeDtypeStruct((n_rows, da), F32))
    return pl.pallas_call(
        _grpa_kernel,
        grid=(n_rows // rows,),
        in_specs=[pl.BlockSpec((rows, da), lambda i: (i, 0)),
                  pl.BlockSpec((rows, da), lambda i: (i, 1)),
                  pl.BlockSpec((1, da), lambda i: (0, 0)),
                  pl.BlockSpec((1, da), lambda i: (0, 0)),
                  pl.BlockSpec((nh, TILE, TILE), lambda i: (0, 0, 0)),
                  pl.BlockSpec((TILE, da), lambda i: (0, 0))],
        out_specs=out_specs,
        out_shape=out_shape,
        compiler_params=_cparams(("arbitrary",)),
        name="grpa",
    )(proj, proj, ln_g.reshape(1, da), ln_b.reshape(1, da), w_mix, bias_full)


def _group_last(x, L):
    if L == TILE:
        return jnp.broadcast_to(x[TILE - 1:TILE, :], x.shape)
    g = TILE // L
    x3 = x.reshape(g, L, x.shape[1])
    return jnp.broadcast_to(x3[:, L - 1:L, :], x3.shape).reshape(x.shape)


def _group_sum(x, L):
    if L == TILE:
        return jnp.broadcast_to(jnp.sum(x, axis=0, keepdims=True), x.shape)
    g = TILE // L
    x3 = x.reshape(g, L, x.shape[1])
    return jnp.broadcast_to(jnp.sum(x3, axis=1, keepdims=True), x3.shape).reshape(x.shape)


def _mlstm_kernel(*refs, L, carry, nh, k_scale):
    if carry:
        (xm_ref, zo_ref, g_ref, wconv_ref, bconv_ref, wq_ref, wk_ref, wkt_ref, wv_ref, bi_ref, bf_ref, gmh_ref,
         yb_ref, c_ref, n_ref, m_ref, prev_scr) = refs
        c_in_ref = c_ref
        first = pl.program_id(1) == 0

        @pl.when(first)
        def _():
            c_ref[...] = jnp.zeros(c_ref.shape, F32)
            n_ref[...] = jnp.zeros(n_ref.shape, F32)
            m_ref[...] = jnp.zeros(m_ref.shape, F32)
            prev_scr[...] = jnp.zeros(prev_scr.shape, F32)

        prev = prev_scr[...]
        n0_rows = jnp.broadcast_to(n_ref[0], (TILE, n_ref.shape[2]))
        m0 = jnp.broadcast_to(m_ref[0], (TILE, LANES))
    else:
        (xm_ref, zo_ref, g_ref, prev_ref, c_layer_ref, n0_ref, m0_ref,
         wconv_ref, bconv_ref, wq_ref, wk_ref, wkt_ref, wv_ref, bi_ref, bf_ref, gmh_ref,
         yb_ref, c_ref, n_ref, m_ref) = refs
        c_in_ref = c_layer_ref.at[0]
        prev = prev_ref[...]
        n0_rows = n0_ref[...]
        m0 = m0_ref[...]

    groups = TILE // L
    x = xm_ref[...]
    row_in_group = lax.broadcasted_iota(jnp.int32, (TILE, 1), 0) % L

    acc = bconv_ref[...] + wconv_ref[CONV_W - 1:CONV_W, :] * x
    for j in range(1, CONV_W):
        from_prev = pltpu.roll(prev, (j - L) % TILE, 0)
        shifted = jnp.where(row_in_group < j, from_prev, pltpu.roll(x, j, 0))
        acc = acc + wconv_ref[CONV_W - 1 - j:CONV_W - j, :] * shifted
    xc = acc * jax.nn.sigmoid(acc)
    if carry:
        prev_scr[...] = x

    gates = g_ref[...]
    gi = gates[:, :LANES] + bi_ref[...]
    gf = gates[:, LANES:] + bf_ref[...]
    lf = jnp.minimum(gf, 0.0) - jnp.log1p(jnp.exp(-jnp.abs(gf)))

    r_i = lax.broadcasted_iota(jnp.int32, (TILE, TILE), 0)
    c_i = lax.broadcasted_iota(jnp.int32, (TILE, TILE), 1)
    mask = (c_i <= r_i) & ((c_i // L) == (r_i // L)) if L != TILE else (c_i <= r_i)
    mask_b = jnp.where(mask, 1.0, 0.0).astype(BF16)
    lf1, lf2, lf3 = _split3(lf)
    cum = (jnp.dot(mask_b, lf1, preferred_element_type=F32) + jnp.dot(mask_b, lf2, preferred_element_type=F32)
           + jnp.dot(mask_b, lf3, preferred_element_type=F32))
    cum_t = cum.T
    gi_t = gi.T
    log_inter = cum + m0
    lane = lax.broadcasted_iota(jnp.int32, (TILE, LANES), 1)

    def d_matrix(h):
        d = cum[:, h:h + 1] - cum_t[h:h + 1, :] + gi_t[h:h + 1, :]
        return jnp.where(mask, d, NEG_INF)

    mt = jnp.zeros((TILE, LANES), F32)
    for h in range(nh):
        mt_h = jnp.maximum(log_inter[:, h:h + 1], jnp.max(d_matrix(h), axis=1, keepdims=True))
        mt = jnp.where(lane == h, mt_h, mt)

    m_last = _group_last(mt, L)
    w_inter = jnp.exp(log_inter - mt)
    w_last = jnp.exp(_group_last(cum, L) - cum + gi - m_last)
    w_last_t = w_last.T
    inv_floor = jnp.exp(-mt)

    for h in range(nh):
        c0 = h * LANES
        xc_h = xc[:, c0:c0 + LANES].astype(BF16)
        q = jnp.dot(xc_h, wq_ref[h], preferred_element_type=F32)
        k = jnp.dot(xc_h, wk_ref[h], preferred_element_type=F32) * k_scale
        k_t = lax.dot_general(wkt_ref[h], xc_h, (((1,), (1,)), ((), ())), preferred_element_type=F32) * k_scale
        v = jnp.dot(x[:, c0:c0 + LANES].astype(BF16), wv_ref[h], preferred_element_type=F32)
        qb = q.astype(BF16)
        vb = v.astype(BF16)
        mt_h = mt[:, h:h + 1]
        wi_h = w_inter[:, h:h + 1]
        w_intra = jnp.exp(d_matrix(h) - mt_h)
        s = jnp.dot(qb, k_t.astype(BF16), preferred_element_type=F32) * w_intra

        if carry:
            qc = jnp.dot(qb, c_in_ref[0, h].astype(BF16), preferred_element_type=F32)
        else:
            qc = jnp.concatenate(
                [jnp.dot(q[g * L:(g + 1) * L, :].astype(BF16), c_in_ref[g, h].astype(BF16),
                         preferred_element_type=F32)
                 for g in range(groups)], axis=0)

        num = wi_h * qc + jnp.dot(s.astype(BF16), vb, preferred_element_type=F32)
        n0_b = n0_rows[:, c0:c0 + LANES].astype(BF16).astype(F32)
        qn = jnp.sum(qb.astype(F32) * n0_b, axis=1, keepdims=True)
        den = wi_h * qn + jnp.sum(s, axis=1, keepdims=True)
        hh = num / jnp.maximum(jnp.abs(den), inv_floor[:, h:h + 1])
        hh = hh * lax.rsqrt(jnp.mean(hh * hh, axis=1, keepdims=True) + EPS) * gmh_ref[:, c0:c0 + LANES]
        yb_ref[:, c0:c0 + LANES] = (jax.nn.sigmoid(zo_ref[:, c0:c0 + LANES]) * hh).astype(BF16)

        kw_t = k_t * w_last_t[h:h + 1, :]
        wl_b = w_last[:, h:h + 1].astype(BF16).astype(F32)
        n_rows = wi_h * n0_rows[:, c0:c0 + LANES] + _group_sum(wl_b * k.astype(BF16).astype(F32), L)
        if carry:
            decay = jnp.broadcast_to(w_inter[TILE - 1:TILE, h:h + 1], (LANES, LANES))
            c_ref[0, h] = decay * c_in_ref[0, h] + jnp.dot(kw_t.astype(BF16), vb, preferred_element_type=F32)
            n_ref[0, :, c0:c0 + LANES] = n_rows[TILE - 1:TILE, :]
        else:
            n_ref[:, c0:c0 + LANES] = n_rows
            for g in range(groups):
                kw_g = jnp.where((c_i // L) == g, kw_t, 0.0).astype(BF16)
                upd = jnp.dot(kw_g, vb, preferred_element_type=F32)
                last = g * L + L - 1
                decay = jnp.broadcast_to(w_inter[last:last + 1, h:h + 1], (LANES, LANES))
                c_ref[g, h] = decay * c_in_ref[g, h] + upd

    if carry:
        m_ref[0] = mt[TILE - 1:TILE, :]
    else:
        m_ref[...] = mt


def _mlstm_weights(p):
    return [p["w_conv"], p["b_conv"], p["w_q"], p["w_k"], p["w_kt"], p["w_v"], p["b_i"], p["b_f"], p["g_mh"]]


def _mlstm_weight_specs(nh, db, nidx):
    z2 = (lambda i, c: (0, 0)) if nidx == 2 else (lambda i: (0, 0))
    z3 = (lambda i, c: (0, 0, 0)) if nidx == 2 else (lambda i: (0, 0, 0))
    hw = pl.BlockSpec((nh, LANES, LANES), z3)
    return [pl.BlockSpec((CONV_W, db), z2), pl.BlockSpec((1, db), z2), hw, hw, hw, hw,
            pl.BlockSpec((1, LANES), z2), pl.BlockSpec((1, LANES), z2), pl.BlockSpec((1, db), z2)]


def _mlstm_prompt(proj, gates, p, b, t, db, nh, xm_col, zo_col):
    nt = t // TILE
    kern = functools.partial(_mlstm_kernel, L=TILE, carry=True, nh=nh, k_scale=p["k_scale"])
    return pl.pallas_call(
        kern,
        grid=(b, nt),
        in_specs=[pl.BlockSpec((TILE, db), lambda i, c: (i * nt + c, xm_col)),
                  pl.BlockSpec((TILE, db), lambda i, c: (i * nt + c, zo_col)),
                  pl.BlockSpec((TILE, 2 * LANES), lambda i, c: (i * nt + c, 0))] + _mlstm_weight_specs(nh, db, 2),
        out_specs=[pl.BlockSpec((TILE, db), lambda i, c: (i * nt + c, 0)),
                   pl.BlockSpec((1, nh, LANES, LANES), lambda i, c: (i, 0, 0, 0)),
                   pl.BlockSpec((1, 1, db), lambda i, c: (i, 0, 0)),
                   pl.BlockSpec((1, 1, LANES), lambda i, c: (i, 0, 0))],
        out_shape=[jax.ShapeDtypeStruct((b * t, db), BF16),
                   jax.ShapeDtypeStruct((b, nh, LANES, LANES), F32),
                   jax.ShapeDtypeStruct((b, 1, db), F32),
                   jax.ShapeDtypeStruct((b, 1, LANES), F32)],
        scratch_shapes=[pltpu.VMEM((TILE, db), F32)],
        compiler_params=_cparams(("arbitrary", "arbitrary")),
        name="mlstm_prompt",
    )(proj, proj, gates, *_mlstm_weights(p))


def _mlstm_sample(proj, gates, prev_rows, c0, layer, n0_rows, m0_rows, p, L, db, nh, xm_col, zo_col):
    n_rows = proj.shape[0]
    groups = TILE // L
    kern = functools.partial(_mlstm_kernel, L=L, carry=False, nh=nh, k_scale=p["k_scale"])
    row_spec = pl.BlockSpec((TILE, db), lambda i: (i, 0))
    return pl.pallas_call(
        kern,
        grid=(n_rows // TILE,),
        in_specs=[pl.BlockSpec((TILE, db), lambda i: (i, xm_col)),
                  pl.BlockSpec((TILE, db), lambda i: (i, zo_col)),
                  pl.BlockSpec((TILE, 2 * LANES), lambda i: (i, 0)),
                  row_spec,
                  pl.BlockSpec((1, groups, nh, LANES, LANES), lambda i: (layer, i, 0, 0, 0)),
                  row_spec,
                  pl.BlockSpec((TILE, LANES), lambda i: (i, 0))] + _mlstm_weight_specs(nh, db, 1),
        out_specs=[row_spec,
                   pl.BlockSpec((groups, nh, LANES, LANES), lambda i: (i, 0, 0, 0)),
                   row_spec,
                   pl.BlockSpec((TILE, LANES), lambda i: (i, 0))],
        out_shape=[jax.ShapeDtypeStruct((n_rows, db), BF16),
                   jax.ShapeDtypeStruct(c0.shape[1:], F32),
                   jax.ShapeDtypeStruct((n_rows, db), F32),
                   jax.ShapeDtypeStruct((n_rows, LANES), F32)],
        compiler_params=_cparams(("arbitrary",)),
        name="mlstm_sample",
    )(proj, proj, gates, prev_rows, c0, n0_rows, m0_rows, *_mlstm_weights(p))


def _route(lg, n_groups, per_group):
    lane_i = lax.broadcasted_iota(jnp.int32, lg.shape, 1)
    lane = lane_i.astype(F32)
    far = float(LANES)
    is_g = lane_i < n_groups
    lgg = jnp.where(is_g, lg, NEG_INF)
    mg = jnp.max(lgg, axis=1, keepdims=True)
    g_sel = jnp.min(jnp.where(lgg == mg, lane, far), axis=1, keepdims=True)
    p_g = 1.0 / jnp.sum(jnp.where(is_g, jnp.exp(lgg - mg), 0.0), axis=1, keepdims=True)
    lo = n_groups + g_sel * per_group
    sel = (lane >= lo) & (lane < lo + per_group)
    le = jnp.where(sel, lg, NEG_INF)
    m1 = jnp.max(le, axis=1, keepdims=True)
    i1 = jnp.min(jnp.where(le == m1, lane, far), axis=1, keepdims=True)
    le2 = jnp.where(lane == i1, NEG_INF, le)
    m2 = jnp.max(le2, axis=1, keepdims=True)
    i2 = jnp.min(jnp.where(le2 == m2, lane, far), axis=1, keepdims=True)
    ratio = jnp.exp(m2 - m1)
    gate1 = p_g / (1.0 + ratio)
    gate2 = p_g * ratio / (1.0 + ratio)
    out = jnp.where(lane_i == 0, i1 - n_groups, 0.0)
    out = jnp.where(lane_i == 1, i2 - n_groups, out)
    out = jnp.where(lane_i == 2, gate1, out)
    return jnp.where(lane_i == 3, gate2, out)


def _outproj_kernel(x_ref, ya_ref, yb_ref, w_ref, g1_ref, sc_ref, sh_ref, gn_ref, wr_ref, br_ref,
                    xn_ref, h2_ref, rt_ref, *, n_groups, per_group):
    da = ya_ref.shape[1]
    mix = (jnp.dot(ya_ref[...], w_ref[:da, :], preferred_element_type=F32)
           + jnp.dot(yb_ref[...], w_ref[da:, :], preferred_element_type=F32))
    x = x_ref[...] + g1_ref[...] * mix.reshape(x_ref.shape)
    xn_ref[...] = x
    ms = jnp.mean(x * x, axis=-1, keepdims=True)
    h = x * lax.rsqrt(ms + EPS) * gn_ref[...]
    h = (h * (1.0 + sc_ref[...]) + sh_ref[...]).reshape(h2_ref.shape)
    h2_ref[...] = h
    lg = jnp.dot(h.astype(BF16), wr_ref[...], preferred_element_type=F32) + br_ref[...]
    rt_ref[...] = _route(lg, n_groups, per_group)


def _outproj(x, ya, yb, w_out, gate, scale, shift, g_norm, w_router, b_router, n_groups, per_group, max_rows):
    b, t, d = x.shape
    bb, tt = _row_tiling(b, t, max_rows)
    rows = bb * tt
    nt = t // tt
    da = ya.shape[1]
    xspec = pl.BlockSpec((bb, tt, d), lambda i, k: (i, k, 0))
    mspec = pl.BlockSpec((bb, 1, d), lambda i, k: (i, 0, 0))
    kern = functools.partial(_outproj_kernel, n_groups=n_groups, per_group=per_group)
    return pl.pallas_call(
        kern,
        grid=(b // bb, nt),
        in_specs=[xspec,
                  pl.BlockSpec((rows, da), lambda i, k: (i * nt + k, 0)),
                  pl.BlockSpec((rows, ya.shape[1]), lambda i, k: (i * nt + k, 0)),
                  pl.BlockSpec(w_out.shape, lambda i, k: (0, 0), pipeline_mode=pl.Buffered(1)),
                  mspec, mspec, mspec,
                  pl.BlockSpec((1, 1, d), lambda i, k: (0, 0, 0)),
                  pl.BlockSpec(w_router.shape, lambda i, k: (0, 0), pipeline_mode=pl.Buffered(1)),
                  pl.BlockSpec((1, LANES), lambda i, k: (0, 0))],
        out_specs=[xspec,
                   pl.BlockSpec((rows, d), lambda i, k: (i * nt + k, 0)),
                   pl.BlockSpec((rows, LANES), lambda i, k: (i * nt + k, 0))],
        out_shape=[jax.ShapeDtypeStruct(x.shape, F32),
                   jax.ShapeDtypeStruct((b * t, d), F32),
                   jax.ShapeDtypeStruct((b * t, LANES), F32)],
        compiler_params=_cparams(("arbitrary", "arbitrary")),
        name="outproj",
    )(x, ya, yb, w_out, gate, scale, shift, g_norm.reshape(1, 1, d), w_router, b_router)


def _gather_kernel(idx_ref, used_ref, src_a_ref, src_b_ref, out_ref, buf, sem, *, n_a, chunk):
    i = pl.program_id(0)

    def in_use(step):
        return step * chunk < used_ref[0]

    def issue_step(step, slot):
        def issue(r, _):
            t = idx_ref[step * chunk + r]
            dst = buf.at[slot, pl.ds(r, 1)]

            @pl.when(t < n_a)
            def _():
                pltpu.make_async_copy(src_a_ref.at[pl.ds(t, 1)], dst, sem.at[slot]).start()

            @pl.when(t >= n_a)
            def _():
                pltpu.make_async_copy(src_b_ref.at[pl.ds(t - n_a, 1)], dst, sem.at[slot]).start()

            return 0

        lax.fori_loop(0, chunk, issue, 0, unroll=8)

    @pl.when(i == 0)
    def _():
        issue_step(0, 0)

    @pl.when((i + 1 < pl.num_programs(0)) & in_use(i + 1))
    def _():
        issue_step(i + 1, (i + 1) % 2)

    @pl.when(in_use(i))
    def _():
        slot = i % 2
        pltpu.make_async_copy(buf.at[slot], buf.at[slot], sem.at[slot]).wait()
        out_ref[...] = buf[slot].astype(out_ref.dtype)

    @pl.when(jnp.logical_not(in_use(i)))
    def _():
        out_ref[...] = jnp.zeros(out_ref.shape, out_ref.dtype)


def _gather_rows(idx, n_used, src_a, src_b, out_dtype):
    n_out = idx.shape[0]
    chunk = math.gcd(GATHER_CHUNK, n_out)
    d = src_a.shape[1]
    kern = functools.partial(_gather_kernel, n_a=src_a.shape[0], chunk=chunk)
    return pl.pallas_call(
        kern,
        grid_spec=pltpu.PrefetchScalarGridSpec(
            num_scalar_prefetch=2,
            grid=(n_out // chunk,),
            in_specs=[pl.BlockSpec(memory_space=pl.ANY), pl.BlockSpec(memory_space=pl.ANY)],
            out_specs=pl.BlockSpec((chunk, d), lambda i, idx, used: (i, 0)),
            scratch_shapes=[pltpu.VMEM((2, chunk, d), src_a.dtype), pltpu.SemaphoreType.DMA((2,))]),
        out_shape=jax.ShapeDtypeStruct((n_out, d), out_dtype),
        compiler_params=_cparams(("arbitrary",)),
        name="gather_rows",
    )(idx, n_used, src_a, src_b)


def _expert_up_kernel(ie_ref, ij_ref, ib_ref, ioj_ref, inew_ref, islot_ref, ine_ref, inj_ref, ihas_ref, n_ref,
                      x_ref, wg_hbm, wu_hbm, o_ref, stage, wgb_scr, wub_scr, sem, *, layer):
    i = pl.program_id(0)
    tn = wgb_scr.shape[1]

    def weight_copies(e, j, slot):
        col = pl.multiple_of(j * tn, tn)
        return (pltpu.make_async_copy(wg_hbm.at[layer, e, :, pl.ds(col, tn)], stage.at[slot, 0], sem.at[slot, 0]),
                pltpu.make_async_copy(wu_hbm.at[layer, e, :, pl.ds(col, tn)], stage.at[slot, 1], sem.at[slot, 1]))

    @pl.when(i == 0)
    def _():
        for cp in weight_copies(ie_ref[0], ij_ref[0], islot_ref[0]):
            cp.start()

    @pl.when(i >= n_ref[0])
    def _():
        o_ref[...] = jnp.zeros(o_ref.shape, o_ref.dtype)

    @pl.when(i < n_ref[0])
    def _():
        @pl.when(inew_ref[i] == 1)
        def _():
            slot = islot_ref[i]

            @pl.when(ihas_ref[i] == 1)
            def _():
                for cp in weight_copies(ine_ref[i], inj_ref[i], 1 - slot):
                    cp.start()

            for cp in weight_copies(ie_ref[i], ij_ref[i], slot):
                cp.wait()
            a = u = None
            for k0 in range(0, x_ref.shape[1], CAST_CHUNK):
                ks = slice(k0, k0 + CAST_CHUNK)
                wg_k = stage[slot, 0, ks, :].astype(BF16)
                wu_k = stage[slot, 1, ks, :].astype(BF16)
                wgb_scr[ks, :] = wg_k
                wub_scr[ks, :] = wu_k
                a_k = jnp.dot(x_ref[:, ks], wg_k, preferred_element_type=F32)
                u_k = jnp.dot(x_ref[:, ks], wu_k, preferred_element_type=F32)
                a = a_k if a is None else a + a_k
                u = u_k if u is None else u + u_k
            o_ref[...] = (a * jax.nn.sigmoid(a) * u).astype(BF16)

        @pl.when(inew_ref[i] != 1)
        def _():
            xb = x_ref[...]
            a = jnp.dot(xb, wgb_scr[...], preferred_element_type=F32)
            u = jnp.dot(xb, wub_scr[...], preferred_element_type=F32)
            o_ref[...] = (a * jax.nn.sigmoid(a) * u).astype(BF16)


def _expert_up(plan, x_buf, w_gate, w_up, layer):
    d, de = w_gate.shape[2], w_gate.shape[3]
    n_rows = x_buf.shape[0]
    tn = de // 2
    bm = MOE_BLOCK_ROWS
    up = plan["up"]
    n_items = up["e"].shape[0]
    names = ("e", "j", "b", "oj", "new", "slot", "next_e", "next_j", "has_next", "n")
    return pl.pallas_call(
        functools.partial(_expert_up_kernel, layer=layer),
        grid_spec=pltpu.PrefetchScalarGridSpec(
            num_scalar_prefetch=len(names),
            grid=(n_items,),
            in_specs=[pl.BlockSpec((bm, d), lambda i, *s: (s[2][i], 0)),
                      pl.BlockSpec(memory_space=pl.ANY), pl.BlockSpec(memory_space=pl.ANY)],
            out_specs=pl.BlockSpec((bm, tn), lambda i, *s: (s[2][i], s[3][i])),
            scratch_shapes=[pltpu.VMEM((2, 2, d, tn), F32), pltpu.VMEM((d, tn), BF16), pltpu.VMEM((d, tn), BF16),
                            pltpu.SemaphoreType.DMA((2, 2))]),
        out_shape=jax.ShapeDtypeStruct((n_rows, de), BF16),
        compiler_params=_cparams(("arbitrary",)),
        name="expert_up",
    )(*[up[k] for k in names], x_buf, w_gate, w_up)


def _expert_down_kernel(be_ref, bnew_ref, bslot_ref, bne_ref, bhas_ref, n_ref, h_ref, wd_hbm, o_ref,
                        stage, wdb_scr, sem, *, layer):
    i = pl.program_id(0)

    def weight_copy(e, slot):
        return pltpu.make_async_copy(wd_hbm.at[layer, e], stage.at[slot], sem.at[slot])

    @pl.when(i == 0)
    def _():
        weight_copy(be_ref[0], bslot_ref[0]).start()

    @pl.when(i >= n_ref[0])
    def _():
        o_ref[...] = jnp.zeros(o_ref.shape, o_ref.dtype)

    @pl.when(i < n_ref[0])
    def _():
        @pl.when(bnew_ref[i] == 1)
        def _():
            slot = bslot_ref[i]

            @pl.when(bhas_ref[i] == 1)
            def _():
                weight_copy(bne_ref[i], 1 - slot).start()

            weight_copy(be_ref[i], slot).wait()
            y = None
            for k0 in range(0, h_ref.shape[1], CAST_CHUNK):
                ks = slice(k0, k0 + CAST_CHUNK)
                wd_k = stage[slot, ks, :].astype(BF16)
                wdb_scr[ks, :] = wd_k
                y_k = jnp.dot(h_ref[:, ks], wd_k, preferred_element_type=F32)
                y = y_k if y is None else y + y_k
            o_ref[...] = y

        @pl.when(bnew_ref[i] != 1)
        def _():
            o_ref[...] = jnp.dot(h_ref[...], wdb_scr[...], preferred_element_type=F32)


def _expert_down(plan, h_buf, w_down, layer):
    n_rows, de = h_buf.shape
    d = w_down.shape[3]
    bm = MOE_BLOCK_ROWS
    down = plan["down"]
    n_blocks = down["e"].shape[0]
    names = ("e", "new", "slot", "next_e", "has_next", "n")
    return pl.pallas_call(
        functools.partial(_expert_down_kernel, layer=layer),
        grid_spec=pltpu.PrefetchScalarGridSpec(
            num_scalar_prefetch=len(names),
            grid=(n_blocks,),
            in_specs=[pl.BlockSpec((bm, de), lambda i, *s: (i, 0)),
                      pl.BlockSpec(memory_space=pl.ANY)],
            out_specs=pl.BlockSpec((bm, d), lambda i, *s: (i, 0)),
            scratch_shapes=[pltpu.VMEM((2, de, d), F32), pltpu.VMEM((de, d), BF16), pltpu.SemaphoreType.DMA((2,))]),
        out_shape=jax.ShapeDtypeStruct((n_rows, d), F32),
        compiler_params=_cparams(("arbitrary",)),
        name="expert_down",
    )(*[down[k] for k in names], h_buf, w_down)


def _moe_plan(expert, n_experts):
    bm = MOE_BLOCK_ROWS
    n_tok = expert.shape[0]
    n_asg = n_tok * TOP_K
    e_flat = expert.reshape(-1)
    onehot = (e_flat[:, None] == jnp.arange(n_experts, dtype=jnp.int32)[None, :]).astype(jnp.int32)
    csum = jnp.cumsum(onehot, axis=0)
    counts = csum[-1]
    rank = jnp.take_along_axis(csum, e_flat[:, None], axis=1)[:, 0] - 1
    nblk = (counts + bm - 1) // bm
    bend = jnp.cumsum(nblk)
    bstart = bend - nblk
    dest = bstart[e_flat] * bm + rank
    n_blocks = (n_asg + n_experts * (bm - 1) + bm - 1) // bm
    n_rows = -(-n_blocks * bm // GATHER_CHUNK) * GATHER_CHUNK
    tok = jnp.arange(n_asg, dtype=jnp.int32) // TOP_K
    row_src = (jnp.arange(n_rows, dtype=jnp.int32) % n_tok).at[dest].set(tok, unique_indices=True)
    used = bend[-1]

    def expert_of_block(b):
        return jnp.minimum(jnp.sum((bend[None, :] <= b[:, None]).astype(jnp.int32), axis=1), n_experts - 1)

    blk = jnp.arange(n_blocks, dtype=jnp.int32)
    blk_e = expert_of_block(jnp.minimum(blk, used - 1))
    blk_new = (blk == bstart[blk_e]).astype(jnp.int32)
    n_items = 2 * n_blocks
    idx = jnp.arange(n_items, dtype=jnp.int32)
    it = jnp.minimum(idx, 2 * used - 1)
    item_e = expert_of_block(it // 2)
    off = it - 2 * bstart[item_e]
    item_j = (off // nblk[item_e]).astype(jnp.int32)
    item_b = (bstart[item_e] + off % nblk[item_e]).astype(jnp.int32)
    item_new = (item_b == bstart[item_e]).astype(jnp.int32)
    spare = idx - 2 * used
    is_spare = spare >= 0
    item_b = jnp.where(is_spare, used + spare // 2, item_b)
    item_oj = jnp.where(is_spare, spare % 2, item_j)
    e_rank = jnp.cumsum((nblk > 0).astype(jnp.int32)) - 1
    e_after = expert_of_block(jnp.minimum(bend, used - 1))
    more_after = (bend < used).astype(jnp.int32)
    down = dict(e=blk_e, new=blk_new, slot=e_rank[blk_e] % 2, next_e=e_after[blk_e], has_next=more_after[blk_e],
                n=used.reshape(1).astype(jnp.int32))
    first_half = item_j == 0
    up = dict(e=item_e, j=item_j, b=item_b, oj=item_oj, new=item_new, slot=item_j,
              next_e=jnp.where(first_half, item_e, e_after[item_e]), next_j=jnp.where(first_half, 1, 0),
              has_next=jnp.where(first_half, 1, more_after[item_e]), n=(2 * used).reshape(1).astype(jnp.int32))
    return dict(dest=dest, row_src=row_src, up=up, down=down,
                used_rows=(used * bm).reshape(1).astype(jnp.int32))


def _combine_kernel(dst_ref, x_ref, rt_ref, g2_ref, gf_ref, y_hbm, o_ref, buf, sem, *, final_norm, tok0, n_all):
    bb, tt, _ = x_ref.shape
    rows = bb * tt
    nt = pl.num_programs(1)
    lin = pl.program_id(0) * nt + pl.program_id(1)

    def issue_step(step, slot):
        base = tok0 + step * rows

        def issue(r, _):
            for k in range(TOP_K):
                row = dst_ref[k * n_all + base + r]
                pltpu.make_async_copy(y_hbm.at[pl.ds(row, 1)], buf.at[slot, k, pl.ds(r, 1)], sem.at[slot]).start()
            return 0

        lax.fori_loop(0, rows, issue, 0, unroll=4)

    @pl.when(lin == 0)
    def _():
        issue_step(0, 0)

    @pl.when(lin + 1 < pl.num_programs(0) * nt)
    def _():
        issue_step(lin + 1, (lin + 1) % 2)

    slot = lin % 2
    pltpu.make_async_copy(buf.at[slot], buf.at[slot], sem.at[slot]).wait()
    rt = rt_ref[...]
    y = rt[:, 2:3] * buf[slot, 0] + rt[:, 3:4] * buf[slot, 1]
    x = x_ref[...] + g2_ref[...] * y.reshape(x_ref.shape)
    if final_norm:
        ms = jnp.mean(x * x, axis=-1, keepdims=True)
        x = x * lax.rsqrt(ms + EPS) * gf_ref[...]
    o_ref[...] = x


def _combine(x, y_buf, dest_t, tok0, route, gate, g_final, final_norm, max_rows):
    b, t, d = x.shape
    bb, tt = _row_tiling(b, t, max_rows)
    rows = bb * tt
    nt = t // tt
    xspec = pl.BlockSpec((bb, tt, d), lambda i, k, dst: (i, k, 0))
    kern = functools.partial(_combine_kernel, final_norm=final_norm, tok0=tok0, n_all=dest_t.shape[0] // TOP_K)
    return pl.pallas_call(
        kern,
        grid_spec=pltpu.PrefetchScalarGridSpec(
            num_scalar_prefetch=1,
            grid=(b // bb, nt),
            in_specs=[xspec,
                      pl.BlockSpec((rows, LANES), lambda i, k, dst: (i * nt + k, 0)),
                      pl.BlockSpec((bb, 1, d), lambda i, k, dst: (i, 0, 0)),
                      pl.BlockSpec((1, 1, d), lambda i, k, dst: (0, 0, 0)),
                      pl.BlockSpec(memory_space=pl.ANY)],
            out_specs=xspec,
            scratch_shapes=[pltpu.VMEM((2, TOP_K, rows, d), F32), pltpu.SemaphoreType.DMA((2,))]),
        out_shape=jax.ShapeDtypeStruct(x.shape, F32),
        compiler_params=_cparams(("arbitrary", "arbitrary")),
        name="combine",
    )(dest_t, x, route, gate, g_final.reshape(1, 1, d), y_buf)


def _layer_params(l, w):
    d = w["w_in"].shape[1]
    nh = w["w_q"].shape[1]
    da = w["ln_v_g"].shape[1]
    db = w["w_conv"].shape[2]
    n_main = 2 * da + 2 * db
    w_in = w["w_in"][l]
    w_gate = jnp.zeros((d, 2 * LANES), BF16)
    w_gate = w_gate.at[:, :nh].set(w_in[:, n_main:n_main + nh].astype(BF16))
    w_gate = w_gate.at[:, LANES:LANES + nh].set(w_in[:, n_main + nh:n_main + 2 * nh].astype(BF16))
    n_groups = w["w_r1"].shape[2]
    per_group = w["w_r2"].shape[3]
    w_r = jnp.concatenate([w["w_r1"][l], jnp.transpose(w["w_r2"][l], (1, 0, 2)).reshape(d, n_groups * per_group)],
                          axis=1)
    n_r = w_r.shape[1]
    w_r = jnp.pad(w_r, ((0, 0), (0, LANES - n_r)))
    b_r =jnp.pad(jnp.concatenate([w["b_r1"][l], w["b_r2"][l].reshape(-1)]), (0, LANES - n_r)).reshape(1, LANES)
    return dict(
        g_norm1=w["g_norm1"][l], g_norm2=w["g_norm2"][l],
        w_main=w_in.astype(BF16), n_main=n_main, w_gate=w_gate,
        ln_g=w["ln_v_g"][l], ln_b=w["ln_v_b"][l],
        w_conv=w["w_conv"][l], b_conv=w["b_conv"][l].reshape(1, db),
        w_q=w["w_q"][l].astype(BF16), w_k=w["w_k"][l].astype(BF16),
        w_kt=jnp.swapaxes(w["w_k"][l], 1, 2).astype(BF16), w_v=w["w_v"][l].astype(BF16),
        b_i=jnp.pad(w["b_i"][l], (0, LANES - nh)).reshape(1, LANES),
        b_f=jnp.pad(w["b_f"][l], (0, LANES - nh)).reshape(1, LANES),
        g_mh=w["g_mh"][l].reshape(1, db), k_scale=1.0 / math.sqrt(LANES),
        w_out=w["w_out"][l].astype(BF16),
        w_router=w_r.astype(BF16), b_router=b_r,
        n_groups=n_groups, per_group=per_group,
    )


def _mix_weights(w_spatial, b_spatial, L, da):
    nh = w_spatial.shape[0]
    causal = jnp.tril(jnp.ones((L, L), dtype=bool))
    w = jnp.where(causal[None], w_spatial[:, :L, :L], 0.0)
    reps = TILE // L
    if reps > 1:
        eye = jnp.eye(reps, dtype=F32)
        w = jnp.einsum("ab,hts->hatbs", eye, w).reshape(nh, TILE, TILE)
    bias = jnp.tile(jnp.transpose(b_spatial[:, :L]), (reps, 1))
    bias_full = jnp.repeat(bias, da // nh, axis=1)
    return w.astype(BF16), bias_full


def kernel(x_prompt, x_sample, c_prompt, c_sample, state_mlstm_C, state_mlstm_n, state_mlstm_m, state_conv, w_ada, b_ada, g_norm1, g_norm2, w_in, ln_v_g, ln_v_b, w_spatial, b_spatial, w_conv, b_conv, w_q, w_k, w_v, b_i, b_f, g_mh, w_out, w_r1, b_r1, w_r2, b_r2, w_e_gate, w_e_up, w_e_down, g_final):
    weights = dict(w_in=w_in, ln_v_g=ln_v_g, ln_v_b=ln_v_b, w_conv=w_conv, b_conv=b_conv, w_q=w_q, w_k=w_k,
                   w_v=w_v, b_i=b_i, b_f=b_f, g_mh=g_mh, w_out=w_out, w_r1=w_r1, b_r1=b_r1, w_r2=w_r2, b_r2=b_r2,
                   g_norm1=g_norm1, g_norm2=g_norm2)
    depth = w_in.shape[0]
    bp, tp, d = x_prompt.shape
    bs, ts, _ = x_sample.shape
    nh = w_q.shape[1]
    da = ln_v_g.shape[1]
    db = w_conv.shape[2]
    n_mod = w_ada.shape[2] // d
    n_experts = w_e_gate.shape[1]
    n_p, n_s = bp * tp, bs * ts
    assert da == db and da % LANES == 0 and nh * LANES == db and tp % TILE == 0 and TILE % ts == 0
    assert ts >= CONV_W - 1 and ts % SUBLANES == 0 and n_s % TILE == 0
    xm_col, zo_col = 2 * da // db, 2 * da // db + 1

    mod = _ada(jnp.concatenate([c_prompt, c_sample], axis=0), w_ada, b_ada).reshape(depth, bp + bs, n_mod, 1, d)

    xp, xs = x_prompt, x_sample
    outs = {k: [] for k in ("cp", "np", "mp", "bp", "cs", "ns", "ms", "bs", "vs")}
    for l in range(depth):
        p = _layer_params(l, weights)
        mods_p = [mod[l, :bp, i] for i in range(n_mod)]
        mods_s = [mod[l, bp:, i] for i in range(n_mod)]

        proj_p, gates_p = _inproj(xp, mods_p[1], mods_p[0], p["g_norm1"], p["w_main"], p["n_main"], p["w_gate"], 1024)
        w_mix_p, bias_p = _mix_weights(w_spatial[l], b_spatial[l], min(tp, TILE), da)
        (ya_p,) = _grpa(proj_p, p["ln_g"], p["ln_b"], w_mix_p, bias_p, da, False)
        yb_p, c_p, n_pn, m_p = _mlstm_prompt(proj_p, gates_p, p, bp, tp, db, nh, xm_col, zo_col)
        outs["cp"].append(c_p)
        outs["np"].append(n_pn.reshape(bp, nh, LANES))
        outs["mp"].append(m_p[:, 0, :nh])
        outs["bp"].append(proj_p.reshape(bp, tp, -1)[:, tp - (CONV_W - 1):, 2 * da:2 * da + db])

        proj_s, gates_s = _inproj(xs, mods_s[1], mods_s[0], p["g_norm1"], p["w_main"], p["n_main"], p["w_gate"], 512)
        w_mix_s, bias_s = _mix_weights(w_spatial[l], b_spatial[l], min(ts, TILE), da)
        ya_s, v_s = _grpa(proj_s, p["ln_g"], p["ln_b"], w_mix_s, bias_s, da, True)
        prev_rows = jnp.pad(state_conv[l], ((0, 0), (ts - (CONV_W - 1), 0), (0, 0))).reshape(n_s, db)
        n0_rows = jnp.repeat(state_mlstm_n[l].reshape(bs, db), ts, axis=0)
        m0_rows = jnp.pad(jnp.repeat(state_mlstm_m[l], ts, axis=0), ((0, 0), (0, LANES - nh)))
        yb_s, c_s, n_rows_s, m_rows_s = _mlstm_sample(proj_s, gates_s, prev_rows, state_mlstm_C, l, n0_rows, m0_rows,
                                                      p, math.gcd(ts, TILE), db, nh, xm_col, zo_col)
        outs["cs"].append(c_s)
        outs["ns"].append(n_rows_s.reshape(bs, ts, nh, LANES)[:, ts - 1])
        outs["ms"].append(m_rows_s.reshape(bs, ts, LANES)[:, ts - 1, :nh])
        outs["bs"].append(proj_s.reshape(bs, ts, -1)[:, ts - (CONV_W - 1):, 2 * da:2 * da + db])
        outs["vs"].append(v_s.reshape(bs, ts, da))

        rargs = (p["w_router"], p["b_router"], p["n_groups"], p["per_group"])
        xp, h2_p, rt_p = _outproj(xp, ya_p, yb_p, p["w_out"], mods_p[2], mods_p[4], mods_p[3], p["g_norm2"], *rargs, 512)
        xs, h2_s, rt_s = _outproj(xs, ya_s, yb_s, p["w_out"], mods_s[2], mods_s[4], mods_s[3], p["g_norm2"], *rargs, 256)

        route = jnp.concatenate([rt_p, rt_s], axis=0)
        plan = _moe_plan(route[:, :TOP_K].astype(jnp.int32), n_experts)
        x_buf = _gather_rows(plan["row_src"], plan["used_rows"], h2_p, h2_s, BF16)
        h_buf = _expert_up(plan, x_buf, w_e_gate, w_e_up, l)
        y_buf = _expert_down(plan, h_buf, w_e_down, l)
        dest_t = jnp.transpose(plan["dest"].reshape(n_p + n_s, TOP_K)).reshape(-1)
        last = l == depth - 1
        xp = _combine(xp, y_buf, dest_t, 0, rt_p, mods_p[5], g_final, last, 256)
        xs = _combine(xs, y_buf, dest_t, n_p, rt_s, mods_s[5], g_final, last, 256)

    st = jnp.stack
    return (xp, xs, st(outs["cp"]), st(outs["np"]), st(outs["mp"]), st(outs["bp"]),
            st(outs["cs"]), st(outs["ns"]), st(outs["ms"]), st(outs["bs"]), st(outs["vs"]))
```

```python
import functools
import math

import jax
import jax.numpy as jnp
from jax import lax
from jax.experimental import pallas as pl
from jax.experimental.pallas import tpu as pltpu

F32 = jnp.float32
BF16 = jnp.bfloat16
EPS = 1e-6
LANES = 128
SUBLANES = 8
TILE = 128
CONV_W = 4
TOP_K = 2
VMEM_LIMIT_BYTES = 56 * 1024 * 1024
MOE_BLOCK_ROWS = 256
GATHER_CHUNK = 512
CAST_CHUNK = 256
NEG_INF = float("-inf")


def _cparams(sem):
    return pltpu.CompilerParams(dimension_semantics=sem, vmem_limit_bytes=VMEM_LIMIT_BYTES)


def _split3(x):
    x1 = x.astype(BF16)
    r1 = x - x1.astype(F32)
    x2 = r1.astype(BF16)
    r2 = r1 - x2.astype(F32)
    return x1, x2, r2.astype(BF16)


def _ada_kernel(c_ref, w_ref, b_ref, o_ref):
    c = c_ref[...]
    a = c * jax.nn.sigmoid(c)
    o_ref[0] = jnp.dot(a.astype(BF16), w_ref[0].astype(BF16), preferred_element_type=F32) + b_ref[0]


def _ada(c_all, w_ada, b_ada):
    depth, d, n = w_ada.shape
    bc = c_all.shape[0]
    tn = min(1024, n)
    return pl.pallas_call(
        _ada_kernel,
        grid=(depth, n // tn),
        in_specs=[pl.BlockSpec((bc, d), lambda l, j: (0, 0)),
                  pl.BlockSpec((1, d, tn), lambda l, j: (l, 0, j)),
                  pl.BlockSpec((1, 1, tn), lambda l, j: (l, 0, j))],
        out_specs=pl.BlockSpec((1, bc, tn), lambda l, j: (l, 0, j)),
        out_shape=jax.ShapeDtypeStruct((depth, bc, n), F32),
        compiler_params=_cparams(("arbitrary", "arbitrary")),
        name="ada",
    )(c_all, w_ada, b_ada.reshape(depth, 1, n))


def _inproj_kernel(x_ref, sc_ref, sh_ref, g_ref, w_ref, wg_ref, o_ref, og_ref, hn_ref):
    @pl.when(pl.program_id(2) == 0)
    def _():
        x = x_ref[...]
        ms = jnp.mean(x * x, axis=-1, keepdims=True)
        h = x * lax.rsqrt(ms + EPS) * g_ref[...]
        h = h * (1.0 + sc_ref[...]) + sh_ref[...]
        hb = h.reshape(hn_ref.shape).astype(BF16)
        hn_ref[...] = hb
        og_ref[...] = jnp.dot(hb, wg_ref[...], preferred_element_type=F32)

    o_ref[...] = jnp.dot(hn_ref[...], w_ref[...], preferred_element_type=F32)


def _row_tiling(b, t, max_rows):
    if t >= max_rows:
        assert t % max_rows == 0
        return 1, max_rows
    bb = min(b, max_rows // t)
    assert b % bb == 0 and t % SUBLANES == 0
    return bb, t


def _inproj(x, scale, shift, g_norm, w_main, n, w_gate, max_rows):
    b, t, d = x.shape
    ng = w_gate.shape[1]
    bb, tt = _row_tiling(b, t, max_rows)
    rows = bb * tt
    tn = min(512, n)
    nt = t // tt
    return pl.pallas_call(
        _inproj_kernel,
        grid=(b // bb, nt, n // tn),
        in_specs=[pl.BlockSpec((bb, tt, d), lambda i, k, j: (i, k, 0)),
                  pl.BlockSpec((bb, 1, d), lambda i, k, j: (i, 0, 0)),
                  pl.BlockSpec((bb, 1, d), lambda i, k, j: (i, 0, 0)),
                  pl.BlockSpec((1, 1, d), lambda i, k, j: (0, 0, 0)),
                  pl.BlockSpec((d, tn), lambda i, k, j: (0, j)),
                  pl.BlockSpec((d, ng), lambda i, k, j: (0, 0))],
        out_specs=[pl.BlockSpec((rows, tn), lambda i, k, j: (i * nt + k, j)),
                   pl.BlockSpec((rows, ng), lambda i, k, j: (i * nt + k, 0))],
        out_shape=[jax.ShapeDtypeStruct((b * t, n), F32), jax.ShapeDtypeStruct((b * t, ng), F32)],
        scratch_shapes=[pltpu.VMEM((rows, d), BF16)],
        compiler_params=_cparams(("arbitrary", "arbitrary", "arbitrary")),
        name="inproj",
    )(x, scale, shift, g_norm.reshape(1, 1, d), w_main, w_gate)


def _grpa_kernel(u_ref, v_ref, lg_ref, lb_ref, w_ref, bias_ref, ya_ref, *maybe_v_out):
    rows, da = u_ref.shape
    nh = w_ref.shape[0]
    g = jax.nn.gelu(v_ref[...])
    mu = jnp.mean(g, axis=-1, keepdims=True)
    gc = g - mu
    var = jnp.mean(gc * gc, axis=-1, keepdims=True)
    v = gc * lax.rsqrt(var + EPS) * lg_ref[...] + lb_ref[...]
    if maybe_v_out:
        maybe_v_out[0][...] = v
    vb = v.astype(BF16)
    for c in range(rows // TILE):
        r0 = c * TILE
        for h in range(nh):
            c0 = h * LANES
            z = jnp.dot(w_ref[h], vb[r0:r0 + TILE, c0:c0 + LANES], preferred_element_type=F32)
            u = jax.nn.gelu(u_ref[r0:r0 + TILE, c0:c0 + LANES])
            y = u * (z + bias_ref[:, c0:c0 + LANES])
            ya_ref[r0:r0 + TILE, c0:c0 + LANES] = y.astype(BF16)


def _grpa(proj, ln_g, ln_b, w_mix, bias_full, da, want_v):
    n_rows = proj.shape[0]
    rows = min(512, n_rows)
    nh = w_mix.shape[0]
    out_specs = [pl.BlockSpec((rows, da), lambda i: (i, 0))]
    out_shape = [jax.ShapeDtypeStruct((n_rows, da), BF16)]
    if want_v:
        out_specs.append(pl.BlockSpec((rows, da), lambda i: (i, 0)))
        out_shape.append(jax.ShapeDtypeStruct((n_rows, da), F32))
    return pl.pallas_call(
        _grpa_kernel,
        grid=(n_rows // rows,),
        in_specs=[pl.BlockSpec((rows, da), lambda i: (i, 0)),
                  pl.BlockSpec((rows, da), lambda i: (i, 1)),
                  pl.BlockSpec((1, da), lambda i: (0, 0)),
                  pl.BlockSpec((1, da), lambda i: (0, 0)),
                  pl.BlockSpec((nh, TILE, TILE), lambda i: (0, 0, 0)),
                  pl.BlockSpec((TILE, da), lambda i: (0, 0))],
        out_specs=out_specs,
        out_shape=out_shape,
        compiler_params=_cparams(("arbitrary",)),
        name="grpa",
    )(proj, proj, ln_g.reshape(1, da), ln_b.reshape(1, da), w_mix, bias_full)


def _group_last(x, L):
    if L == TILE:
        return jnp.broadcast_to(x[TILE - 1:TILE, :], x.shape)
    g = TILE // L
    x3 = x.reshape(g, L, x.shape[1])
    return jnp.broadcast_to(x3[:, L - 1:L, :], x3.shape).reshape(x.shape)


def _mlstm_kernel(*refs, L, carry, nh, k_scale):
    if carry:
        (xm_ref, zo_ref, g_ref, wconv_ref, bconv_ref, wq_ref, wk_ref, wkt_ref, wv_ref, bi_ref, bf_ref, gmh_ref,
         yb_ref, c_ref, n_ref, m_ref, prev_scr) = refs
        c_in_ref = c_ref
        first = pl.program_id(1) == 0

        @pl.when(first)
        def _():
            c_ref[...] = jnp.zeros(c_ref.shape, F32)
            n_ref[...] = jnp.zeros(n_ref.shape, F32)
            m_ref[...] = jnp.zeros(m_ref.shape, F32)
            prev_scr[...] = jnp.zeros(prev_scr.shape, F32)

        prev = prev_scr[...]
        n0_rows = jnp.broadcast_to(n_ref[0], (TILE, n_ref.shape[2]))
        m0 = jnp.broadcast_to(m_ref[0], (TILE, LANES))
    else:
        (xm_ref, zo_ref, g_ref, prev_ref, c_layer_ref, n0_ref, m0_ref,
         wconv_ref, bconv_ref, wq_ref, wk_ref, wkt_ref, wv_ref, bi_ref, bf_ref, gmh_ref,
         yb_ref, c_ref, n_ref, m_ref) = refs
        c_in_ref = c_layer_ref.at[0]
        prev = prev_ref[...]
        n0_rows = n0_ref[...]
        m0 = m0_ref[...]

    groups = TILE // L
    x = xm_ref[...]
    row_in_group = lax.broadcasted_iota(jnp.int32, (TILE, 1), 0) % L

    acc = bconv_ref[...] + wconv_ref[CONV_W - 1:CONV_W, :] * x
    for j in range(1, CONV_W):
        from_prev = pltpu.roll(prev, (j - L) % TILE, 0)
        shifted = jnp.where(row_in_group < j, from_prev, pltpu.roll(x, j, 0))
        acc = acc + wconv_ref[CONV_W - 1 - j:CONV_W - j, :] * shifted
    xc = acc * jax.nn.sigmoid(acc)
    if carry:
        prev_scr[...] = x

    gates = g_ref[...]
    gi = gates[:, :LANES] + bi_ref[...]
    gf = gates[:, LANES:] + bf_ref[...]
    lf = jnp.minimum(gf, 0.0) - jnp.log1p(jnp.exp(-jnp.abs(gf)))

    r_i = lax.broadcasted_iota(jnp.int32, (TILE, TILE), 0)
    c_i = lax.broadcasted_iota(jnp.int32, (TILE, TILE), 1)
    mask = (c_i <= r_i) & ((c_i // L) == (r_i // L)) if L != TILE else (c_i <= r_i)
    mask_b = jnp.where(mask, 1.0, 0.0).astype(BF16)
    same_group = ((c_i // L) == (r_i // L)) if L != TILE else (c_i >= 0)
    ones_b = jnp.ones((TILE, LANES), BF16)
    lf1, lf2, lf3 = _split3(lf)
    cum = (jnp.dot(mask_b, lf1, preferred_element_type=F32) + jnp.dot(mask_b, lf2, preferred_element_type=F32)
           + jnp.dot(mask_b, lf3, preferred_element_type=F32))
    cum_t = cum.T
    gi_t = gi.T
    log_inter = cum + m0

    def d_matrix(h):
        d = cum[:, h:h + 1] - cum_t[h:h + 1, :] + gi_t[h:h + 1, :]
        return jnp.where(mask, d, NEG_INF)

    run = gi - cum
    step = 1
    while step < L:
        run = jnp.maximum(run, jnp.where(row_in_group >= step, pltpu.roll(run, step, 0), NEG_INF))
        step *= 2
    mt = jnp.maximum(log_inter, cum + run)

    m_last = _group_last(mt, L)
    w_inter = jnp.exp(log_inter - mt)
    w_last = jnp.exp(_group_last(cum, L) - cum + gi - m_last)
    w_last_t = w_last.T
    inv_floor = jnp.exp(-mt)

    heads = range(nh)
    cols = [slice(h * LANES, (h + 1) * LANES) for h in heads]
    nt_dims = (((1,), (1,)), ((), ()))

    def mm(a, b):
        return jnp.dot(a, b, preferred_element_type=F32)

    xcb = [xc[:, cols[h]].astype(BF16) for h in heads]
    q = [mm(xcb[h], wq_ref[h]) for h in heads]
    k = [mm(xcb[h], wk_ref[h]) * k_scale for h in heads]
    k_t = [lax.dot_general(wkt_ref[h], xcb[h], nt_dims, preferred_element_type=F32) * k_scale for h in heads]
    vb = [mm(x[:, cols[h]].astype(BF16), wv_ref[h]).astype(BF16) for h in heads]
    qb = [q[h].astype(BF16) for h in heads]

    w_intra = [jnp.exp(d_matrix(h) - mt[:, h:h + 1]) for h in heads]
    s = [mm(qb[h], k_t[h].astype(BF16)) * w_intra[h] for h in heads]
    if carry:
        qc = [mm(qb[h], c_in_ref[0, h].astype(BF16)) for h in heads]
    else:
        qc = [jnp.concatenate([mm(q[h][g * L:(g + 1) * L, :].astype(BF16), c_in_ref[g, h].astype(BF16))
                               for g in range(groups)], axis=0) for h in heads]

    s_hi = [s[h].astype(BF16) for h in heads]
    s_lo = [(s[h] - s_hi[h].astype(F32)).astype(BF16) for h in heads]
    sv = [mm(s_hi[h], jnp.concatenate([vb[h], ones_b], axis=1)) for h in heads]
    s_sum = [sv[h][:, LANES:] + mm(s_lo[h], ones_b) for h in heads]
    wi_rep = [jnp.broadcast_to(w_inter[:, h:h + 1], (TILE, LANES)) for h in heads]
    n0_b = [n0_rows[:, cols[h]].astype(BF16) for h in heads]
    if carry:
        qn = [lax.dot_general(qb[h], n0_b[h], nt_dims, preferred_element_type=F32) for h in heads]
    else:
        qn = [jnp.sum(qb[h].astype(F32) * n0_b[h].astype(F32), axis=1, keepdims=True) for h in heads]
    den = [wi_rep[h] * qn[h] + s_sum[h] for h in heads]
    hh = [(wi_rep[h] * qc[h] + sv[h][:, :LANES]) / jnp.maximum(jnp.abs(den[h]), inv_floor[:, h:h + 1])
          for h in heads]
    h_sq = [mm((hh[h] * hh[h]).astype(BF16), ones_b) for h in heads]
    for h in heads:
        hn = hh[h] * lax.rsqrt(h_sq[h] * (1.0 / LANES) + EPS) * gmh_ref[:, cols[h]]
        yb_ref[:, cols[h]] = (jax.nn.sigmoid(zo_ref[:, cols[h]]) * hn).astype(BF16)

    for h in heads:
        w_last_row = w_last_t[h:h + 1, :]
        kw_t = k_t[h] * w_last_row
        wl_rows = jnp.where(same_group, w_last_row, 0.0).astype(BF16)
        n_rows = wi_rep[h] * n0_rows[:, cols[h]] + mm(wl_rows, k[h].astype(BF16))
        if carry:
            decay = jnp.broadcast_to(w_inter[TILE - 1:TILE, h:h + 1], (LANES, LANES))
            c_ref[0, h] = decay * c_in_ref[0, h] + mm(kw_t.astype(BF16), vb[h])
            n_ref[0, :, cols[h]] = n_rows[TILE - 1:TILE, :]
        else:
            n_ref[:, cols[h]] = n_rows
            for g in range(groups):
                kw_g = jnp.where((c_i // L) == g, kw_t, 0.0).astype(BF16)
                last = g * L + L - 1
                decay = jnp.broadcast_to(w_inter[last:last + 1, h:h + 1], (LANES, LANES))
                c_ref[g, h] = decay * c_in_ref[g, h] + mm(kw_g, vb[h])

    if carry:
        m_ref[0] = mt[TILE - 1:TILE, :]
    else:
        m_ref[...] = mt


def _mlstm_weights(p):
    return [p["w_conv"], p["b_conv"], p["w_q"], p["w_k"], p["w_kt"], p["w_v"], p["b_i"], p["b_f"], p["g_mh"]]


def _mlstm_weight_specs(nh, db, nidx):
    z2 = (lambda i, c: (0, 0)) if nidx == 2 else (lambda i: (0, 0))
    z3 = (lambda i, c: (0, 0, 0)) if nidx == 2 else (lambda i: (0, 0, 0))
    hw = pl.BlockSpec((nh, LANES, LANES), z3)
    return [pl.BlockSpec((CONV_W, db), z2), pl.BlockSpec((1, db), z2), hw, hw, hw, hw,
            pl.BlockSpec((1, LANES), z2), pl.BlockSpec((1, LANES), z2), pl.BlockSpec((1, db), z2)]


def _mlstm_prompt(proj, gates, p, b, t, db, nh, xm_col, zo_col):
    nt = t // TILE
    kern = functools.partial(_mlstm_kernel, L=TILE, carry=True, nh=nh, k_scale=p["k_scale"])
    return pl.pallas_call(
        kern,
        grid=(b, nt),
        in_specs=[pl.BlockSpec((TILE, db), lambda i, c: (i * nt + c, xm_col)),
                  pl.BlockSpec((TILE, db), lambda i, c: (i * nt + c, zo_col)),
                  pl.BlockSpec((TILE, 2 * LANES), lambda i, c: (i * nt + c, 0))] + _mlstm_weight_specs(nh, db, 2),
        out_specs=[pl.BlockSpec((TILE, db), lambda i, c: (i * nt + c, 0)),
                   pl.BlockSpec((1, nh, LANES, LANES), lambda i, c: (i, 0, 0, 0)),
                   pl.BlockSpec((1, 1, db), lambda i, c: (i, 0, 0)),
                   pl.BlockSpec((1, 1, LANES), lambda i, c: (i, 0, 0))],
        out_shape=[jax.ShapeDtypeStruct((b * t, db), BF16),
                   jax.ShapeDtypeStruct((b, nh, LANES, LANES), F32),
                   jax.ShapeDtypeStruct((b, 1, db), F32),
                   jax.ShapeDtypeStruct((b, 1, LANES), F32)],
        scratch_shapes=[pltpu.VMEM((TILE, db), F32)],
        compiler_params=_cparams(("arbitrary", "arbitrary")),
        name="mlstm_prompt",
    )(proj, proj, gates, *_mlstm_weights(p))


def _mlstm_sample(proj, gates, prev_rows, c0, layer, n0_rows, m0_rows, p, L, db, nh, xm_col, zo_col):
    n_rows = proj.shape[0]
    groups = TILE // L
    kern = functools.partial(_mlstm_kernel, L=L, carry=False, nh=nh, k_scale=p["k_scale"])
    row_spec = pl.BlockSpec((TILE, db), lambda i: (i, 0))
    return pl.pallas_call(
        kern,
        grid=(n_rows // TILE,),
        in_specs=[pl.BlockSpec((TILE, db), lambda i: (i, xm_col)),
                  pl.BlockSpec((TILE, db), lambda i: (i, zo_col)),
                  pl.BlockSpec((TILE, 2 * LANES), lambda i: (i, 0)),
                  row_spec,
                  pl.BlockSpec((1, groups, nh, LANES, LANES), lambda i: (layer, i, 0, 0, 0)),
                  row_spec,
                  pl.BlockSpec((TILE, LANES), lambda i: (i, 0))] + _mlstm_weight_specs(nh, db, 1),
        out_specs=[row_spec,
                   pl.BlockSpec((groups, nh, LANES, LANES), lambda i: (i, 0, 0, 0)),
                   row_spec,
                   pl.BlockSpec((TILE, LANES), lambda i: (i, 0))],
        out_shape=[jax.ShapeDtypeStruct((n_rows, db), BF16),
                   jax.ShapeDtypeStruct(c0.shape[1:], F32),
                   jax.ShapeDtypeStruct((n_rows, db), F32),
                   jax.ShapeDtypeStruct((n_rows, LANES), F32)],
        compiler_params=_cparams(("arbitrary",)),
        name="mlstm_sample",
    )(proj, proj, gates, prev_rows, c0, n0_rows, m0_rows, *_mlstm_weights(p))


def _route(lg, n_groups, per_group):
    lane_i = lax.broadcasted_iota(jnp.int32, lg.shape, 1)
    lane = lane_i.astype(F32)
    far = float(LANES)
    is_g = lane_i < n_groups
    lgg = jnp.where(is_g, lg, NEG_INF)
    mg = jnp.max(lgg, axis=1, keepdims=True)
    g_sel = jnp.min(jnp.where(lgg == mg, lane, far), axis=1, keepdims=True)
    p_g = 1.0 / jnp.sum(jnp.where(is_g, jnp.exp(lgg - mg), 0.0), axis=1, keepdims=True)
    lo = n_groups + g_sel * per_group
    sel = (lane >= lo) & (lane < lo + per_group)
    le = jnp.where(sel, lg, NEG_INF)
    m1 = jnp.max(le, axis=1, keepdims=True)
    i1 = jnp.min(jnp.where(le == m1, lane, far), axis=1, keepdims=True)
    le2 = jnp.where(lane == i1, NEG_INF, le)
    m2 = jnp.max(le2, axis=1, keepdims=True)
    i2 = jnp.min(jnp.where(le2 == m2, lane, far), axis=1, keepdims=True)
    ratio = jnp.exp(m2 - m1)
    gate1 = p_g / (1.0 + ratio)
    gate2 = p_g * ratio / (1.0 + ratio)
    out = jnp.where(lane_i == 0, i1 - n_groups, 0.0)
    out = jnp.where(lane_i == 1, i2 - n_groups, out)
    out = jnp.where(lane_i == 2, gate1, out)
    return jnp.where(lane_i == 3, gate2, out)


def _outproj_kernel(x_ref, ya_ref, yb_ref, w_ref, g1_ref, sc_ref, sh_ref, gn_ref, wr_ref, br_ref,
                    xn_ref, h2_ref, rt_ref, *, n_groups, per_group):
    da = ya_ref.shape[1]
    mix = (jnp.dot(ya_ref[...], w_ref[:da, :], preferred_element_type=F32)
           + jnp.dot(yb_ref[...], w_ref[da:, :], preferred_element_type=F32))
    x = x_ref[...] + g1_ref[...] * mix.reshape(x_ref.shape)
    xn_ref[...] = x
    ms = jnp.mean(x * x, axis=-1, keepdims=True)
    h = x * lax.rsqrt(ms + EPS) * gn_ref[...]
    h = (h * (1.0 + sc_ref[...]) + sh_ref[...]).reshape(h2_ref.shape)
    h2_ref[...] = h
    lg = jnp.dot(h.astype(BF16), wr_ref[...], preferred_element_type=F32) + br_ref[...]
    rt_ref[...] = _route(lg, n_groups, per_group)


def _outproj(x, ya, yb, w_out, gate, scale, shift, g_norm, w_router, b_router, n_groups, per_group, max_rows):
    b, t, d = x.shape
    bb, tt = _row_tiling(b, t, max_rows)
    rows = bb * tt
    nt = t // tt
    da = ya.shape[1]
    xspec = pl.BlockSpec((bb, tt, d), lambda i, k: (i, k, 0))
    mspec = pl.BlockSpec((bb, 1, d), lambda i, k: (i, 0, 0))
    kern = functools.partial(_outproj_kernel, n_groups=n_groups, per_group=per_group)
    return pl.pallas_call(
        kern,
        grid=(b // bb, nt),
        in_specs=[xspec,
                  pl.BlockSpec((rows, da), lambda i, k: (i * nt + k, 0)),
                  pl.BlockSpec((rows, ya.shape[1]), lambda i, k: (i * nt + k, 0)),
                  pl.BlockSpec(w_out.shape, lambda i, k: (0, 0), pipeline_mode=pl.Buffered(1)),
                  mspec, mspec, mspec,
                  pl.BlockSpec((1, 1, d), lambda i, k: (0, 0, 0)),
                  pl.BlockSpec(w_router.shape, lambda i, k: (0, 0), pipeline_mode=pl.Buffered(1)),
                  pl.BlockSpec((1, LANES), lambda i, k: (0, 0))],
        out_specs=[xspec,
                   pl.BlockSpec((rows, d), lambda i, k: (i * nt + k, 0)),
                   pl.BlockSpec((rows, LANES), lambda i, k: (i * nt + k, 0))],
        out_shape=[jax.ShapeDtypeStruct(x.shape, F32),
                   jax.ShapeDtypeStruct((b * t, d), F32),
                   jax.ShapeDtypeStruct((b * t, LANES), F32)],
        compiler_params=_cparams(("arbitrary", "arbitrary")),
        name="outproj",
    )(x, ya, yb, w_out, gate, scale, shift, g_norm.reshape(1, 1, d), w_router, b_router)


def _gather_kernel(idx_ref, used_ref, src_a_ref, src_b_ref, out_ref, buf, sem, *, n_a, chunk):
    i = pl.program_id(0)

    def in_use(step):
        return step * chunk < used_ref[0]

    def issue_step(step, slot):
        def issue(r, _):
            t = idx_ref[step * chunk + r]
            dst = buf.at[slot, pl.ds(r, 1)]

            @pl.when(t < n_a)
            def _():
                pltpu.make_async_copy(src_a_ref.at[pl.ds(t, 1)], dst, sem.at[slot]).start()

            @pl.when(t >= n_a)
            def _():
                pltpu.make_async_copy(src_b_ref.at[pl.ds(t - n_a, 1)], dst, sem.at[slot]).start()

            return 0

        lax.fori_loop(0, chunk, issue, 0, unroll=8)

    @pl.when(i == 0)
    def _():
        issue_step(0, 0)

    @pl.when((i + 1 < pl.num_programs(0)) & in_use(i + 1))
    def _():
        issue_step(i + 1, (i + 1) % 2)

    @pl.when(in_use(i))
    def _():
        slot = i % 2
        pltpu.make_async_copy(buf.at[slot], buf.at[slot], sem.at[slot]).wait()
        out_ref[...] = buf[slot].astype(out_ref.dtype)

    @pl.when(jnp.logical_not(in_use(i)))
    def _():
        out_ref[...] = jnp.zeros(out_ref.shape, out_ref.dtype)


def _gather_rows(idx, n_used, src_a, src_b, out_dtype):
    n_out = idx.shape[0]
    chunk = math.gcd(GATHER_CHUNK, n_out)
    d = src_a.shape[1]
    kern = functools.partial(_gather_kernel, n_a=src_a.shape[0], chunk=chunk)
    return pl.pallas_call(
        kern,
        grid_spec=pltpu.PrefetchScalarGridSpec(
            num_scalar_prefetch=2,
            grid=(n_out // chunk,),
            in_specs=[pl.BlockSpec(memory_space=pl.ANY), pl.BlockSpec(memory_space=pl.ANY)],
            out_specs=pl.BlockSpec((chunk, d), lambda i, idx, used: (i, 0)),
            scratch_shapes=[pltpu.VMEM((2, chunk, d), src_a.dtype), pltpu.SemaphoreType.DMA((2,))]),
        out_shape=jax.ShapeDtypeStruct((n_out, d), out_dtype),
        compiler_params=_cparams(("arbitrary",)),
        name="gather_rows",
    )(idx, n_used, src_a, src_b)


def _expert_up_kernel(ie_ref, ij_ref, ib_ref, ioj_ref, inew_ref, islot_ref, ine_ref, inj_ref, ihas_ref, n_ref,
                      x_ref, wg_hbm, wu_hbm, o_ref, stage, wgb_scr, wub_scr, sem, *, layer):
    i = pl.program_id(0)
    tn = wgb_scr.shape[1]

    def weight_copies(e, j, slot):
        col = pl.multiple_of(j * tn, tn)
        return (pltpu.make_async_copy(wg_hbm.at[layer, e, :, pl.ds(col, tn)], stage.at[slot, 0], sem.at[slot, 0]),
                pltpu.make_async_copy(wu_hbm.at[layer, e, :, pl.ds(col, tn)], stage.at[slot, 1], sem.at[slot, 1]))

    @pl.when(i == 0)
    def _():
        for cp in weight_copies(ie_ref[0], ij_ref[0], islot_ref[0]):
            cp.start()

    @pl.when(i >= n_ref[0])
    def _():
        o_ref[...] = jnp.zeros(o_ref.shape, o_ref.dtype)

    @pl.when(i < n_ref[0])
    def _():
        @pl.when(inew_ref[i] == 1)
        def _():
            slot = islot_ref[i]

            @pl.when(ihas_ref[i] == 1)
            def _():
                for cp in weight_copies(ine_ref[i], inj_ref[i], 1 - slot):
                    cp.start()

            for cp in weight_copies(ie_ref[i], ij_ref[i], slot):
                cp.wait()
            a = u = None
            for k0 in range(0, x_ref.shape[1], CAST_CHUNK):
                ks = slice(k0, k0 + CAST_CHUNK)
                wg_k = stage[slot, 0, ks, :].astype(BF16)
                wu_k = stage[slot, 1, ks, :].astype(BF16)
                wgb_scr[ks, :] = wg_k
                wub_scr[ks, :] = wu_k
                a_k = jnp.dot(x_ref[:, ks], wg_k, preferred_element_type=F32)
                u_k = jnp.dot(x_ref[:, ks], wu_k, preferred_element_type=F32)
                a = a_k if a is None else a + a_k
                u = u_k if u is None else u + u_k
            o_ref[...] = (a * jax.nn.sigmoid(a) * u).astype(BF16)

        @pl.when(inew_ref[i] != 1)
        def _():
            xb = x_ref[...]
            a = jnp.dot(xb, wgb_scr[...], preferred_element_type=F32)
            u = jnp.dot(xb, wub_scr[...], preferred_element_type=F32)
            o_ref[...] = (a * jax.nn.sigmoid(a) * u).astype(BF16)


def _expert_up(plan, x_buf, w_gate, w_up, layer):
    d, de = w_gate.shape[2], w_gate.shape[3]
    n_rows = x_buf.shape[0]
    tn = de // 2
    bm = MOE_BLOCK_ROWS
    up = plan["up"]
    n_items = up["e"].shape[0]
    names = ("e", "j", "b", "oj", "new", "slot", "next_e", "next_j", "has_next", "n")
    return pl.pallas_call(
        functools.partial(_expert_up_kernel, layer=layer),
        grid_spec=pltpu.PrefetchScalarGridSpec(
            num_scalar_prefetch=len(names),
            grid=(n_items,),
            in_specs=[pl.BlockSpec((bm, d), lambda i, *s: (s[2][i], 0)),
                      pl.BlockSpec(memory_space=pl.ANY), pl.BlockSpec(memory_space=pl.ANY)],
            out_specs=pl.BlockSpec((bm, tn), lambda i, *s: (s[2][i], s[3][i])),
            scratch_shapes=[pltpu.VMEM((2, 2, d, tn), F32), pltpu.VMEM((d, tn), BF16), pltpu.VMEM((d, tn), BF16),
                            pltpu.SemaphoreType.DMA((2, 2))]),
        out_shape=jax.ShapeDtypeStruct((n_rows, de), BF16),
        compiler_params=_cparams(("arbitrary",)),
        name="expert_up",
    )(*[up[k] for k in names], x_buf, w_gate, w_up)


def _expert_down_kernel(be_ref, bnew_ref, bslot_ref, bne_ref, bhas_ref, n_ref, h_ref, wd_hbm, o_ref,
                        stage, wdb_scr, sem, *, layer):
    i = pl.program_id(0)

    def weight_copy(e, slot):
        return pltpu.make_async_copy(wd_hbm.at[layer, e], stage.at[slot], sem.at[slot])

    @pl.when(i == 0)
    def _():
        weight_copy(be_ref[0], bslot_ref[0]).start()

    @pl.when(i >= n_ref[0])
    def _():
        o_ref[...] = jnp.zeros(o_ref.shape, o_ref.dtype)

    @pl.when(i < n_ref[0])
    def _():
        @pl.when(bnew_ref[i] == 1)
        def _():
            slot = bslot_ref[i]

            @pl.when(bhas_ref[i] == 1)
            def _():
                weight_copy(bne_ref[i], 1 - slot).start()

            weight_copy(be_ref[i], slot).wait()
            y = None
            for k0 in range(0, h_ref.shape[1], CAST_CHUNK):
                ks = slice(k0, k0 + CAST_CHUNK)
                wd_k = stage[slot, ks, :].astype(BF16)
                wdb_scr[ks, :] = wd_k
                y_k = jnp.dot(h_ref[:, ks], wd_k, preferred_element_type=F32)
                y = y_k if y is None else y + y_k
            o_ref[...] = y

        @pl.when(bnew_ref[i] != 1)
        def _():
            o_ref[...] = jnp.dot(h_ref[...], wdb_scr[...], preferred_element_type=F32)


def _expert_down(plan, h_buf, w_down, layer):
    n_rows, de = h_buf.shape
    d = w_down.shape[3]
    bm = MOE_BLOCK_ROWS
    down = plan["down"]
    n_blocks = down["e"].shape[0]
    names = ("e", "new", "slot", "next_e", "has_next", "n")
    return pl.pallas_call(
        functools.partial(_expert_down_kernel, layer=layer),
        grid_spec=pltpu.PrefetchScalarGridSpec(
            num_scalar_prefetch=len(names),
            grid=(n_blocks,),
            in_specs=[pl.BlockSpec((bm, de), lambda i, *s: (i, 0)),
                      pl.BlockSpec(memory_space=pl.ANY)],
            out_specs=pl.BlockSpec((bm, d), lambda i, *s: (i, 0)),
            scratch_shapes=[pltpu.VMEM((2, de, d), F32), pltpu.VMEM((de, d), BF16), pltpu.SemaphoreType.DMA((2,))]),
        out_shape=jax.ShapeDtypeStruct((n_rows, d), F32),
        compiler_params=_cparams(("arbitrary",)),
        name="expert_down",
    )(*[down[k] for k in names], h_buf, w_down)


def _moe_plan(expert, n_experts):
    bm = MOE_BLOCK_ROWS
    n_tok = expert.shape[0]
    n_asg = n_tok * TOP_K
    e_flat = expert.reshape(-1)
    e_ids = jnp.arange(n_experts, dtype=jnp.int32)

    def take(table, idx):
        return jnp.sum(jnp.where(idx[:, None] == e_ids[None, :], table[None, :], 0), axis=1)

    hit = e_flat[:, None] == e_ids[None, :]
    csum = jnp.cumsum(hit.astype(jnp.int32), axis=0)
    counts = csum[-1]
    rank = jnp.sum(jnp.where(hit, csum, 0), axis=1) - 1
    nblk = (counts + bm - 1) // bm
    bend = jnp.cumsum(nblk)
    bstart = bend - nblk
    dest = take(bstart, e_flat) * bm + rank
    n_blocks = (n_asg + n_experts * (bm - 1) + bm - 1) // bm
    n_rows = -(-n_blocks * bm // GATHER_CHUNK) * GATHER_CHUNK
    tok = jnp.arange(n_asg, dtype=jnp.int32) // TOP_K
    row_src = (jnp.arange(n_rows, dtype=jnp.int32) % n_tok).at[dest].set(tok, unique_indices=True)
    used = bend[-1]

    def expert_of_block(b):
        return jnp.minimum(jnp.sum((bend[None, :] <= b[:, None]).astype(jnp.int32), axis=1), n_experts - 1)

    blk = jnp.arange(n_blocks, dtype=jnp.int32)
    blk_e = expert_of_block(jnp.minimum(blk, used - 1))
    blk_new = (blk == take(bstart, blk_e)).astype(jnp.int32)
    n_items = 2 * n_blocks
    idx = jnp.arange(n_items, dtype=jnp.int32)
    it = jnp.minimum(idx, 2 * used - 1)
    item_e = expert_of_block(it // 2)
    item_bstart = take(bstart, item_e)
    item_nblk = take(nblk, item_e)
    off = it - 2 * item_bstart
    item_j = (off // item_nblk).astype(jnp.int32)
    item_b = (item_bstart + off % item_nblk).astype(jnp.int32)
    item_new = (item_b == item_bstart).astype(jnp.int32)
    spare = idx - 2 * used
    is_spare = spare >= 0
    item_b = jnp.where(is_spare, used + spare // 2, item_b)
    item_oj = jnp.where(is_spare, spare % 2, item_j)
    e_rank = jnp.cumsum((nblk > 0).astype(jnp.int32)) - 1
    e_after = expert_of_block(jnp.minimum(bend, used - 1))
    more_after = (bend < used).astype(jnp.int32)
    down = dict(e=blk_e, new=blk_new, slot=take(e_rank, blk_e) % 2, next_e=take(e_after, blk_e),
                has_next=take(more_after, blk_e), n=used.reshape(1).astype(jnp.int32))
    first_half = item_j == 0
    up = dict(e=item_e, j=item_j, b=item_b, oj=item_oj, new=item_new, slot=item_j,
              next_e=jnp.where(first_half, item_e, take(e_after, item_e)), next_j=jnp.where(first_half, 1, 0),
              has_next=jnp.where(first_half, 1, take(more_after, item_e)),
              n=(2 * used).reshape(1).astype(jnp.int32))
    return dict(dest=dest, row_src=row_src, up=up, down=down,
                used_rows=(used * bm).reshape(1).astype(jnp.int32))


def _combine_kernel(dst_ref, x_ref, rt_ref, g2_ref, gf_ref, y_hbm, o_ref, buf, sem, *, final_norm, tok0, n_all):
    bb, tt, _ = x_ref.shape
    rows = bb * tt
    nt = pl.num_programs(1)
    lin = pl.program_id(0) * nt + pl.program_id(1)

    def issue_step(step, slot):
        base = tok0 + step * rows

        def issue(r, _):
            for k in range(TOP_K):
                row = dst_ref[k * n_all + base + r]
                pltpu.make_async_copy(y_hbm.at[pl.ds(row, 1)], buf.at[slot, k, pl.ds(r, 1)], sem.at[slot]).start()
            return 0

        lax.fori_loop(0, rows, issue, 0, unroll=4)

    @pl.when(lin == 0)
    def _():
        issue_step(0, 0)

    @pl.when(lin + 1 < pl.num_programs(0) * nt)
    def _():
        issue_step(lin + 1, (lin + 1) % 2)

    slot = lin % 2
    pltpu.make_async_copy(buf.at[slot], buf.at[slot], sem.at[slot]).wait()
    rt = rt_ref[...]
    y = rt[:, 2:3] * buf[slot, 0] + rt[:, 3:4] * buf[slot, 1]
    x = x_ref[...] + g2_ref[...] * y.reshape(x_ref.shape)
    if final_norm:
        ms = jnp.mean(x * x, axis=-1, keepdims=True)
        x = x * lax.rsqrt(ms + EPS) * gf_ref[...]
    o_ref[...] = x


def _combine(x, y_buf, dest_t, tok0, route, gate, g_final, final_norm, max_rows):
    b, t, d = x.shape
    bb, tt = _row_tiling(b, t, max_rows)
    rows = bb * tt
    nt = t // tt
    xspec = pl.BlockSpec((bb, tt, d), lambda i, k, dst: (i, k, 0))
    kern = functools.partial(_combine_kernel, final_norm=final_norm, tok0=tok0, n_all=dest_t.shape[0] // TOP_K)
    return pl.pallas_call(
        kern,
        grid_spec=pltpu.PrefetchScalarGridSpec(
            num_scalar_prefetch=1,
            grid=(b // bb, nt),
            in_specs=[xspec,
                      pl.BlockSpec((rows, LANES), lambda i, k, dst: (i * nt + k, 0)),
                      pl.BlockSpec((bb, 1, d), lambda i, k, dst: (i, 0, 0)),
                      pl.BlockSpec((1, 1, d), lambda i, k, dst: (0, 0, 0)),
                      pl.BlockSpec(memory_space=pl.ANY)],
            out_specs=xspec,
            scratch_shapes=[pltpu.VMEM((2, TOP_K, rows, d), F32), pltpu.SemaphoreType.DMA((2,))]),
        out_shape=jax.ShapeDtypeStruct(x.shape, F32),
        compiler_params=_cparams(("arbitrary", "arbitrary")),
        name="combine",
    )(dest_t, x, route, gate, g_final.reshape(1, 1, d), y_buf)


def _layer_params(l, w):
    d = w["w_in"].shape[1]
    nh = w["w_q"].shape[1]
    da = w["ln_v_g"].shape[1]
    db = w["w_conv"].shape[2]
    n_main = 2 * da + 2 * db
    w_in = w["w_in"][l]
    w_gate = jnp.zeros((d, 2 * LANES), BF16)
    w_gate = w_gate.at[:, :nh].set(w_in[:, n_main:n_main + nh].astype(BF16))
    w_gate = w_gate.at[:, LANES:LANES + nh].set(w_in[:, n_main + nh:n_main + 2 * nh].astype(BF16))
    n_groups = w["w_r1"].shape[2]
    per_group = w["w_r2"].shape[3]
    w_r = jnp.concatenate([w["w_r1"][l], jnp.transpose(w["w_r2"][l], (1, 0, 2)).reshape(d, n_groups * per_group)],
                          axis=1)
    n_r = w_r.shape[1]
    w_r = jnp.pad(w_r, ((0, 0), (0, LANES - n_r)))
    b_r =jnp.pad(jnp.concatenate([w["b_r1"][l], w["b_r2"][l].reshape(-1)]), (0, LANES - n_r)).reshape(1, LANES)
    return dict(
        g_norm1=w["g_norm1"][l], g_norm2=w["g_norm2"][l],
        w_main=w_in.astype(BF16), n_main=n_main, w_gate=w_gate,
        ln_g=w["ln_v_g"][l], ln_b=w["ln_v_b"][l],
        w_conv=w["w_conv"][l], b_conv=w["b_conv"][l].reshape(1, db),
        w_q=w["w_q"][l].astype(BF16), w_k=w["w_k"][l].astype(BF16),
        w_kt=jnp.swapaxes(w["w_k"][l], 1, 2).astype(BF16), w_v=w["w_v"][l].astype(BF16),
        b_i=jnp.pad(w["b_i"][l], (0, LANES - nh)).reshape(1, LANES),
        b_f=jnp.pad(w["b_f"][l], (0, LANES - nh)).reshape(1, LANES),
        g_mh=w["g_mh"][l].reshape(1, db), k_scale=1.0 / math.sqrt(LANES),
        w_out=w["w_out"][l].astype(BF16),
        w_router=w_r.astype(BF16), b_router=b_r,
        n_groups=n_groups, per_group=per_group,
    )


def _mix_weights(w_spatial, b_spatial, L, da):
    nh = w_spatial.shape[0]
    causal = jnp.tril(jnp.ones((L, L), dtype=bool))
    w = jnp.where(causal[None], w_spatial[:, :L, :L], 0.0)
    reps = TILE // L
    if reps > 1:
        eye = jnp.eye(reps, dtype=F32)
        w = jnp.einsum("ab,hts->hatbs", eye, w).reshape(nh, TILE, TILE)
    bias = jnp.tile(jnp.transpose(b_spatial[:, :L]), (reps, 1))
    bias_full = jnp.repeat(bias, da // nh, axis=1)
    return w.astype(BF16), bias_full


def kernel(x_prompt, x_sample, c_prompt, c_sample, state_mlstm_C, state_mlstm_n, state_mlstm_m, state_conv, w_ada, b_ada, g_norm1, g_norm2, w_in, ln_v_g, ln_v_b, w_spatial, b_spatial, w_conv, b_conv, w_q, w_k, w_v, b_i, b_f, g_mh, w_out, w_r1, b_r1, w_r2, b_r2, w_e_gate, w_e_up, w_e_down, g_final):
    weights = dict(w_in=w_in, ln_v_g=ln_v_g, ln_v_b=ln_v_b, w_conv=w_conv, b_conv=b_conv, w_q=w_q, w_k=w_k,
                   w_v=w_v, b_i=b_i, b_f=b_f, g_mh=g_mh, w_out=w_out, w_r1=w_r1, b_r1=b_r1, w_r2=w_r2, b_r2=b_r2,
                   g_norm1=g_norm1, g_norm2=g_norm2)
    depth = w_in.shape[0]
    bp, tp, d = x_prompt.shape
    bs, ts, _ = x_sample.shape
    nh = w_q.shape[1]
    da = ln_v_g.shape[1]
    db = w_conv.shape[2]
    n_mod = w_ada.shape[2] // d
    n_experts = w_e_gate.shape[1]
    n_p, n_s = bp * tp, bs * ts
    assert da == db and da % LANES == 0 and nh * LANES == db and tp % TILE == 0 and TILE % ts == 0
    assert ts >= CONV_W - 1 and ts % SUBLANES == 0 and n_s % TILE == 0
    xm_col, zo_col = 2 * da // db, 2 * da // db + 1

    mod = _ada(jnp.concatenate([c_prompt, c_sample], axis=0), w_ada, b_ada).reshape(depth, bp + bs, n_mod, 1, d)

    xp, xs = x_prompt, x_sample
    outs = {k: [] for k in ("cp", "np", "mp", "bp", "cs", "ns", "ms", "bs", "vs")}
    for l in range(depth):
        p = _layer_params(l, weights)
        mods_p = [mod[l, :bp, i] for i in range(n_mod)]
        mods_s = [mod[l, bp:, i] for i in range(n_mod)]

        proj_p, gates_p = _inproj(xp, mods_p[1], mods_p[0], p["g_norm1"], p["w_main"], p["n_main"], p["w_gate"], 1024)
        w_mix_p, bias_p = _mix_weights(w_spatial[l], b_spatial[l], min(tp, TILE), da)
        (ya_p,) = _grpa(proj_p, p["ln_g"], p["ln_b"], w_mix_p, bias_p, da, False)
        yb_p, c_p, n_pn, m_p = _mlstm_prompt(proj_p, gates_p, p, bp, tp, db, nh, xm_col, zo_col)
        outs["cp"].append(c_p)
        outs["np"].append(n_pn.reshape(bp, nh, LANES))
        outs["mp"].append(m_p[:, 0, :nh])
        outs["bp"].append(proj_p.reshape(bp, tp, -1)[:, tp - (CONV_W - 1):, 2 * da:2 * da + db])

        proj_s, gates_s = _inproj(xs, mods_s[1], mods_s[0], p["g_norm1"], p["w_main"], p["n_main"], p["w_gate"], 512)
        w_mix_s, bias_s = _mix_weights(w_spatial[l], b_spatial[l], min(ts, TILE), da)
        ya_s, v_s = _grpa(proj_s, p["ln_g"], p["ln_b"], w_mix_s, bias_s, da, True)
        prev_rows = jnp.pad(state_conv[l], ((0, 0), (ts - (CONV_W - 1), 0), (0, 0))).reshape(n_s, db)
        n0_rows = jnp.repeat(state_mlstm_n[l].reshape(bs, db), ts, axis=0)
        m0_rows = jnp.pad(jnp.repeat(state_mlstm_m[l], ts, axis=0), ((0, 0), (0, LANES - nh)))
        yb_s, c_s, n_rows_s, m_rows_s = _mlstm_sample(proj_s, gates_s, prev_rows, state_mlstm_C, l, n0_rows, m0_rows,
                                                      p, math.gcd(ts, TILE), db, nh, xm_col, zo_col)
        outs["cs"].append(c_s)
        outs["ns"].append(n_rows_s.reshape(bs, ts, nh, LANES)[:, ts - 1])
        outs["ms"].append(m_rows_s.reshape(bs, ts, LANES)[:, ts - 1, :nh])
        outs["bs"].append(proj_s.reshape(bs, ts, -1)[:, ts - (CONV_W - 1):, 2 * da:2 * da + db])
        outs["vs"].append(v_s.reshape(bs, ts, da))

        rargs = (p["w_router"], p["b_router"], p["n_groups"], p["per_group"])
        xp, h2_p, rt_p = _outproj(xp, ya_p, yb_p, p["w_out"], mods_p[2], mods_p[4], mods_p[3], p["g_norm2"], *rargs, 512)
        xs, h2_s, rt_s = _outproj(xs, ya_s, yb_s, p["w_out"], mods_s[2], mods_s[4], mods_s[3], p["g_norm2"], *rargs, 256)

        route = jnp.concatenate([rt_p, rt_s], axis=0)
        plan = _moe_plan(route[:, :TOP_K].astype(jnp.int32), n_experts)
        x_buf = _gather_rows(plan["row_src"], plan["used_rows"], h2_p, h2_s, BF16)
        h_buf = _expert_up(plan, x_buf, w_e_gate, w_e_up, l)
        y_buf = _expert_down(plan, h_buf, w_e_down, l)
        dest_t = jnp.transpose(plan["dest"].reshape(n_p + n_s, TOP_K)).reshape(-1)
        last = l == depth - 1
        xp = _combine(xp, y_buf, dest_t, 0, rt_p, mods_p[5], g_final, last, 256)
        xs = _combine(xs, y_buf, dest_t, n_p, rt_s, mods_s[5], g_final, last, 256)

    st = jnp.stack
    return (xp, xs, st(outs["cp"]), st(outs["np"]), st(outs["mp"]), st(outs["bp"]),
            st(outs["cs"]), st(outs["ns"]), st(outs["ms"]), st(outs["bs"]), st(outs["vs"]))
```

```python
import functools
import math

import jax
import jax.numpy as jnp
from jax import lax
from jax.experimental import pallas as pl
from jax.experimental.pallas import tpu as pltpu

F32 = jnp.float32
BF16 = jnp.bfloat16
EPS = 1e-6
LANES = 128
SUBLANES = 8
TILE = 128
CONV_W = 4
TOP_K = 2
VMEM_LIMIT_BYTES = 56 * 1024 * 1024
MOE_BLOCK_ROWS = 256
GATHER_CHUNK = 512
CAST_CHUNK = 256
EXPERT_COL_TILES = 1
NEG_INF = float("-inf")


def _cparams(sem):
    return pltpu.CompilerParams(dimension_semantics=sem, vmem_limit_bytes=VMEM_LIMIT_BYTES)


def _split3(x):
    x1 = x.astype(BF16)
    r1 = x - x1.astype(F32)
    x2 = r1.astype(BF16)
    r2 = r1 - x2.astype(F32)
    return x1, x2, r2.astype(BF16)


def _ada_kernel(c_ref, w_ref, b_ref, o_ref):
    c = c_ref[...]
    a = c * jax.nn.sigmoid(c)
    o_ref[0] = jnp.dot(a.astype(BF16), w_ref[0].astype(BF16), preferred_element_type=F32) + b_ref[0]


def _ada(c_all, w_ada, b_ada):
    depth, d, n = w_ada.shape
    bc = c_all.shape[0]
    tn = min(1024, n)
    return pl.pallas_call(
        _ada_kernel,
        grid=(depth, n // tn),
        in_specs=[pl.BlockSpec((bc, d), lambda l, j: (0, 0)),
                  pl.BlockSpec((1, d, tn), lambda l, j: (l, 0, j)),
                  pl.BlockSpec((1, 1, tn), lambda l, j: (l, 0, j))],
        out_specs=pl.BlockSpec((1, bc, tn), lambda l, j: (l, 0, j)),
        out_shape=jax.ShapeDtypeStruct((depth, bc, n), F32),
        compiler_params=_cparams(("arbitrary", "arbitrary")),
        name="ada",
    )(c_all, w_ada, b_ada.reshape(depth, 1, n))


def _inproj_kernel(x_ref, sc_ref, sh_ref, g_ref, w_ref, wg_ref, o_ref, og_ref, hn_ref):
    @pl.when(pl.program_id(2) == 0)
    def _():
        x = x_ref[...]
        ms = jnp.mean(x * x, axis=-1, keepdims=True)
        h = x * lax.rsqrt(ms + EPS) * g_ref[...]
        h = h * (1.0 + sc_ref[...]) + sh_ref[...]
        hb = h.reshape(hn_ref.shape).astype(BF16)
        hn_ref[...] = hb
        og_ref[...] = jnp.dot(hb, wg_ref[...], preferred_element_type=F32)

    o_ref[...] = jnp.dot(hn_ref[...], w_ref[...], preferred_element_type=F32)


def _row_tiling(b, t, max_rows):
    if t >= max_rows:
        assert t % max_rows == 0
        return 1, max_rows
    bb = min(b, max_rows // t)
    assert b % bb == 0 and t % SUBLANES == 0
    return bb, t


def _inproj(x, scale, shift, g_norm, w_main, n, w_gate, max_rows):
    b, t, d = x.shape
    ng = w_gate.shape[1]
    bb, tt = _row_tiling(b, t, max_rows)
    rows = bb * tt
    tn = min(512, n)
    nt = t // tt
    return pl.pallas_call(
        _inproj_kernel,
        grid=(b // bb, nt, n // tn),
        in_specs=[pl.BlockSpec((bb, tt, d), lambda i, k, j: (i, k, 0)),
                  pl.BlockSpec((bb, 1, d), lambda i, k, j: (i, 0, 0)),
                  pl.BlockSpec((bb, 1, d), lambda i, k, j: (i, 0, 0)),
                  pl.BlockSpec((1, 1, d), lambda i, k, j: (0, 0, 0)),
                  pl.BlockSpec((d, tn), lambda i, k, j: (0, j)),
                  pl.BlockSpec((d, ng), lambda i, k, j: (0, 0))],
        out_specs=[pl.BlockSpec((rows, tn), lambda i, k, j: (i * nt + k, j)),
                   pl.BlockSpec((rows, ng), lambda i, k, j: (i * nt + k, 0))],
        out_shape=[jax.ShapeDtypeStruct((b * t, n), F32), jax.ShapeDtypeStruct((b * t, ng), F32)],
        scratch_shapes=[pltpu.VMEM((rows, d), BF16)],
        compiler_params=_cparams(("arbitrary", "arbitrary", "arbitrary")),
        name="inproj",
    )(x, scale, shift, g_norm.reshape(1, 1, d), w_main, w_gate)


def _grpa_kernel(u_ref, v_ref, lg_ref, lb_ref, w_ref, bias_ref, ya_ref, *maybe_v_out):
    rows, da = u_ref.shape
    nh = w_ref.shape[0]
    g = jax.nn.gelu(v_ref[...])
    mu = jnp.mean(g, axis=-1, keepdims=True)
    gc = g - mu
    var = jnp.mean(gc * gc, axis=-1, keepdims=True)
    v = gc * lax.rsqrt(var + EPS) * lg_ref[...] + lb_ref[...]
    if maybe_v_out:
        maybe_v_out[0][...] = v
    vb = v.astype(BF16)
    for c in range(rows // TILE):
        r0 = c * TILE
        for h in range(nh):
            c0 = h * LANES
            z = jnp.dot(w_ref[h], vb[r0:r0 + TILE, c0:c0 + LANES], preferred_element_type=F32)
            u = jax.nn.gelu(u_ref[r0:r0 + TILE, c0:c0 + LANES])
            y = u * (z + bias_ref[:, c0:c0 + LANES])
            ya_ref[r0:r0 + TILE, c0:c0 + LANES] = y.astype(BF16)


def _grpa(proj, ln_g, ln_b, w_mix, bias_full, da, want_v):
    n_rows = proj.shape[0]
    rows = min(512, n_rows)
    nh = w_mix.shape[0]
    out_specs = [pl.BlockSpec((rows, da), lambda i: (i, 0))]
    out_shape = [jax.ShapeDtypeStruct((n_rows, da), BF16)]
    if want_v:
        out_specs.append(pl.BlockSpec((rows, da), lambda i: (i, 0)))
        out_shape.append(jax.ShapeDtypeStruct((n_rows, da), F32))
    return pl.pallas_call(
        _grpa_kernel,
        grid=(n_rows // rows,),
        in_specs=[pl.BlockSpec((rows, da), lambda i: (i, 0)),
                  pl.BlockSpec((rows, da), lambda i: (i, 1)),
                  pl.BlockSpec((1, da), lambda i: (0, 0)),
                  pl.BlockSpec((1, da), lambda i: (0, 0)),
                  pl.BlockSpec((nh, TILE, TILE), lambda i: (0, 0, 0)),
                  pl.BlockSpec((TILE, da), lambda i: (0, 0))],
        out_specs=out_specs,
        out_shape=out_shape,
        compiler_params=_cparams(("arbitrary",)),
        name="grpa",
    )(proj, proj, ln_g.reshape(1, da), ln_b.reshape(1, da), w_mix, bias_full)


def _group_last(x, L):
    if L == TILE:
        return jnp.broadcast_to(x[TILE - 1:TILE, :], x.shape)
    g = TILE // L
    x3 = x.reshape(g, L, x.shape[1])
    return jnp.broadcast_to(x3[:, L - 1:L, :], x3.shape).reshape(x.shape)


def _mlstm_kernel(*refs, L, carry, nh, k_scale):
    if carry:
        (xm_ref, zo_ref, g_ref, wconv_ref, bconv_ref, wq_ref, wk_ref, wkt_ref, wv_ref, bi_ref, bf_ref, gmh_ref,
         yb_ref, c_ref, n_ref, m_ref, prev_scr) = refs
        c_in_ref = c_ref
        first = pl.program_id(1) == 0

        @pl.when(first)
        def _():
            c_ref[...] = jnp.zeros(c_ref.shape, F32)
            n_ref[...] = jnp.zeros(n_ref.shape, F32)
            m_ref[...] = jnp.zeros(m_ref.shape, F32)
            prev_scr[...] = jnp.zeros(prev_scr.shape, F32)

        prev = prev_scr[...]
        n0_rows = jnp.broadcast_to(n_ref[0], (TILE, n_ref.shape[2]))
        m0 = jnp.broadcast_to(m_ref[0], (TILE, LANES))
    else:
        (xm_ref, zo_ref, g_ref, prev_ref, c_layer_ref, n0_ref, m0_ref,
         wconv_ref, bconv_ref, wq_ref, wk_ref, wkt_ref, wv_ref, bi_ref, bf_ref, gmh_ref,
         yb_ref, c_ref, n_ref, m_ref) = refs
        c_in_ref = c_layer_ref.at[0]
        prev = prev_ref[...]
        n0_rows = n0_ref[...]
        m0 = m0_ref[...]

    groups = TILE // L
    x = xm_ref[...]
    row_in_group = lax.broadcasted_iota(jnp.int32, (TILE, 1), 0) % L

    acc = bconv_ref[...] + wconv_ref[CONV_W - 1:CONV_W, :] * x
    for j in range(1, CONV_W):
        from_prev = pltpu.roll(prev, (j - L) % TILE, 0)
        shifted = jnp.where(row_in_group < j, from_prev, pltpu.roll(x, j, 0))
        acc = acc + wconv_ref[CONV_W - 1 - j:CONV_W - j, :] * shifted
    xc = acc * jax.nn.sigmoid(acc)
    if carry:
        prev_scr[...] = x

    gates = g_ref[...]
    gi = gates[:, :LANES] + bi_ref[...]
    gf = gates[:, LANES:] + bf_ref[...]
    lf = jnp.minimum(gf, 0.0) - jnp.log1p(jnp.exp(-jnp.abs(gf)))

    r_i = lax.broadcasted_iota(jnp.int32, (TILE, TILE), 0)
    c_i = lax.broadcasted_iota(jnp.int32, (TILE, TILE), 1)
    mask = (c_i <= r_i) & ((c_i // L) == (r_i // L)) if L != TILE else (c_i <= r_i)
    mask_b = jnp.where(mask, 1.0, 0.0).astype(BF16)
    same_group = ((c_i // L) == (r_i // L)) if L != TILE else (c_i >= 0)
    ones_b = jnp.ones((TILE, LANES), BF16)
    lf1, lf2, lf3 = _split3(lf)
    cum = (jnp.dot(mask_b, lf1, preferred_element_type=F32) + jnp.dot(mask_b, lf2, preferred_element_type=F32)
           + jnp.dot(mask_b, lf3, preferred_element_type=F32))
    cum_t = cum.T
    gi_t = gi.T
    log_inter = cum + m0

    def d_matrix(h):
        d = cum[:, h:h + 1] - cum_t[h:h + 1, :] + gi_t[h:h + 1, :]
        return jnp.where(mask, d, NEG_INF)

    run = gi - cum
    step = 1
    while step < L:
        run = jnp.maximum(run, jnp.where(row_in_group >= step, pltpu.roll(run, step, 0), NEG_INF))
        step *= 2
    mt = jnp.maximum(log_inter, cum + run)

    m_last = _group_last(mt, L)
    w_inter = jnp.exp(log_inter - mt)
    w_last = jnp.exp(_group_last(cum, L) - cum + gi - m_last)
    w_last_t = w_last.T
    inv_floor = jnp.exp(-mt)

    heads = range(nh)
    cols = [slice(h * LANES, (h + 1) * LANES) for h in heads]
    nt_dims = (((1,), (1,)), ((), ()))

    def mm(a, b):
        return jnp.dot(a, b, preferred_element_type=F32)

    xcb = [xc[:, cols[h]].astype(BF16) for h in heads]
    q = [mm(xcb[h], wq_ref[h]) for h in heads]
    k = [mm(xcb[h], wk_ref[h]) * k_scale for h in heads]
    k_t = [lax.dot_general(wkt_ref[h], xcb[h], nt_dims, preferred_element_type=F32) * k_scale for h in heads]
    vb = [mm(x[:, cols[h]].astype(BF16), wv_ref[h]).astype(BF16) for h in heads]
    qb = [q[h].astype(BF16) for h in heads]

    w_intra = [jnp.exp(d_matrix(h) - mt[:, h:h + 1]) for h in heads]
    s = [mm(qb[h], k_t[h].astype(BF16)) * w_intra[h] for h in heads]
    if carry:
        qc = [mm(qb[h], c_in_ref[0, h].astype(BF16)) for h in heads]
    else:
        qc = [jnp.concatenate([mm(q[h][g * L:(g + 1) * L, :].astype(BF16), c_in_ref[g, h].astype(BF16))
                               for g in range(groups)], axis=0) for h in heads]

    s_hi = [s[h].astype(BF16) for h in heads]
    s_lo = [(s[h] - s_hi[h].astype(F32)).astype(BF16) for h in heads]
    sv = [mm(s_hi[h], jnp.concatenate([vb[h], ones_b], axis=1)) for h in heads]
    s_sum = [sv[h][:, LANES:] + mm(s_lo[h], ones_b) for h in heads]
    wi_rep = [jnp.broadcast_to(w_inter[:, h:h + 1], (TILE, LANES)) for h in heads]
    n0_b = [n0_rows[:, cols[h]].astype(BF16) for h in heads]
    if carry:
        qn = [lax.dot_general(qb[h], n0_b[h], nt_dims, preferred_element_type=F32) for h in heads]
    else:
        qn = [jnp.sum(qb[h].astype(F32) * n0_b[h].astype(F32), axis=1, keepdims=True) for h in heads]
    den = [wi_rep[h] * qn[h] + s_sum[h] for h in heads]
    hh = [(wi_rep[h] * qc[h] + sv[h][:, :LANES]) / jnp.maximum(jnp.abs(den[h]), inv_floor[:, h:h + 1])
          for h in heads]
    h_sq = [mm((hh[h] * hh[h]).astype(BF16), ones_b) for h in heads]
    for h in heads:
        hn = hh[h] * lax.rsqrt(h_sq[h] * (1.0 / LANES) + EPS) * gmh_ref[:, cols[h]]
        yb_ref[:, cols[h]] = (jax.nn.sigmoid(zo_ref[:, cols[h]]) * hn).astype(BF16)

    for h in heads:
        w_last_row = w_last_t[h:h + 1, :]
        kw_t = k_t[h] * w_last_row
        wl_rows = jnp.where(same_group, w_last_row, 0.0).astype(BF16)
        n_rows = wi_rep[h] * n0_rows[:, cols[h]] + mm(wl_rows, k[h].astype(BF16))
        if carry:
            decay = jnp.broadcast_to(w_inter[TILE - 1:TILE, h:h + 1], (LANES, LANES))
            c_ref[0, h] = decay * c_in_ref[0, h] + mm(kw_t.astype(BF16), vb[h])
            n_ref[0, :, cols[h]] = n_rows[TILE - 1:TILE, :]
        else:
            n_ref[:, cols[h]] = n_rows
            for g in range(groups):
                kw_g = jnp.where((c_i // L) == g, kw_t, 0.0).astype(BF16)
                last = g * L + L - 1
                decay = jnp.broadcast_to(w_inter[last:last + 1, h:h + 1], (LANES, LANES))
                c_ref[g, h] = decay * c_in_ref[g, h] + mm(kw_g, vb[h])

    if carry:
        m_ref[0] = mt[TILE - 1:TILE, :]
    else:
        m_ref[...] = mt


def _mlstm_weights(p):
    return [p["w_conv"], p["b_conv"], p["w_q"], p["w_k"], p["w_kt"], p["w_v"], p["b_i"], p["b_f"], p["g_mh"]]


def _mlstm_weight_specs(nh, db, nidx):
    z2 = (lambda i, c: (0, 0)) if nidx == 2 else (lambda i: (0, 0))
    z3 = (lambda i, c: (0, 0, 0)) if nidx == 2 else (lambda i: (0, 0, 0))
    hw = pl.BlockSpec((nh, LANES, LANES), z3)
    return [pl.BlockSpec((CONV_W, db), z2), pl.BlockSpec((1, db), z2), hw, hw, hw, hw,
            pl.BlockSpec((1, LANES), z2), pl.BlockSpec((1, LANES), z2), pl.BlockSpec((1, db), z2)]


def _mlstm_prompt(proj, gates, p, b, t, db, nh, xm_col, zo_col):
    nt = t // TILE
    kern = functools.partial(_mlstm_kernel, L=TILE, carry=True, nh=nh, k_scale=p["k_scale"])
    return pl.pallas_call(
        kern,
        grid=(b, nt),
        in_specs=[pl.BlockSpec((TILE, db), lambda i, c: (i * nt + c, xm_col)),
                  pl.BlockSpec((TILE, db), lambda i, c: (i * nt + c, zo_col)),
                  pl.BlockSpec((TILE, 2 * LANES), lambda i, c: (i * nt + c, 0))] + _mlstm_weight_specs(nh, db, 2),
        out_specs=[pl.BlockSpec((TILE, db), lambda i, c: (i * nt + c, 0)),
                   pl.BlockSpec((1, nh, LANES, LANES), lambda i, c: (i, 0, 0, 0)),
                   pl.BlockSpec((1, 1, db), lambda i, c: (i, 0, 0)),
                   pl.BlockSpec((1, 1, LANES), lambda i, c: (i, 0, 0))],
        out_shape=[jax.ShapeDtypeStruct((b * t, db), BF16),
                   jax.ShapeDtypeStruct((b, nh, LANES, LANES), F32),
                   jax.ShapeDtypeStruct((b, 1, db), F32),
                   jax.ShapeDtypeStruct((b, 1, LANES), F32)],
        scratch_shapes=[pltpu.VMEM((TILE, db), F32)],
        compiler_params=_cparams(("arbitrary", "arbitrary")),
        name="mlstm_prompt",
    )(proj, proj, gates, *_mlstm_weights(p))


def _mlstm_sample(proj, gates, prev_rows, c0, layer, n0_rows, m0_rows, p, L, db, nh, xm_col, zo_col):
    n_rows = proj.shape[0]
    groups = TILE // L
    kern = functools.partial(_mlstm_kernel, L=L, carry=False, nh=nh, k_scale=p["k_scale"])
    row_spec = pl.BlockSpec((TILE, db), lambda i: (i, 0))
    return pl.pallas_call(
        kern,
        grid=(n_rows // TILE,),
        in_specs=[pl.BlockSpec((TILE, db), lambda i: (i, xm_col)),
                  pl.BlockSpec((TILE, db), lambda i: (i, zo_col)),
                  pl.BlockSpec((TILE, 2 * LANES), lambda i: (i, 0)),
                  row_spec,
                  pl.BlockSpec((1, groups, nh, LANES, LANES), lambda i: (layer, i, 0, 0, 0)),
                  row_spec,
                  pl.BlockSpec((TILE, LANES), lambda i: (i, 0))] + _mlstm_weight_specs(nh, db, 1),
        out_specs=[row_spec,
                   pl.BlockSpec((groups, nh, LANES, LANES), lambda i: (i, 0, 0, 0)),
                   row_spec,
                   pl.BlockSpec((TILE, LANES), lambda i: (i, 0))],
        out_shape=[jax.ShapeDtypeStruct((n_rows, db), BF16),
                   jax.ShapeDtypeStruct(c0.shape[1:], F32),
                   jax.ShapeDtypeStruct((n_rows, db), F32),
                   jax.ShapeDtypeStruct((n_rows, LANES), F32)],
        compiler_params=_cparams(("arbitrary",)),
        name="mlstm_sample",
    )(proj, proj, gates, prev_rows, c0, n0_rows, m0_rows, *_mlstm_weights(p))


def _route(lg, n_groups, per_group):
    lane_i = lax.broadcasted_iota(jnp.int32, lg.shape, 1)
    lane = lane_i.astype(F32)
    far = float(LANES)
    is_g = lane_i < n_groups
    lgg = jnp.where(is_g, lg, NEG_INF)
    mg = jnp.max(lgg, axis=1, keepdims=True)
    g_sel = jnp.min(jnp.where(lgg == mg, lane, far), axis=1, keepdims=True)
    p_g = 1.0 / jnp.sum(jnp.where(is_g, jnp.exp(lgg - mg), 0.0), axis=1, keepdims=True)
    lo = n_groups + g_sel * per_group
    sel = (lane >= lo) & (lane < lo + per_group)
    le = jnp.where(sel, lg, NEG_INF)
    m1 = jnp.max(le, axis=1, keepdims=True)
    i1 = jnp.min(jnp.where(le == m1, lane, far), axis=1, keepdims=True)
    le2 = jnp.where(lane == i1, NEG_INF, le)
    m2 = jnp.max(le2, axis=1, keepdims=True)
    i2 = jnp.min(jnp.where(le2 == m2, lane, far), axis=1, keepdims=True)
    ratio = jnp.exp(m2 - m1)
    gate1 = p_g / (1.0 + ratio)
    gate2 = p_g * ratio / (1.0 + ratio)
    out = jnp.where(lane_i == 0, i1 - n_groups, 0.0)
    out = jnp.where(lane_i == 1, i2 - n_groups, out)
    out = jnp.where(lane_i == 2, gate1, out)
    return jnp.where(lane_i == 3, gate2, out)


def _outproj_kernel(x_ref, ya_ref, yb_ref, w_ref, g1_ref, sc_ref, sh_ref, gn_ref, wr_ref, br_ref,
                    xn_ref, h2_ref, rt_ref, *, n_groups, per_group):
    da = ya_ref.shape[1]
    mix = (jnp.dot(ya_ref[...], w_ref[:da, :], preferred_element_type=F32)
           + jnp.dot(yb_ref[...], w_ref[da:, :], preferred_element_type=F32))
    x = x_ref[...] + g1_ref[...] * mix.reshape(x_ref.shape)
    xn_ref[...] = x
    ms = jnp.mean(x * x, axis=-1, keepdims=True)
    h = x * lax.rsqrt(ms + EPS) * gn_ref[...]
    h = (h * (1.0 + sc_ref[...]) + sh_ref[...]).reshape(h2_ref.shape)
    h2_ref[...] = h
    lg = jnp.dot(h.astype(BF16), wr_ref[...], preferred_element_type=F32) + br_ref[...]
    rt_ref[...] = _route(lg, n_groups, per_group)


def _outproj(x, ya, yb, w_out, gate, scale, shift, g_norm, w_router, b_router, n_groups, per_group, max_rows):
    b, t, d = x.shape
    bb, tt = _row_tiling(b, t, max_rows)
    rows = bb * tt
    nt = t // tt
    da = ya.shape[1]
    xspec = pl.BlockSpec((bb, tt, d), lambda i, k: (i, k, 0))
    mspec = pl.BlockSpec((bb, 1, d), lambda i, k: (i, 0, 0))
    kern = functools.partial(_outproj_kernel, n_groups=n_groups, per_group=per_group)
    return pl.pallas_call(
        kern,
        grid=(b // bb, nt),
        in_specs=[xspec,
                  pl.BlockSpec((rows, da), lambda i, k: (i * nt + k, 0)),
                  pl.BlockSpec((rows, ya.shape[1]), lambda i, k: (i * nt + k, 0)),
                  pl.BlockSpec(w_out.shape, lambda i, k: (0, 0), pipeline_mode=pl.Buffered(1)),
                  mspec, mspec, mspec,
                  pl.BlockSpec((1, 1, d), lambda i, k: (0, 0, 0)),
                  pl.BlockSpec(w_router.shape, lambda i, k: (0, 0), pipeline_mode=pl.Buffered(1)),
                  pl.BlockSpec((1, LANES), lambda i, k: (0, 0))],
        out_specs=[xspec,
                   pl.BlockSpec((rows, d), lambda i, k: (i * nt + k, 0)),
                   pl.BlockSpec((rows, LANES), lambda i, k: (i * nt + k, 0))],
        out_shape=[jax.ShapeDtypeStruct(x.shape, F32),
                   jax.ShapeDtypeStruct((b * t, d), F32),
                   jax.ShapeDtypeStruct((b * t, LANES), F32)],
        compiler_params=_cparams(("arbitrary", "arbitrary")),
        name="outproj",
    )(x, ya, yb, w_out, gate, scale, shift, g_norm.reshape(1, 1, d), w_router, b_router)


def _gather_kernel(idx_ref, used_ref, src_ref, out_ref, buf, sem, *, chunk):
    i = pl.program_id(0)

    def in_use(step):
        return step * chunk < used_ref[0]

    def issue_step(step, slot):
        def issue(r, _):
            t = idx_ref[step * chunk + r]
            pltpu.make_async_copy(src_ref.at[pl.ds(t, 1)], buf.at[slot, pl.ds(r, 1)], sem.at[slot]).start()
            return 0

        lax.fori_loop(0, chunk, issue, 0, unroll=8)

    @pl.when(i == 0)
    def _():
        issue_step(0, 0)

    @pl.when((i + 1 < pl.num_programs(0)) & in_use(i + 1))
    def _():
        issue_step(i + 1, (i + 1) % 2)

    @pl.when(in_use(i))
    def _():
        slot = i % 2
        pltpu.make_async_copy(buf.at[slot], buf.at[slot], sem.at[slot]).wait()
        out_ref[...] = buf[slot].astype(out_ref.dtype)

    @pl.when(jnp.logical_not(in_use(i)))
    def _():
        out_ref[...] = jnp.zeros(out_ref.shape, out_ref.dtype)


def _gather_rows(idx, n_used, src, out_dtype):
    n_out = idx.shape[0]
    chunk = math.gcd(GATHER_CHUNK, n_out)
    d = src.shape[1]
    return pl.pallas_call(
        functools.partial(_gather_kernel, chunk=chunk),
        grid_spec=pltpu.PrefetchScalarGridSpec(
            num_scalar_prefetch=2,
            grid=(n_out // chunk,),
            in_specs=[pl.BlockSpec(memory_space=pl.ANY)],
            out_specs=pl.BlockSpec((chunk, d), lambda i, idx, used: (i, 0)),
            scratch_shapes=[pltpu.VMEM((2, chunk, d), src.dtype), pltpu.SemaphoreType.DMA((2,))]),
        out_shape=jax.ShapeDtypeStruct((n_out, d), out_dtype),
        compiler_params=_cparams(("arbitrary",)),
        name="gather_rows",
    )(idx, n_used, src)


def _expert_up_kernel(ie_ref, ij_ref, ib_ref, ioj_ref, inew_ref, islot_ref, ine_ref, inj_ref, ihas_ref, n_ref,
                      x_ref, wg_hbm, wu_hbm, o_ref, stage, wgb_scr, wub_scr, sem, *, layer):
    i = pl.program_id(0)
    tn = wgb_scr.shape[1]

    def weight_copies(e, j, slot):
        col = pl.multiple_of(j * tn, tn)
        return (pltpu.make_async_copy(wg_hbm.at[layer, e, :, pl.ds(col, tn)], stage.at[slot, 0], sem.at[slot, 0]),
                pltpu.make_async_copy(wu_hbm.at[layer, e, :, pl.ds(col, tn)], stage.at[slot, 1], sem.at[slot, 1]))

    @pl.when(i == 0)
    def _():
        for cp in weight_copies(ie_ref[0], ij_ref[0], islot_ref[0]):
            cp.start()

    @pl.when(i >= n_ref[0])
    def _():
        o_ref[...] = jnp.zeros(o_ref.shape, o_ref.dtype)

    @pl.when(i < n_ref[0])
    def _():
        @pl.when(inew_ref[i] == 1)
        def _():
            slot = islot_ref[i]

            @pl.when(ihas_ref[i] == 1)
            def _():
                for cp in weight_copies(ine_ref[i], inj_ref[i], 1 - slot):
                    cp.start()

            for cp in weight_copies(ie_ref[i], ij_ref[i], slot):
                cp.wait()
            a = u = None
            for k0 in range(0, x_ref.shape[1], CAST_CHUNK):
                ks = slice(k0, k0 + CAST_CHUNK)
                wg_k = stage[slot, 0, ks, :].astype(BF16)
                wu_k = stage[slot, 1, ks, :].astype(BF16)
                wgb_scr[ks, :] = wg_k
                wub_scr[ks, :] = wu_k
                a_k = jnp.dot(x_ref[:, ks], wg_k, preferred_element_type=F32)
                u_k = jnp.dot(x_ref[:, ks], wu_k, preferred_element_type=F32)
                a = a_k if a is None else a + a_k
                u = u_k if u is None else u + u_k
            o_ref[...] = (a * jax.nn.sigmoid(a) * u).astype(BF16)

        @pl.when(inew_ref[i] != 1)
        def _():
            xb = x_ref[...]
            a = jnp.dot(xb, wgb_scr[...], preferred_element_type=F32)
            u = jnp.dot(xb, wub_scr[...], preferred_element_type=F32)
            o_ref[...] = (a * jax.nn.sigmoid(a) * u).astype(BF16)


def _expert_up(plan, x_buf, w_gate, w_up, layer):
    d, de = w_gate.shape[2], w_gate.shape[3]
    n_rows = x_buf.shape[0]
    tn = de // EXPERT_COL_TILES
    bm = MOE_BLOCK_ROWS
    up = plan["up"]
    n_items = up["e"].shape[0]
    names = ("e", "j", "b", "oj", "new", "slot", "next_e", "next_j", "has_next", "n")
    return pl.pallas_call(
        functools.partial(_expert_up_kernel, layer=layer),
        grid_spec=pltpu.PrefetchScalarGridSpec(
            num_scalar_prefetch=len(names),
            grid=(n_items,),
            in_specs=[pl.BlockSpec((bm, d), lambda i, *s: (s[2][i], 0)),
                      pl.BlockSpec(memory_space=pl.ANY), pl.BlockSpec(memory_space=pl.ANY)],
            out_specs=pl.BlockSpec((bm, tn), lambda i, *s: (s[2][i], s[3][i])),
            scratch_shapes=[pltpu.VMEM((2, 2, d, tn), F32), pltpu.VMEM((d, tn), BF16), pltpu.VMEM((d, tn), BF16),
                            pltpu.SemaphoreType.DMA((2, 2))]),
        out_shape=jax.ShapeDtypeStruct((n_rows, de), BF16),
        compiler_params=_cparams(("arbitrary",)),
        name="expert_up",
    )(*[up[k] for k in names], x_buf, w_gate, w_up)


def _expert_down_kernel(be_ref, bnew_ref, bslot_ref, bne_ref, bhas_ref, n_ref, h_ref, wd_hbm, o_ref,
                        stage, wdb_scr, sem, *, layer):
    i = pl.program_id(0)

    def weight_copy(e, slot):
        return pltpu.make_async_copy(wd_hbm.at[layer, e], stage.at[slot], sem.at[slot])

    @pl.when(i == 0)
    def _():
        weight_copy(be_ref[0], bslot_ref[0]).start()

    @pl.when(i >= n_ref[0])
    def _():
        o_ref[...] = jnp.zeros(o_ref.shape, o_ref.dtype)

    @pl.when(i < n_ref[0])
    def _():
        @pl.when(bnew_ref[i] == 1)
        def _():
            slot = bslot_ref[i]

            @pl.when(bhas_ref[i] == 1)
            def _():
                weight_copy(bne_ref[i], 1 - slot).start()

            weight_copy(be_ref[i], slot).wait()
            y = None
            for k0 in range(0, h_ref.shape[1], CAST_CHUNK):
                ks = slice(k0, k0 + CAST_CHUNK)
                wd_k = stage[slot, ks, :].astype(BF16)
                wdb_scr[ks, :] = wd_k
                y_k = jnp.dot(h_ref[:, ks], wd_k, preferred_element_type=F32)
                y = y_k if y is None else y + y_k
            o_ref[...] = y

        @pl.when(bnew_ref[i] != 1)
        def _():
            o_ref[...] = jnp.dot(h_ref[...], wdb_scr[...], preferred_element_type=F32)


def _expert_down(plan, h_buf, w_down, layer):
    n_rows, de = h_buf.shape
    d = w_down.shape[3]
    bm = MOE_BLOCK_ROWS
    down = plan["down"]
    n_blocks = down["e"].shape[0]
    names = ("e", "new", "slot", "next_e", "has_next", "n")
    return pl.pallas_call(
        functools.partial(_expert_down_kernel, layer=layer),
        grid_spec=pltpu.PrefetchScalarGridSpec(
            num_scalar_prefetch=len(names),
            grid=(n_blocks,),
            in_specs=[pl.BlockSpec((bm, de), lambda i, *s: (i, 0)),
                      pl.BlockSpec(memory_space=pl.ANY)],
            out_specs=pl.BlockSpec((bm, d), lambda i, *s: (i, 0)),
            scratch_shapes=[pltpu.VMEM((2, de, d), F32), pltpu.VMEM((de, d), BF16), pltpu.SemaphoreType.DMA((2,))]),
        out_shape=jax.ShapeDtypeStruct((n_rows, d), F32),
        compiler_params=_cparams(("arbitrary",)),
        name="expert_down",
    )(*[down[k] for k in names], h_buf, w_down)


def _moe_plan(expert, n_experts):
    bm = MOE_BLOCK_ROWS
    n_tok = expert.shape[0]
    n_asg = n_tok * TOP_K
    e_flat = expert.reshape(-1)
    e_ids = jnp.arange(n_experts, dtype=jnp.int32)

    def take(table, idx):
        return jnp.sum(jnp.where(idx[:, None] == e_ids[None, :], table[None, :], 0), axis=1)

    hit = e_flat[:, None] == e_ids[None, :]
    csum = jnp.cumsum(hit.astype(jnp.int32), axis=0)
    counts = csum[-1]
    rank = jnp.sum(jnp.where(hit, csum, 0), axis=1) - 1
    nblk = (counts + bm - 1) // bm
    bend = jnp.cumsum(nblk)
    bstart = bend - nblk
    dest = take(bstart, e_flat) * bm + rank
    n_blocks = (n_asg + n_experts * (bm - 1) + bm - 1) // bm
    n_rows = -(-n_blocks * bm // GATHER_CHUNK) * GATHER_CHUNK
    tok = jnp.arange(n_asg, dtype=jnp.int32) // TOP_K
    row_src = (jnp.arange(n_rows, dtype=jnp.int32) % n_tok).at[dest].set(tok, unique_indices=True)
    used = bend[-1]

    def expert_of_block(b):
        return jnp.minimum(jnp.sum((bend[None, :] <= b[:, None]).astype(jnp.int32), axis=1), n_experts - 1)

    blk = jnp.arange(n_blocks, dtype=jnp.int32)
    blk_e = expert_of_block(jnp.minimum(blk, used - 1))
    blk_new = (blk == take(bstart, blk_e)).astype(jnp.int32)
    nct = EXPERT_COL_TILES
    n_items = nct * n_blocks
    idx = jnp.arange(n_items, dtype=jnp.int32)
    it = jnp.minimum(idx, nct * used - 1)
    item_e = expert_of_block(it // nct)
    item_bstart = take(bstart, item_e)
    item_nblk = take(nblk, item_e)
    off = it - nct * item_bstart
    item_j = (off // item_nblk).astype(jnp.int32)
    item_b = (item_bstart + off % item_nblk).astype(jnp.int32)
    item_new = (item_b == item_bstart).astype(jnp.int32)
    spare = idx - nct * used
    is_spare = spare >= 0
    item_b = jnp.where(is_spare, used + spare // nct, item_b)
    item_oj = jnp.where(is_spare, spare % nct, item_j)
    e_rank = jnp.cumsum((nblk > 0).astype(jnp.int32)) - 1
    e_after = expert_of_block(jnp.minimum(bend, used - 1))
    more_after = (bend < used).astype(jnp.int32)
    down = dict(e=blk_e, new=blk_new, slot=take(e_rank, blk_e) % 2, next_e=take(e_after, blk_e),
                has_next=take(more_after, blk_e), n=used.reshape(1).astype(jnp.int32))
    not_last = item_j < nct - 1
    up = dict(e=item_e, j=item_j, b=item_b, oj=item_oj, new=item_new,
              slot=(nct * take(e_rank, item_e) + item_j) % 2,
              next_e=jnp.where(not_last, item_e, take(e_after, item_e)), next_j=jnp.where(not_last, item_j + 1, 0),
              has_next=jnp.where(not_last, 1, take(more_after, item_e)),
              n=(nct * used).reshape(1).astype(jnp.int32))
    return dict(dest=dest, row_src=row_src, up=up, down=down,
                used_rows=(used * bm).reshape(1).astype(jnp.int32))


def _combine_kernel(dst_ref, x_ref, rt_ref, g2_ref, gf_ref, y_hbm, o_ref, buf, sem, *, final_norm, tok0, n_all):
    bb, tt, _ = x_ref.shape
    rows = bb * tt
    nt = pl.num_programs(1)
    lin = pl.program_id(0) * nt + pl.program_id(1)

    def issue_step(step, slot):
        base = tok0 + step * rows

        def issue(r, _):
            for k in range(TOP_K):
                row = dst_ref[k * n_all + base + r]
                pltpu.make_async_copy(y_hbm.at[pl.ds(row, 1)], buf.at[slot, k, pl.ds(r, 1)], sem.at[slot]).start()
            return 0

        lax.fori_loop(0, rows, issue, 0, unroll=4)

    @pl.when(lin == 0)
    def _():
        issue_step(0, 0)

    @pl.when(lin + 1 < pl.num_programs(0) * nt)
    def _():
        issue_step(lin + 1, (lin + 1) % 2)

    slot = lin % 2
    pltpu.make_async_copy(buf.at[slot], buf.at[slot], sem.at[slot]).wait()
    rt = rt_ref[...]
    y = rt[:, 2:3] * buf[slot, 0] + rt[:, 3:4] * buf[slot, 1]
    x = x_ref[...] + g2_ref[...] * y.reshape(x_ref.shape)
    if final_norm:
        ms = jnp.mean(x * x, axis=-1, keepdims=True)
        x = x * lax.rsqrt(ms + EPS) * gf_ref[...]
    o_ref[...] = x


def _combine(x, y_buf, dest_t, tok0, route, gate, g_final, final_norm, max_rows):
    b, t, d = x.shape
    bb, tt = _row_tiling(b, t, max_rows)
    rows = bb * tt
    nt = t // tt
    xspec = pl.BlockSpec((bb, tt, d), lambda i, k, dst: (i, k, 0))
    kern = functools.partial(_combine_kernel, final_norm=final_norm, tok0=tok0, n_all=dest_t.shape[0] // TOP_K)
    return pl.pallas_call(
        kern,
        grid_spec=pltpu.PrefetchScalarGridSpec(
            num_scalar_prefetch=1,
            grid=(b // bb, nt),
            in_specs=[xspec,
                      pl.BlockSpec((rows, LANES), lambda i, k, dst: (i * nt + k, 0)),
                      pl.BlockSpec((bb, 1, d), lambda i, k, dst: (i, 0, 0)),
                      pl.BlockSpec((1, 1, d), lambda i, k, dst: (0, 0, 0)),
                      pl.BlockSpec(memory_space=pl.ANY)],
            out_specs=xspec,
            scratch_shapes=[pltpu.VMEM((2, TOP_K, rows, d), F32), pltpu.SemaphoreType.DMA((2,))]),
        out_shape=jax.ShapeDtypeStruct(x.shape, F32),
        compiler_params=_cparams(("arbitrary", "arbitrary")),
        name="combine",
    )(dest_t, x, route, gate, g_final.reshape(1, 1, d), y_buf)


def _layer_params(l, w):
    d = w["w_in"].shape[1]
    nh = w["w_q"].shape[1]
    da = w["ln_v_g"].shape[1]
    db = w["w_conv"].shape[2]
    n_main = 2 * da + 2 * db
    w_in = w["w_in"][l]
    w_gate = jnp.zeros((d, 2 * LANES), BF16)
    w_gate = w_gate.at[:, :nh].set(w_in[:, n_main:n_main + nh].astype(BF16))
    w_gate = w_gate.at[:, LANES:LANES + nh].set(w_in[:, n_main + nh:n_main + 2 * nh].astype(BF16))
    n_groups = w["w_r1"].shape[2]
    per_group = w["w_r2"].shape[3]
    w_r = jnp.concatenate([w["w_r1"][l], jnp.transpose(w["w_r2"][l], (1, 0, 2)).reshape(d, n_groups * per_group)],
                          axis=1)
    n_r = w_r.shape[1]
    w_r = jnp.pad(w_r, ((0, 0), (0, LANES - n_r)))
    b_r =jnp.pad(jnp.concatenate([w["b_r1"][l], w["b_r2"][l].reshape(-1)]), (0, LANES - n_r)).reshape(1, LANES)
    return dict(
        g_norm1=w["g_norm1"][l], g_norm2=w["g_norm2"][l],
        w_main=w_in.astype(BF16), n_main=n_main, w_gate=w_gate,
        ln_g=w["ln_v_g"][l], ln_b=w["ln_v_b"][l],
        w_conv=w["w_conv"][l], b_conv=w["b_conv"][l].reshape(1, db),
        w_q=w["w_q"][l].astype(BF16), w_k=w["w_k"][l].astype(BF16),
        w_kt=jnp.swapaxes(w["w_k"][l], 1, 2).astype(BF16), w_v=w["w_v"][l].astype(BF16),
        b_i=jnp.pad(w["b_i"][l], (0, LANES - nh)).reshape(1, LANES),
        b_f=jnp.pad(w["b_f"][l], (0, LANES - nh)).reshape(1, LANES),
        g_mh=w["g_mh"][l].reshape(1, db), k_scale=1.0 / math.sqrt(LANES),
        w_out=w["w_out"][l].astype(BF16),
        w_router=w_r.astype(BF16), b_router=b_r,
        n_groups=n_groups, per_group=per_group,
    )


def _mix_weights(w_spatial, b_spatial, L, da):
    nh = w_spatial.shape[0]
    causal = jnp.tril(jnp.ones((L, L), dtype=bool))
    w = jnp.where(causal[None], w_spatial[:, :L, :L], 0.0)
    reps = TILE // L
    if reps > 1:
        eye = jnp.eye(reps, dtype=F32)
        w = jnp.einsum("ab,hts->hatbs", eye, w).reshape(nh, TILE, TILE)
    bias = jnp.tile(jnp.transpose(b_spatial[:, :L]), (reps, 1))
    bias_full = jnp.repeat(bias, da // nh, axis=1)
    return w.astype(BF16), bias_full


def kernel(x_prompt, x_sample, c_prompt, c_sample, state_mlstm_C, state_mlstm_n, state_mlstm_m, state_conv, w_ada, b_ada, g_norm1, g_norm2, w_in, ln_v_g, ln_v_b, w_spatial, b_spatial, w_conv, b_conv, w_q, w_k, w_v, b_i, b_f, g_mh, w_out, w_r1, b_r1, w_r2, b_r2, w_e_gate, w_e_up, w_e_down, g_final):
    weights = dict(w_in=w_in, ln_v_g=ln_v_g, ln_v_b=ln_v_b, w_conv=w_conv, b_conv=b_conv, w_q=w_q, w_k=w_k,
                   w_v=w_v, b_i=b_i, b_f=b_f, g_mh=g_mh, w_out=w_out, w_r1=w_r1, b_r1=b_r1, w_r2=w_r2, b_r2=b_r2,
                   g_norm1=g_norm1, g_norm2=g_norm2)
    depth = w_in.shape[0]
    bp, tp, d = x_prompt.shape
    bs, ts, _ = x_sample.shape
    nh = w_q.shape[1]
    da = ln_v_g.shape[1]
    db = w_conv.shape[2]
    n_mod = w_ada.shape[2] // d
    n_experts = w_e_gate.shape[1]
    n_p, n_s = bp * tp, bs * ts
    assert da == db and da % LANES == 0 and nh * LANES == db and tp % TILE == 0 and TILE % ts == 0
    assert ts >= CONV_W - 1 and ts % SUBLANES == 0 and n_s % TILE == 0
    xm_col, zo_col = 2 * da // db, 2 * da // db + 1

    mod = _ada(jnp.concatenate([c_prompt, c_sample], axis=0), w_ada, b_ada).reshape(depth, bp + bs, n_mod, 1, d)

    xp, xs = x_prompt, x_sample
    outs = {k: [] for k in ("cp", "np", "mp", "bp", "cs", "ns", "ms", "bs", "vs")}
    for l in range(depth):
        p = _layer_params(l, weights)
        mods_p = [mod[l, :bp, i] for i in range(n_mod)]
        mods_s = [mod[l, bp:, i] for i in range(n_mod)]

        proj_p, gates_p = _inproj(xp, mods_p[1], mods_p[0], p["g_norm1"], p["w_main"], p["n_main"], p["w_gate"], 1024)
        w_mix_p, bias_p = _mix_weights(w_spatial[l], b_spatial[l], min(tp, TILE), da)
        (ya_p,) = _grpa(proj_p, p["ln_g"], p["ln_b"], w_mix_p, bias_p, da, False)
        yb_p, c_p, n_pn, m_p = _mlstm_prompt(proj_p, gates_p, p, bp, tp, db, nh, xm_col, zo_col)
        outs["cp"].append(c_p)
        outs["np"].append(n_pn.reshape(bp, nh, LANES))
        outs["mp"].append(m_p[:, 0, :nh])
        outs["bp"].append(proj_p.reshape(bp, tp, -1)[:, tp - (CONV_W - 1):, 2 * da:2 * da + db])

        proj_s, gates_s = _inproj(xs, mods_s[1], mods_s[0], p["g_norm1"], p["w_main"], p["n_main"], p["w_gate"], 512)
        w_mix_s, bias_s = _mix_weights(w_spatial[l], b_spatial[l], min(ts, TILE), da)
        ya_s, v_s = _grpa(proj_s, p["ln_g"], p["ln_b"], w_mix_s, bias_s, da, True)
        prev_rows = jnp.pad(state_conv[l], ((0, 0), (ts - (CONV_W - 1), 0), (0, 0))).reshape(n_s, db)
        n0_rows = jnp.repeat(state_mlstm_n[l].reshape(bs, db), ts, axis=0)
        m0_rows = jnp.pad(jnp.repeat(state_mlstm_m[l], ts, axis=0), ((0, 0), (0, LANES - nh)))
        yb_s, c_s, n_rows_s, m_rows_s = _mlstm_sample(proj_s, gates_s, prev_rows, state_mlstm_C, l, n0_rows, m0_rows,
                                                      p, math.gcd(ts, TILE), db, nh, xm_col, zo_col)
        outs["cs"].append(c_s)
        outs["ns"].append(n_rows_s.reshape(bs, ts, nh, LANES)[:, ts - 1])
        outs["ms"].append(m_rows_s.reshape(bs, ts, LANES)[:, ts - 1, :nh])
        outs["bs"].append(proj_s.reshape(bs, ts, -1)[:, ts - (CONV_W - 1):, 2 * da:2 * da + db])
        outs["vs"].append(v_s.reshape(bs, ts, da))

        rargs = (p["w_router"], p["b_router"], p["n_groups"], p["per_group"])
        xp, h2_p, rt_p = _outproj(xp, ya_p, yb_p, p["w_out"], mods_p[2], mods_p[4], mods_p[3], p["g_norm2"], *rargs, 512)
        xs, h2_s, rt_s = _outproj(xs, ya_s, yb_s, p["w_out"], mods_s[2], mods_s[4], mods_s[3], p["g_norm2"], *rargs, 256)

        route = jnp.concatenate([rt_p, rt_s], axis=0)
        plan = _moe_plan(route[:, :TOP_K].astype(jnp.int32), n_experts)
        x_buf = _gather_rows(plan["row_src"], plan["used_rows"], jnp.concatenate([h2_p, h2_s], axis=0), BF16)
        h_buf = _expert_up(plan, x_buf, w_e_gate, w_e_up, l)
        y_buf = _expert_down(plan, h_buf, w_e_down, l)
        dest_t = jnp.transpose(plan["dest"].reshape(n_p + n_s, TOP_K)).reshape(-1)
        last = l == depth - 1
        xp = _combine(xp, y_buf, dest_t, 0, rt_p, mods_p[5], g_final, last, 256)
        xs = _combine(xs, y_buf, dest_t, n_p, rt_s, mods_s[5], g_final, last, 256)

    st = jnp.stack
    return (xp, xs, st(outs["cp"]), st(outs["np"]), st(outs["mp"]), st(outs["bp"]),
            st(outs["cs"]), st(outs["ns"]), st(outs["ms"]), st(outs["bs"]), st(outs["vs"]))
```

```python
import functools
import math

import jax
import jax.numpy as jnp
from jax import lax
from jax.experimental import pallas as pl
from jax.experimental.pallas import tpu as pltpu

F32 = jnp.float32
BF16 = jnp.bfloat16
EPS = 1e-6
LANES = 128
SUBLANES = 8
TILE = 128
CONV_W = 4
TOP_K = 2
VMEM_LIMIT_BYTES = 56 * 1024 * 1024
MOE_BLOCK_ROWS = 256
GATHER_CHUNK = 512
CAST_CHUNK = 256
EXPERT_COL_TILES = 1
DOWN_STAGE_SLOTS = 3
NEG_INF = float("-inf")


def _cparams(sem):
    return pltpu.CompilerParams(dimension_semantics=sem, vmem_limit_bytes=VMEM_LIMIT_BYTES)


def _split3(x):
    x1 = x.astype(BF16)
    r1 = x - x1.astype(F32)
    x2 = r1.astype(BF16)
    r2 = r1 - x2.astype(F32)
    return x1, x2, r2.astype(BF16)


def _ada_kernel(c_ref, w_ref, b_ref, o_ref):
    c = c_ref[...]
    a = c * jax.nn.sigmoid(c)
    o_ref[0] = jnp.dot(a.astype(BF16), w_ref[0].astype(BF16), preferred_element_type=F32) + b_ref[0]


def _ada(c_all, w_ada, b_ada):
    depth, d, n = w_ada.shape
    bc = c_all.shape[0]
    tn = min(1024, n)
    return pl.pallas_call(
        _ada_kernel,
        grid=(depth, n // tn),
        in_specs=[pl.BlockSpec((bc, d), lambda l, j: (0, 0)),
                  pl.BlockSpec((1, d, tn), lambda l, j: (l, 0, j)),
                  pl.BlockSpec((1, 1, tn), lambda l, j: (l, 0, j))],
        out_specs=pl.BlockSpec((1, bc, tn), lambda l, j: (l, 0, j)),
        out_shape=jax.ShapeDtypeStruct((depth, bc, n), F32),
        compiler_params=_cparams(("arbitrary", "arbitrary")),
        name="ada",
    )(c_all, w_ada, b_ada.reshape(depth, 1, n))


def _inproj_kernel(x_ref, sc_ref, sh_ref, g_ref, w_ref, wg_ref, o_ref, og_ref, hn_ref):
    @pl.when(pl.program_id(2) == 0)
    def _():
        x = x_ref[...]
        ms = jnp.mean(x * x, axis=-1, keepdims=True)
        h = x * lax.rsqrt(ms + EPS) * g_ref[...]
        h = h * (1.0 + sc_ref[...]) + sh_ref[...]
        hb = h.reshape(hn_ref.shape).astype(BF16)
        hn_ref[...] = hb
        og_ref[...] = jnp.dot(hb, wg_ref[...], preferred_element_type=F32)

    o_ref[...] = jnp.dot(hn_ref[...], w_ref[...], preferred_element_type=F32)


def _row_tiling(b, t, max_rows):
    if t >= max_rows:
        assert t % max_rows == 0
        return 1, max_rows
    bb = min(b, max_rows // t)
    assert b % bb == 0 and t % SUBLANES == 0
    return bb, t


def _inproj(x, scale, shift, g_norm, w_main, n, w_gate, max_rows):
    b, t, d = x.shape
    ng = w_gate.shape[1]
    bb, tt = _row_tiling(b, t, max_rows)
    rows = bb * tt
    tn = min(512, n)
    nt = t // tt
    return pl.pallas_call(
        _inproj_kernel,
        grid=(b // bb, nt, n // tn),
        in_specs=[pl.BlockSpec((bb, tt, d), lambda i, k, j: (i, k, 0)),
                  pl.BlockSpec((bb, 1, d), lambda i, k, j: (i, 0, 0)),
                  pl.BlockSpec((bb, 1, d), lambda i, k, j: (i, 0, 0)),
                  pl.BlockSpec((1, 1, d), lambda i, k, j: (0, 0, 0)),
                  pl.BlockSpec((d, tn), lambda i, k, j: (0, j)),
                  pl.BlockSpec((d, ng), lambda i, k, j: (0, 0))],
        out_specs=[pl.BlockSpec((rows, tn), lambda i, k, j: (i * nt + k, j)),
                   pl.BlockSpec((rows, ng), lambda i, k, j: (i * nt + k, 0))],
        out_shape=[jax.ShapeDtypeStruct((b * t, n), F32), jax.ShapeDtypeStruct((b * t, ng), F32)],
        scratch_shapes=[pltpu.VMEM((rows, d), BF16)],
        compiler_params=_cparams(("arbitrary", "arbitrary", "arbitrary")),
        name="inproj",
    )(x, scale, shift, g_norm.reshape(1, 1, d), w_main, w_gate)


def _grpa_kernel(u_ref, v_ref, lg_ref, lb_ref, w_ref, bias_ref, ya_ref, *maybe_v_out):
    rows, da = u_ref.shape
    nh = w_ref.shape[0]
    g = jax.nn.gelu(v_ref[...])
    mu = jnp.mean(g, axis=-1, keepdims=True)
    gc = g - mu
    var = jnp.mean(gc * gc, axis=-1, keepdims=True)
    v = gc * lax.rsqrt(var + EPS) * lg_ref[...] + lb_ref[...]
    if maybe_v_out:
        maybe_v_out[0][...] = v
    vb = v.astype(BF16)
    for c in range(rows // TILE):
        r0 = c * TILE
        for h in range(nh):
            c0 = h * LANES
            z = jnp.dot(w_ref[h], vb[r0:r0 + TILE, c0:c0 + LANES], preferred_element_type=F32)
            u = jax.nn.gelu(u_ref[r0:r0 + TILE, c0:c0 + LANES])
            y = u * (z + bias_ref[:, c0:c0 + LANES])
            ya_ref[r0:r0 + TILE, c0:c0 + LANES] = y.astype(BF16)


def _grpa(proj, ln_g, ln_b, w_mix, bias_full, da, want_v):
    n_rows = proj.shape[0]
    rows = min(512, n_rows)
    nh = w_mix.shape[0]
    out_specs = [pl.BlockSpec((rows, da), lambda i: (i, 0))]
    out_shape = [jax.ShapeDtypeStruct((n_rows, da), BF16)]
    if want_v:
        out_specs.append(pl.BlockSpec((rows, da), lambda i: (i, 0)))
        out_shape.append(jax.ShapeDtypeStruct((n_rows, da), F32))
    return pl.pallas_call(
        _grpa_kernel,
        grid=(n_rows // rows,),
        in_specs=[pl.BlockSpec((rows, da), lambda i: (i, 0)),
                  pl.BlockSpec((rows, da), lambda i: (i, 1)),
                  pl.BlockSpec((1, da), lambda i: (0, 0)),
                  pl.BlockSpec((1, da), lambda i: (0, 0)),
                  pl.BlockSpec((nh, TILE, TILE), lambda i: (0, 0, 0)),
                  pl.BlockSpec((TILE, da), lambda i: (0, 0))],
        out_specs=out_specs,
        out_shape=out_shape,
        compiler_params=_cparams(("arbitrary",)),
        name="grpa",
    )(proj, proj, ln_g.reshape(1, da), ln_b.reshape(1, da), w_mix, bias_full)


def _group_last(x, L):
    if L == TILE:
        return jnp.broadcast_to(x[TILE - 1:TILE, :], x.shape)
    g = TILE // L
    x3 = x.reshape(g, L, x.shape[1])
    return jnp.broadcast_to(x3[:, L - 1:L, :], x3.shape).reshape(x.shape)


def _mlstm_kernel(*refs, L, carry, nh, k_scale):
    if carry:
        (xm_ref, zo_ref, g_ref, wconv_ref, bconv_ref, wq_ref, wk_ref, wkt_ref, wv_ref, bi_ref, bf_ref, gmh_ref,
         yb_ref, c_ref, n_ref, m_ref, prev_scr) = refs
        c_in_ref = c_ref
        first = pl.program_id(1) == 0

        @pl.when(first)
        def _():
            c_ref[...] = jnp.zeros(c_ref.shape, F32)
            n_ref[...] = jnp.zeros(n_ref.shape, F32)
            m_ref[...] = jnp.zeros(m_ref.shape, F32)
            prev_scr[...] = jnp.zeros(prev_scr.shape, F32)

        prev = prev_scr[...]
        n0_rows = jnp.broadcast_to(n_ref[0], (TILE, n_ref.shape[2]))
        m0 = jnp.broadcast_to(m_ref[0], (TILE, LANES))
    else:
        (xm_ref, zo_ref, g_ref, prev_ref, c_layer_ref, n0_ref, m0_ref,
         wconv_ref, bconv_ref, wq_ref, wk_ref, wkt_ref, wv_ref, bi_ref, bf_ref, gmh_ref,
         yb_ref, c_ref, n_ref, m_ref) = refs
        c_in_ref = c_layer_ref.at[0]
        prev = prev_ref[...]
        n0_rows = n0_ref[...]
        m0 = m0_ref[...]

    groups = TILE // L
    x = xm_ref[...]
    row_in_group = lax.broadcasted_iota(jnp.int32, (TILE, 1), 0) % L

    acc = bconv_ref[...] + wconv_ref[CONV_W - 1:CONV_W, :] * x
    for j in range(1, CONV_W):
        from_prev = pltpu.roll(prev, (j - L) % TILE, 0)
        shifted = jnp.where(row_in_group < j, from_prev, pltpu.roll(x, j, 0))
        acc = acc + wconv_ref[CONV_W - 1 - j:CONV_W - j, :] * shifted
    xc = acc * jax.nn.sigmoid(acc)
    if carry:
        prev_scr[...] = x

    gates = g_ref[...]
    gi = gates[:, :LANES] + bi_ref[...]
    gf = gates[:, LANES:] + bf_ref[...]
    lf = jnp.minimum(gf, 0.0) - jnp.log1p(jnp.exp(-jnp.abs(gf)))

    r_i = lax.broadcasted_iota(jnp.int32, (TILE, TILE), 0)
    c_i = lax.broadcasted_iota(jnp.int32, (TILE, TILE), 1)
    mask = (c_i <= r_i) & ((c_i // L) == (r_i // L)) if L != TILE else (c_i <= r_i)
    mask_b = jnp.where(mask, 1.0, 0.0).astype(BF16)
    same_group = ((c_i // L) == (r_i // L)) if L != TILE else (c_i >= 0)
    ones_b = jnp.ones((TILE, LANES), BF16)
    lf1, lf2, lf3 = _split3(lf)
    cum = (jnp.dot(mask_b, lf1, preferred_element_type=F32) + jnp.dot(mask_b, lf2, preferred_element_type=F32)
           + jnp.dot(mask_b, lf3, preferred_element_type=F32))
    cum_t = cum.T
    gi_t = gi.T
    log_inter = cum + m0

    def d_matrix(h):
        d = cum[:, h:h + 1] - cum_t[h:h + 1, :] + gi_t[h:h + 1, :]
        return jnp.where(mask, d, NEG_INF)

    run = gi - cum
    step = 1
    while step < L:
        run = jnp.maximum(run, jnp.where(row_in_group >= step, pltpu.roll(run, step, 0), NEG_INF))
        step *= 2
    mt = jnp.maximum(log_inter, cum + run)

    m_last = _group_last(mt, L)
    w_inter = jnp.exp(log_inter - mt)
    w_last = jnp.exp(_group_last(cum, L) - cum + gi - m_last)
    w_last_t = w_last.T
    inv_floor = jnp.exp(-mt)

    heads = range(nh)
    cols = [slice(h * LANES, (h + 1) * LANES) for h in heads]
    nt_dims = (((1,), (1,)), ((), ()))

    def mm(a, b):
        return jnp.dot(a, b, preferred_element_type=F32)

    xcb = [xc[:, cols[h]].astype(BF16) for h in heads]
    q = [mm(xcb[h], wq_ref[h]) for h in heads]
    k = [mm(xcb[h], wk_ref[h]) * k_scale for h in heads]
    k_t = [lax.dot_general(wkt_ref[h], xcb[h], nt_dims, preferred_element_type=F32) * k_scale for h in heads]
    vb = [mm(x[:, cols[h]].astype(BF16), wv_ref[h]).astype(BF16) for h in heads]
    qb = [q[h].astype(BF16) for h in heads]

    w_intra = [jnp.exp(d_matrix(h) - mt[:, h:h + 1]) for h in heads]
    s = [mm(qb[h], k_t[h].astype(BF16)) * w_intra[h] for h in heads]
    if carry:
        qc = [mm(qb[h], c_in_ref[0, h].astype(BF16)) for h in heads]
    else:
        qc = [jnp.concatenate([mm(q[h][g * L:(g + 1) * L, :].astype(BF16), c_in_ref[g, h].astype(BF16))
                               for g in range(groups)], axis=0) for h in heads]

    s_hi = [s[h].astype(BF16) for h in heads]
    s_lo = [(s[h] - s_hi[h].astype(F32)).astype(BF16) for h in heads]
    sv = [mm(s_hi[h], jnp.concatenate([vb[h], ones_b], axis=1)) for h in heads]
    s_sum = [sv[h][:, LANES:] + mm(s_lo[h], ones_b) for h in heads]
    wi_rep = [jnp.broadcast_to(w_inter[:, h:h + 1], (TILE, LANES)) for h in heads]
    n0_b = [n0_rows[:, cols[h]].astype(BF16) for h in heads]
    if carry:
        qn = [lax.dot_general(qb[h], n0_b[h], nt_dims, preferred_element_type=F32) for h in heads]
    else:
        qn = [jnp.sum(qb[h].astype(F32) * n0_b[h].astype(F32), axis=1, keepdims=True) for h in heads]
    den = [wi_rep[h] * qn[h] + s_sum[h] for h in heads]
    hh = [(wi_rep[h] * qc[h] + sv[h][:, :LANES]) / jnp.maximum(jnp.abs(den[h]), inv_floor[:, h:h + 1])
          for h in heads]
    h_sq = [mm((hh[h] * hh[h]).astype(BF16), ones_b) for h in heads]
    for h in heads:
        hn = hh[h] * lax.rsqrt(h_sq[h] * (1.0 / LANES) + EPS) * gmh_ref[:, cols[h]]
        yb_ref[:, cols[h]] = (jax.nn.sigmoid(zo_ref[:, cols[h]]) * hn).astype(BF16)

    for h in heads:
        w_last_row = w_last_t[h:h + 1, :]
        kw_t = k_t[h] * w_last_row
        wl_rows = jnp.where(same_group, w_last_row, 0.0).astype(BF16)
        n_rows = wi_rep[h] * n0_rows[:, cols[h]] + mm(wl_rows, k[h].astype(BF16))
        if carry:
            decay = jnp.broadcast_to(w_inter[TILE - 1:TILE, h:h + 1], (LANES, LANES))
            c_ref[0, h] = decay * c_in_ref[0, h] + mm(kw_t.astype(BF16), vb[h])
            n_ref[0, :, cols[h]] = n_rows[TILE - 1:TILE, :]
        else:
            n_ref[:, cols[h]] = n_rows
            for g in range(groups):
                kw_g = jnp.where((c_i // L) == g, kw_t, 0.0).astype(BF16)
                last = g * L + L - 1
                decay = jnp.broadcast_to(w_inter[last:last + 1, h:h + 1], (LANES, LANES))
                c_ref[g, h] = decay * c_in_ref[g, h] + mm(kw_g, vb[h])

    if carry:
        m_ref[0] = mt[TILE - 1:TILE, :]
    else:
        m_ref[...] = mt


def _mlstm_weights(p):
    return [p["w_conv"], p["b_conv"], p["w_q"], p["w_k"], p["w_kt"], p["w_v"], p["b_i"], p["b_f"], p["g_mh"]]


def _mlstm_weight_specs(nh, db, nidx):
    z2 = (lambda i, c: (0, 0)) if nidx == 2 else (lambda i: (0, 0))
    z3 = (lambda i, c: (0, 0, 0)) if nidx == 2 else (lambda i: (0, 0, 0))
    hw = pl.BlockSpec((nh, LANES, LANES), z3)
    return [pl.BlockSpec((CONV_W, db), z2), pl.BlockSpec((1, db), z2), hw, hw, hw, hw,
            pl.BlockSpec((1, LANES), z2), pl.BlockSpec((1, LANES), z2), pl.BlockSpec((1, db), z2)]


def _mlstm_prompt(proj, gates, p, b, t, db, nh, xm_col, zo_col):
    nt = t // TILE
    kern = functools.partial(_mlstm_kernel, L=TILE, carry=True, nh=nh, k_scale=p["k_scale"])
    return pl.pallas_call(
        kern,
        grid=(b, nt),
        in_specs=[pl.BlockSpec((TILE, db), lambda i, c: (i * nt + c, xm_col)),
                  pl.BlockSpec((TILE, db), lambda i, c: (i * nt + c, zo_col)),
                  pl.BlockSpec((TILE, 2 * LANES), lambda i, c: (i * nt + c, 0))] + _mlstm_weight_specs(nh, db, 2),
        out_specs=[pl.BlockSpec((TILE, db), lambda i, c: (i * nt + c, 0)),
                   pl.BlockSpec((1, nh, LANES, LANES), lambda i, c: (i, 0, 0, 0)),
                   pl.BlockSpec((1, 1, db), lambda i, c: (i, 0, 0)),
                   pl.BlockSpec((1, 1, LANES), lambda i, c: (i, 0, 0))],
        out_shape=[jax.ShapeDtypeStruct((b * t, db), BF16),
                   jax.ShapeDtypeStruct((b, nh, LANES, LANES), F32),
                   jax.ShapeDtypeStruct((b, 1, db), F32),
                   jax.ShapeDtypeStruct((b, 1, LANES), F32)],
        scratch_shapes=[pltpu.VMEM((TILE, db), F32)],
        compiler_params=_cparams(("arbitrary", "arbitrary")),
        name="mlstm_prompt",
    )(proj, proj, gates, *_mlstm_weights(p))


def _mlstm_sample(proj, gates, prev_rows, c0, layer, n0_rows, m0_rows, p, L, db, nh, xm_col, zo_col):
    n_rows = proj.shape[0]
    groups = TILE // L
    kern = functools.partial(_mlstm_kernel, L=L, carry=False, nh=nh, k_scale=p["k_scale"])
    row_spec = pl.BlockSpec((TILE, db), lambda i: (i, 0))
    return pl.pallas_call(
        kern,
        grid=(n_rows // TILE,),
        in_specs=[pl.BlockSpec((TILE, db), lambda i: (i, xm_col)),
                  pl.BlockSpec((TILE, db), lambda i: (i, zo_col)),
                  pl.BlockSpec((TILE, 2 * LANES), lambda i: (i, 0)),
                  row_spec,
                  pl.BlockSpec((1, groups, nh, LANES, LANES), lambda i: (layer, i, 0, 0, 0)),
                  row_spec,
                  pl.BlockSpec((TILE, LANES), lambda i: (i, 0))] + _mlstm_weight_specs(nh, db, 1),
        out_specs=[row_spec,
                   pl.BlockSpec((groups, nh, LANES, LANES), lambda i: (i, 0, 0, 0)),
                   row_spec,
                   pl.BlockSpec((TILE, LANES), lambda i: (i, 0))],
        out_shape=[jax.ShapeDtypeStruct((n_rows, db), BF16),
                   jax.ShapeDtypeStruct(c0.shape[1:], F32),
                   jax.ShapeDtypeStruct((n_rows, db), F32),
                   jax.ShapeDtypeStruct((n_rows, LANES), F32)],
        compiler_params=_cparams(("arbitrary",)),
        name="mlstm_sample",
    )(proj, proj, gates, prev_rows, c0, n0_rows, m0_rows, *_mlstm_weights(p))


def _route(lg, n_groups, per_group):
    lane_i = lax.broadcasted_iota(jnp.int32, lg.shape, 1)
    lane = lane_i.astype(F32)
    far = float(LANES)
    is_g = lane_i < n_groups
    lgg = jnp.where(is_g, lg, NEG_INF)
    mg = jnp.max(lgg, axis=1, keepdims=True)
    g_sel = jnp.min(jnp.where(lgg == mg, lane, far), axis=1, keepdims=True)
    p_g = 1.0 / jnp.sum(jnp.where(is_g, jnp.exp(lgg - mg), 0.0), axis=1, keepdims=True)
    lo = n_groups + g_sel * per_group
    sel = (lane >= lo) & (lane < lo + per_group)
    le = jnp.where(sel, lg, NEG_INF)
    m1 = jnp.max(le, axis=1, keepdims=True)
    i1 = jnp.min(jnp.where(le == m1, lane, far), axis=1, keepdims=True)
    le2 = jnp.where(lane == i1, NEG_INF, le)
    m2 = jnp.max(le2, axis=1, keepdims=True)
    i2 = jnp.min(jnp.where(le2 == m2, lane, far), axis=1, keepdims=True)
    ratio = jnp.exp(m2 - m1)
    gate1 = p_g / (1.0 + ratio)
    gate2 = p_g * ratio / (1.0 + ratio)
    out = jnp.where(lane_i == 0, i1 - n_groups, 0.0)
    out = jnp.where(lane_i == 1, i2 - n_groups, out)
    out = jnp.where(lane_i == 2, gate1, out)
    return jnp.where(lane_i == 3, gate2, out)


def _outproj_kernel(x_ref, ya_ref, yb_ref, w_ref, g1_ref, sc_ref, sh_ref, gn_ref, wr_ref, br_ref, h2_all_ref,
                    xn_ref, h2_ref, rt_ref, *, n_groups, per_group):
    del h2_all_ref
    da = ya_ref.shape[1]
    mix = (jnp.dot(ya_ref[...], w_ref[:da, :], preferred_element_type=F32)
           + jnp.dot(yb_ref[...], w_ref[da:, :], preferred_element_type=F32))
    x = x_ref[...] + g1_ref[...] * mix.reshape(x_ref.shape)
    xn_ref[...] = x
    ms = jnp.mean(x * x, axis=-1, keepdims=True)
    h = x * lax.rsqrt(ms + EPS) * gn_ref[...]
    h = (h * (1.0 + sc_ref[...]) + sh_ref[...]).reshape(h2_ref.shape)
    h2_ref[...] = h
    lg = jnp.dot(h.astype(BF16), wr_ref[...], preferred_element_type=F32) + br_ref[...]
    rt_ref[...] = _route(lg, n_groups, per_group)


def _outproj(x, ya, yb, w_out, gate, scale, shift, g_norm, w_router, b_router, n_groups, per_group, h2_all, tok0,
             max_rows):
    b, t, d = x.shape
    bb, tt = _row_tiling(b, t, max_rows)
    rows = bb * tt
    nt = t // tt
    da = ya.shape[1]
    assert tok0 % rows == 0
    blk0 = tok0 // rows
    xspec = pl.BlockSpec((bb, tt, d), lambda i, k: (i, k, 0))
    mspec = pl.BlockSpec((bb, 1, d), lambda i, k: (i, 0, 0))
    kern = functools.partial(_outproj_kernel, n_groups=n_groups, per_group=per_group)
    return pl.pallas_call(
        kern,
        grid=(b // bb, nt),
        in_specs=[xspec,
                  pl.BlockSpec((rows, da), lambda i, k: (i * nt + k, 0)),
                  pl.BlockSpec((rows, ya.shape[1]), lambda i, k: (i * nt + k, 0)),
                  pl.BlockSpec(w_out.shape, lambda i, k: (0, 0), pipeline_mode=pl.Buffered(1)),
                  mspec, mspec, mspec,
                  pl.BlockSpec((1, 1, d), lambda i, k: (0, 0, 0)),
                  pl.BlockSpec(w_router.shape, lambda i, k: (0, 0), pipeline_mode=pl.Buffered(1)),
                  pl.BlockSpec((1, LANES), lambda i, k: (0, 0)),
                  pl.BlockSpec(memory_space=pl.ANY)],
        out_specs=[xspec,
                   pl.BlockSpec((rows, d), lambda i, k: (blk0 + i * nt + k, 0)),
                   pl.BlockSpec((rows, LANES), lambda i, k: (i * nt + k, 0))],
        out_shape=[jax.ShapeDtypeStruct(x.shape, F32),
                   jax.ShapeDtypeStruct(h2_all.shape, F32),
                   jax.ShapeDtypeStruct((b * t, LANES), F32)],
        input_output_aliases={10: 1},
        compiler_params=_cparams(("arbitrary", "arbitrary")),
        name="outproj",
    )(x, ya, yb, w_out, gate, scale, shift, g_norm.reshape(1, 1, d), w_router, b_router, h2_all)


def _gather_kernel(idx_ref, used_ref, src_ref, out_ref, buf, sem, *, chunk):
    i = pl.program_id(0)

    def in_use(step):
        return step * chunk < used_ref[0]

    def issue_step(step, slot):
        def issue(r, _):
            t = idx_ref[step * chunk + r]
            pltpu.make_async_copy(src_ref.at[pl.ds(t, 1)], buf.at[slot, pl.ds(r, 1)], sem.at[slot]).start()
            return 0

        lax.fori_loop(0, chunk, issue, 0, unroll=8)

    @pl.when(i == 0)
    def _():
        issue_step(0, 0)

    @pl.when((i + 1 < pl.num_programs(0)) & in_use(i + 1))
    def _():
        issue_step(i + 1, (i + 1) % 2)

    @pl.when(in_use(i))
    def _():
        slot = i % 2
        pltpu.make_async_copy(buf.at[slot], buf.at[slot], sem.at[slot]).wait()
        out_ref[...] = buf[slot].astype(out_ref.dtype)

    @pl.when(jnp.logical_not(in_use(i)))
    def _():
        out_ref[...] = jnp.zeros(out_ref.shape, out_ref.dtype)


def _gather_rows(idx, n_used, src, out_dtype):
    n_out = idx.shape[0]
    chunk = math.gcd(GATHER_CHUNK, n_out)
    d = src.shape[1]
    return pl.pallas_call(
        functools.partial(_gather_kernel, chunk=chunk),
        grid_spec=pltpu.PrefetchScalarGridSpec(
            num_scalar_prefetch=2,
            grid=(n_out // chunk,),
            in_specs=[pl.BlockSpec(memory_space=pl.ANY)],
            out_specs=pl.BlockSpec((chunk, d), lambda i, idx, used: (i, 0)),
            scratch_shapes=[pltpu.VMEM((2, chunk, d), src.dtype), pltpu.SemaphoreType.DMA((2,))]),
        out_shape=jax.ShapeDtypeStruct((n_out, d), out_dtype),
        compiler_params=_cparams(("arbitrary",)),
        name="gather_rows",
    )(idx, n_used, src)


def _expert_up_kernel(ie_ref, ij_ref, ib_ref, ioj_ref, inew_ref, islot_ref, ine_ref, inj_ref, ihas_ref, n_ref,
                      x_ref, wg_hbm, wu_hbm, o_ref, stage, wgb_scr, wub_scr, sem, *, layer):
    i = pl.program_id(0)
    tn = wgb_scr.shape[1]

    def weight_copies(e, j, slot):
        col = pl.multiple_of(j * tn, tn)
        return (pltpu.make_async_copy(wg_hbm.at[layer, e, :, pl.ds(col, tn)], stage.at[slot, 0], sem.at[slot, 0]),
                pltpu.make_async_copy(wu_hbm.at[layer, e, :, pl.ds(col, tn)], stage.at[slot, 1], sem.at[slot, 1]))

    @pl.when(i == 0)
    def _():
        for cp in weight_copies(ie_ref[0], ij_ref[0], islot_ref[0]):
            cp.start()

    @pl.when(i >= n_ref[0])
    def _():
        o_ref[...] = jnp.zeros(o_ref.shape, o_ref.dtype)

    @pl.when(i < n_ref[0])
    def _():
        @pl.when(inew_ref[i] == 1)
        def _():
            slot = islot_ref[i]

            @pl.when(ihas_ref[i] == 1)
            def _():
                for cp in weight_copies(ine_ref[i], inj_ref[i], 1 - slot):
                    cp.start()

            for cp in weight_copies(ie_ref[i], ij_ref[i], slot):
                cp.wait()
            a = u = None
            for k0 in range(0, x_ref.shape[1], CAST_CHUNK):
                ks = slice(k0, k0 + CAST_CHUNK)
                wg_k = stage[slot, 0, ks, :].astype(BF16)
                wu_k = stage[slot, 1, ks, :].astype(BF16)
                wgb_scr[ks, :] = wg_k
                wub_scr[ks, :] = wu_k
                a_k = jnp.dot(x_ref[:, ks], wg_k, preferred_element_type=F32)
                u_k = jnp.dot(x_ref[:, ks], wu_k, preferred_element_type=F32)
                a = a_k if a is None else a + a_k
                u = u_k if u is None else u + u_k
            o_ref[...] = (a * jax.nn.sigmoid(a) * u).astype(BF16)

        @pl.when(inew_ref[i] != 1)
        def _():
            xb = x_ref[...]
            a = jnp.dot(xb, wgb_scr[...], preferred_element_type=F32)
            u = jnp.dot(xb, wub_scr[...], preferred_element_type=F32)
            o_ref[...] = (a * jax.nn.sigmoid(a) * u).astype(BF16)


def _expert_up(plan, x_buf, w_gate, w_up, layer):
    d, de = w_gate.shape[2], w_gate.shape[3]
    n_rows = x_buf.shape[0]
    tn = de // EXPERT_COL_TILES
    bm = MOE_BLOCK_ROWS
    up = plan["up"]
    n_items = up["e"].shape[0]
    names = ("e", "j", "b", "oj", "new", "slot", "next_e", "next_j", "has_next", "n")
    return pl.pallas_call(
        functools.partial(_expert_up_kernel, layer=layer),
        grid_spec=pltpu.PrefetchScalarGridSpec(
            num_scalar_prefetch=len(names),
            grid=(n_items,),
            in_specs=[pl.BlockSpec((bm, d), lambda i, *s: (s[2][i], 0)),
                      pl.BlockSpec(memory_space=pl.ANY), pl.BlockSpec(memory_space=pl.ANY)],
            out_specs=pl.BlockSpec((bm, tn), lambda i, *s: (s[2][i], s[3][i])),
            scratch_shapes=[pltpu.VMEM((2, 2, d, tn), F32), pltpu.VMEM((d, tn), BF16), pltpu.VMEM((d, tn), BF16),
                            pltpu.SemaphoreType.DMA((2, 2))]),
        out_shape=jax.ShapeDtypeStruct((n_rows, de), BF16),
        compiler_params=_cparams(("arbitrary",)),
        name="expert_up",
    )(*[up[k] for k in names], x_buf, w_gate, w_up)


def _expert_down_kernel(be_ref, bnew_ref, bslot_ref, bne_ref, bhas_ref, bne2_ref, bhas2_ref, n_ref, h_ref, wd_hbm,
                        o_ref, stage, wdb_scr, sem, *, layer):
    i = pl.program_id(0)
    n_slots = stage.shape[0]

    def weight_copy(e, slot):
        return pltpu.make_async_copy(wd_hbm.at[layer, e], stage.at[slot], sem.at[slot])

    @pl.when(i == 0)
    def _():
        weight_copy(be_ref[0], bslot_ref[0]).start()

        @pl.when(bhas_ref[0] == 1)
        def _():
            weight_copy(bne_ref[0], (bslot_ref[0] + 1) % n_slots).start()

    @pl.when(i >= n_ref[0])
    def _():
        o_ref[...] = jnp.zeros(o_ref.shape, o_ref.dtype)

    @pl.when(i < n_ref[0])
    def _():
        @pl.when(bnew_ref[i] == 1)
        def _():
            slot = bslot_ref[i]

            @pl.when(bhas2_ref[i] == 1)
            def _():
                weight_copy(bne2_ref[i], (slot + 2) % n_slots).start()

            weight_copy(be_ref[i], slot).wait()
            y = None
            for k0 in range(0, h_ref.shape[1], CAST_CHUNK):
                ks = slice(k0, k0 + CAST_CHUNK)
                wd_k = stage[slot, ks, :].astype(BF16)
                wdb_scr[ks, :] = wd_k
                y_k = jnp.dot(h_ref[:, ks], wd_k, preferred_element_type=F32)
                y = y_k if y is None else y + y_k
            o_ref[...] = y

        @pl.when(bnew_ref[i] != 1)
        def _():
            o_ref[...] = jnp.dot(h_ref[...], wdb_scr[...], preferred_element_type=F32)


def _expert_down(plan, h_buf, w_down, layer):
    n_rows, de = h_buf.shape
    d = w_down.shape[3]
    bm = MOE_BLOCK_ROWS
    down = plan["down"]
    n_blocks = down["e"].shape[0]
    names = ("e", "new", "slot", "next_e", "has_next", "next2_e", "has_next2", "n")
    return pl.pallas_call(
        functools.partial(_expert_down_kernel, layer=layer),
        grid_spec=pltpu.PrefetchScalarGridSpec(
            num_scalar_prefetch=len(names),
            grid=(n_blocks,),
            in_specs=[pl.BlockSpec((bm, de), lambda i, *s: (i, 0)),
                      pl.BlockSpec(memory_space=pl.ANY)],
            out_specs=pl.BlockSpec((bm, d), lambda i, *s: (i, 0)),
            scratch_shapes=[pltpu.VMEM((DOWN_STAGE_SLOTS, de, d), F32), pltpu.VMEM((de, d), BF16),
                            pltpu.SemaphoreType.DMA((DOWN_STAGE_SLOTS,))]),
        out_shape=jax.ShapeDtypeStruct((n_rows, d), F32),
        compiler_params=_cparams(("arbitrary",)),
        name="expert_down",
    )(*[down[k] for k in names], h_buf, w_down)


def _moe_plan(expert, n_experts):
    bm = MOE_BLOCK_ROWS
    n_tok = expert.shape[0]
    n_asg = n_tok * TOP_K
    e_flat = expert.reshape(-1)
    e_ids = jnp.arange(n_experts, dtype=jnp.int32)

    def take(table, idx):
        return jnp.sum(jnp.where(idx[:, None] == e_ids[None, :], table[None, :], 0), axis=1)

    hit = e_flat[:, None] == e_ids[None, :]
    csum = jnp.cumsum(hit.astype(jnp.int32), axis=0)
    counts = csum[-1]
    rank = jnp.sum(jnp.where(hit, csum, 0), axis=1) - 1
    nblk = (counts + bm - 1) // bm
    bend = jnp.cumsum(nblk)
    bstart = bend - nblk
    dest = take(bstart, e_flat) * bm + rank
    n_blocks = (n_asg + n_experts * (bm - 1) + bm - 1) // bm
    n_rows = -(-n_blocks * bm // GATHER_CHUNK) * GATHER_CHUNK
    tok = jnp.arange(n_asg, dtype=jnp.int32) // TOP_K
    row_src = (jnp.arange(n_rows, dtype=jnp.int32) % n_tok).at[dest].set(tok, unique_indices=True)
    used = bend[-1]

    def expert_of_block(b):
        return jnp.minimum(jnp.sum((bend[None, :] <= b[:, None]).astype(jnp.int32), axis=1), n_experts - 1)

    blk = jnp.arange(n_blocks, dtype=jnp.int32)
    blk_e = expert_of_block(jnp.minimum(blk, used - 1))
    blk_new = (blk == take(bstart, blk_e)).astype(jnp.int32)
    nct = EXPERT_COL_TILES
    n_items = nct * n_blocks
    idx = jnp.arange(n_items, dtype=jnp.int32)
    it = jnp.minimum(idx, nct * used - 1)
    item_e = expert_of_block(it // nct)
    item_bstart = take(bstart, item_e)
    item_nblk = take(nblk, item_e)
    off = it - nct * item_bstart
    item_j = (off // item_nblk).astype(jnp.int32)
    item_b = (item_bstart + off % item_nblk).astype(jnp.int32)
    item_new = (item_b == item_bstart).astype(jnp.int32)
    spare = idx - nct * used
    is_spare = spare >= 0
    item_b = jnp.where(is_spare, used + spare // nct, item_b)
    item_oj = jnp.where(is_spare, spare % nct, item_j)
    e_rank = jnp.cumsum((nblk > 0).astype(jnp.int32)) - 1
    e_after = expert_of_block(jnp.minimum(bend, used - 1))
    more_after = (bend < used).astype(jnp.int32)
    e_after2 = take(e_after, e_after)
    more_after2 = more_after * take(more_after, e_after)
    down = dict(e=blk_e, new=blk_new, slot=take(e_rank, blk_e) % DOWN_STAGE_SLOTS, next_e=take(e_after, blk_e),
                has_next=take(more_after, blk_e), next2_e=take(e_after2, blk_e), has_next2=take(more_after2, blk_e),
                n=used.reshape(1).astype(jnp.int32))
    not_last = item_j < nct - 1
    up = dict(e=item_e, j=item_j, b=item_b, oj=item_oj, new=item_new,
              slot=(nct * take(e_rank, item_e) + item_j) % 2,
              next_e=jnp.where(not_last, item_e, take(e_after, item_e)), next_j=jnp.where(not_last, item_j + 1, 0),
              has_next=jnp.where(not_last, 1, take(more_after, item_e)),
              n=(nct * used).reshape(1).astype(jnp.int32))
    return dict(dest=dest, row_src=row_src, up=up, down=down,
                used_rows=(used * bm).reshape(1).astype(jnp.int32))


def _combine_kernel(dst_ref, x_ref, rt_ref, g2_ref, gf_ref, y_hbm, o_ref, buf, sem, *, final_norm, tok0, n_all):
    bb, tt, _ = x_ref.shape
    rows = bb * tt
    nt = pl.num_programs(1)
    lin = pl.program_id(0) * nt + pl.program_id(1)

    def issue_step(step, slot):
        base = tok0 + step * rows

        def issue(r, _):
            for k in range(TOP_K):
                row = dst_ref[k * n_all + base + r]
                pltpu.make_async_copy(y_hbm.at[pl.ds(row, 1)], buf.at[slot, k, pl.ds(r, 1)], sem.at[slot]).start()
            return 0

        lax.fori_loop(0, rows, issue, 0, unroll=8)

    @pl.when(lin == 0)
    def _():
        issue_step(0, 0)

    @pl.when(lin + 1 < pl.num_programs(0) * nt)
    def _():
        issue_step(lin + 1, (lin + 1) % 2)

    slot = lin % 2
    pltpu.make_async_copy(buf.at[slot], buf.at[slot], sem.at[slot]).wait()
    rt = rt_ref[...]
    y = rt[:, 2:3] * buf[slot, 0] + rt[:, 3:4] * buf[slot, 1]
    x = x_ref[...] + g2_ref[...] * y.reshape(x_ref.shape)
    if final_norm:
        ms = jnp.mean(x * x, axis=-1, keepdims=True)
        x = x * lax.rsqrt(ms + EPS) * gf_ref[...]
    o_ref[...] = x


def _combine(x, y_buf, dest_t, tok0, route, gate, g_final, final_norm, max_rows):
    b, t, d = x.shape
    bb, tt = _row_tiling(b, t, max_rows)
    rows = bb * tt
    nt = t // tt
    xspec = pl.BlockSpec((bb, tt, d), lambda i, k, dst: (i, k, 0))
    kern = functools.partial(_combine_kernel, final_norm=final_norm, tok0=tok0, n_all=dest_t.shape[0] // TOP_K)
    return pl.pallas_call(
        kern,
        grid_spec=pltpu.PrefetchScalarGridSpec(
            num_scalar_prefetch=1,
            grid=(b // bb, nt),
            in_specs=[xspec,
                      pl.BlockSpec((rows, LANES), lambda i, k, dst: (i * nt + k, 0)),
                      pl.BlockSpec((bb, 1, d), lambda i, k, dst: (i, 0, 0)),
                      pl.BlockSpec((1, 1, d), lambda i, k, dst: (0, 0, 0)),
                      pl.BlockSpec(memory_space=pl.ANY)],
            out_specs=xspec,
            scratch_shapes=[pltpu.VMEM((2, TOP_K, rows, d), F32), pltpu.SemaphoreType.DMA((2,))]),
        out_shape=jax.ShapeDtypeStruct(x.shape, F32),
        compiler_params=_cparams(("arbitrary", "arbitrary")),
        name="combine",
    )(dest_t, x, route, gate, g_final.reshape(1, 1, d), y_buf)


def _layer_params(l, w):
    d = w["w_in"].shape[1]
    nh = w["w_q"].shape[1]
    da = w["ln_v_g"].shape[1]
    db = w["w_conv"].shape[2]
    n_main = 2 * da + 2 * db
    w_in = w["w_in"][l]
    w_gate = jnp.zeros((d, 2 * LANES), BF16)
    w_gate = w_gate.at[:, :nh].set(w_in[:, n_main:n_main + nh].astype(BF16))
    w_gate = w_gate.at[:, LANES:LANES + nh].set(w_in[:, n_main + nh:n_main + 2 * nh].astype(BF16))
    n_groups = w["w_r1"].shape[2]
    per_group = w["w_r2"].shape[3]
    w_r = jnp.concatenate([w["w_r1"][l], jnp.transpose(w["w_r2"][l], (1, 0, 2)).reshape(d, n_groups * per_group)],
                          axis=1)
    n_r = w_r.shape[1]
    w_r = jnp.pad(w_r, ((0, 0), (0, LANES - n_r)))
    b_r =jnp.pad(jnp.concatenate([w["b_r1"][l], w["b_r2"][l].reshape(-1)]), (0, LANES - n_r)).reshape(1, LANES)
    return dict(
        g_norm1=w["g_norm1"][l], g_norm2=w["g_norm2"][l],
        w_main=w_in.astype(BF16), n_main=n_main, w_gate=w_gate,
        ln_g=w["ln_v_g"][l], ln_b=w["ln_v_b"][l],
        w_conv=w["w_conv"][l], b_conv=w["b_conv"][l].reshape(1, db),
        w_q=w["w_q"][l].astype(BF16), w_k=w["w_k"][l].astype(BF16),
        w_kt=jnp.swapaxes(w["w_k"][l], 1, 2).astype(BF16), w_v=w["w_v"][l].astype(BF16),
        b_i=jnp.pad(w["b_i"][l], (0, LANES - nh)).reshape(1, LANES),
        b_f=jnp.pad(w["b_f"][l], (0, LANES - nh)).reshape(1, LANES),
        g_mh=w["g_mh"][l].reshape(1, db), k_scale=1.0 / math.sqrt(LANES),
        w_out=w["w_out"][l].astype(BF16),
        w_router=w_r.astype(BF16), b_router=b_r,
        n_groups=n_groups, per_group=per_group,
    )


def _mix_weights(w_spatial, b_spatial, L, da):
    nh = w_spatial.shape[0]
    causal = jnp.tril(jnp.ones((L, L), dtype=bool))
    w = jnp.where(causal[None], w_spatial[:, :L, :L], 0.0)
    reps = TILE // L
    if reps > 1:
        eye = jnp.eye(reps, dtype=F32)
        w = jnp.einsum("ab,hts->hatbs", eye, w).reshape(nh, TILE, TILE)
    bias = jnp.tile(jnp.transpose(b_spatial[:, :L]), (reps, 1))
    bias_full = jnp.repeat(bias, da // nh, axis=1)
    return w.astype(BF16), bias_full


def kernel(x_prompt, x_sample, c_prompt, c_sample, state_mlstm_C, state_mlstm_n, state_mlstm_m, state_conv, w_ada, b_ada, g_norm1, g_norm2, w_in, ln_v_g, ln_v_b, w_spatial, b_spatial, w_conv, b_conv, w_q, w_k, w_v, b_i, b_f, g_mh, w_out, w_r1, b_r1, w_r2, b_r2, w_e_gate, w_e_up, w_e_down, g_final):
    weights = dict(w_in=w_in, ln_v_g=ln_v_g, ln_v_b=ln_v_b, w_conv=w_conv, b_conv=b_conv, w_q=w_q, w_k=w_k,
                   w_v=w_v, b_i=b_i, b_f=b_f, g_mh=g_mh, w_out=w_out, w_r1=w_r1, b_r1=b_r1, w_r2=w_r2, b_r2=b_r2,
                   g_norm1=g_norm1, g_norm2=g_norm2)
    depth = w_in.shape[0]
    bp, tp, d = x_prompt.shape
    bs, ts, _ = x_sample.shape
    nh = w_q.shape[1]
    da = ln_v_g.shape[1]
    db = w_conv.shape[2]
    n_mod = w_ada.shape[2] // d
    n_experts = w_e_gate.shape[1]
    n_p, n_s = bp * tp, bs * ts
    assert da == db and da % LANES == 0 and nh * LANES == db and tp % TILE == 0 and TILE % ts == 0
    assert ts >= CONV_W - 1 and ts % SUBLANES == 0 and n_s % TILE == 0
    xm_col, zo_col = 2 * da // db, 2 * da // db + 1

    mod = _ada(jnp.concatenate([c_prompt, c_sample], axis=0), w_ada, b_ada).reshape(depth, bp + bs, n_mod, 1, d)

    xp, xs = x_prompt, x_sample
    h2_all = jnp.zeros((n_p + n_s, d), F32)
    outs = {k: [] for k in ("cp", "np", "mp", "bp", "cs", "ns", "ms", "bs", "vs")}
    for l in range(depth):
        p = _layer_params(l, weights)
        mods_p = [mod[l, :bp, i] for i in range(n_mod)]
        mods_s = [mod[l, bp:, i] for i in range(n_mod)]

        proj_p, gates_p = _inproj(xp, mods_p[1], mods_p[0], p["g_norm1"], p["w_main"], p["n_main"], p["w_gate"], 1024)
        w_mix_p, bias_p = _mix_weights(w_spatial[l], b_spatial[l], min(tp, TILE), da)
        (ya_p,) = _grpa(proj_p, p["ln_g"], p["ln_b"], w_mix_p, bias_p, da, False)
        yb_p, c_p, n_pn, m_p = _mlstm_prompt(proj_p, gates_p, p, bp, tp, db, nh, xm_col, zo_col)
        outs["cp"].append(c_p)
        outs["np"].append(n_pn.reshape(bp, nh, LANES))
        outs["mp"].append(m_p[:, 0, :nh])
        outs["bp"].append(proj_p.reshape(bp, tp, -1)[:, tp - (CONV_W - 1):, 2 * da:2 * da + db])

        proj_s, gates_s = _inproj(xs, mods_s[1], mods_s[0], p["g_norm1"], p["w_main"], p["n_main"], p["w_gate"], 512)
        w_mix_s, bias_s = _mix_weights(w_spatial[l], b_spatial[l], min(ts, TILE), da)
        ya_s, v_s = _grpa(proj_s, p["ln_g"], p["ln_b"], w_mix_s, bias_s, da, True)
        prev_rows = jnp.pad(state_conv[l], ((0, 0), (ts - (CONV_W - 1), 0), (0, 0))).reshape(n_s, db)
        n0_rows = jnp.repeat(state_mlstm_n[l].reshape(bs, db), ts, axis=0)
        m0_rows = jnp.pad(jnp.repeat(state_mlstm_m[l], ts, axis=0), ((0, 0), (0, LANES - nh)))
        yb_s, c_s, n_rows_s, m_rows_s = _mlstm_sample(proj_s, gates_s, prev_rows, state_mlstm_C, l, n0_rows, m0_rows,
                                                      p, math.gcd(ts, TILE), db, nh, xm_col, zo_col)
        outs["cs"].append(c_s)
        outs["ns"].append(n_rows_s.reshape(bs, ts, nh, LANES)[:, ts - 1])
        outs["ms"].append(m_rows_s.reshape(bs, ts, LANES)[:, ts - 1, :nh])
        outs["bs"].append(proj_s.reshape(bs, ts, -1)[:, ts - (CONV_W - 1):, 2 * da:2 * da + db])
        outs["vs"].append(v_s.reshape(bs, ts, da))

        rargs = (p["w_router"], p["b_router"], p["n_groups"], p["per_group"])
        xp, h2_all, rt_p = _outproj(xp, ya_p, yb_p, p["w_out"], mods_p[2], mods_p[4], mods_p[3], p["g_norm2"], *rargs,
                                    h2_all, 0, 512)
        xs, h2_all, rt_s = _outproj(xs, ya_s, yb_s, p["w_out"], mods_s[2], mods_s[4], mods_s[3], p["g_norm2"], *rargs,
                                    h2_all, n_p, 256)

        route = jnp.concatenate([rt_p, rt_s], axis=0)
        plan = _moe_plan(route[:, :TOP_K].astype(jnp.int32), n_experts)
        x_buf = _gather_rows(plan["row_src"], plan["used_rows"], h2_all, BF16)
        h_buf = _expert_up(plan, x_buf, w_e_gate, w_e_up, l)
        y_buf = _expert_down(plan, h_buf, w_e_down, l)
        dest_t = jnp.transpose(plan["dest"].reshape(n_p + n_s, TOP_K)).reshape(-1)
        last = l == depth - 1
        xp = _combine(xp, y_buf, dest_t, 0, rt_p, mods_p[5], g_final, last, 256)
        xs = _combine(xs, y_buf, dest_t, n_p, rt_s, mods_s[5], g_final, last, 256)

    st = jnp.stack
    return (xp, xs, st(outs["cp"]), st(outs["np"]), st(outs["mp"]), st(outs["bp"]),
            st(outs["cs"]), st(outs["ns"]), st(outs["ms"]), st(outs["bs"]), st(outs["vs"]))
```

```python
import functools
import math

import jax
import jax.numpy as jnp
from jax import lax
from jax.experimental import pallas as pl
from jax.experimental.pallas import tpu as pltpu

F32 = jnp.float32
BF16 = jnp.bfloat16
EPS = 1e-6
LANES = 128
SUBLANES = 8
TILE = 128
CONV_W = 4
TOP_K = 2
VMEM_LIMIT_BYTES = 56 * 1024 * 1024
MOE_BLOCK_ROWS = 256
GATHER_CHUNK = 512
CAST_CHUNK = 256
EXPERT_COL_TILES = 1
DOWN_STAGE_SLOTS = 3
WEIGHT_DMA_CHUNKS = 4
NEG_INF = float("-inf")


def _cparams(sem):
    return pltpu.CompilerParams(dimension_semantics=sem, vmem_limit_bytes=VMEM_LIMIT_BYTES)


def _split3(x):
    x1 = x.astype(BF16)
    r1 = x - x1.astype(F32)
    x2 = r1.astype(BF16)
    r2 = r1 - x2.astype(F32)
    return x1, x2, r2.astype(BF16)


def _ada_kernel(c_ref, w_ref, b_ref, o_ref):
    c = c_ref[...]
    a = c * jax.nn.sigmoid(c)
    o_ref[0] = jnp.dot(a.astype(BF16), w_ref[0].astype(BF16), preferred_element_type=F32) + b_ref[0]


def _ada(c_all, w_ada, b_ada):
    depth, d, n = w_ada.shape
    bc = c_all.shape[0]
    tn = min(1024, n)
    return pl.pallas_call(
        _ada_kernel,
        grid=(depth, n // tn),
        in_specs=[pl.BlockSpec((bc, d), lambda l, j: (0, 0)),
                  pl.BlockSpec((1, d, tn), lambda l, j: (l, 0, j)),
                  pl.BlockSpec((1, 1, tn), lambda l, j: (l, 0, j))],
        out_specs=pl.BlockSpec((1, bc, tn), lambda l, j: (l, 0, j)),
        out_shape=jax.ShapeDtypeStruct((depth, bc, n), F32),
        compiler_params=_cparams(("arbitrary", "arbitrary")),
        name="ada",
    )(c_all, w_ada, b_ada.reshape(depth, 1, n))


def _inproj_kernel(x_ref, sc_ref, sh_ref, g_ref, w_ref, wg_ref, o_ref, og_ref, hn_ref):
    @pl.when(pl.program_id(2) == 0)
    def _():
        x = x_ref[...]
        ms = jnp.mean(x * x, axis=-1, keepdims=True)
        h = x * lax.rsqrt(ms + EPS) * g_ref[...]
        h = h * (1.0 + sc_ref[...]) + sh_ref[...]
        hb = h.reshape(hn_ref.shape).astype(BF16)
        hn_ref[...] = hb
        og_ref[...] = jnp.dot(hb, wg_ref[...], preferred_element_type=F32)

    o_ref[...] = jnp.dot(hn_ref[...], w_ref[...], preferred_element_type=F32)


def _row_tiling(b, t, max_rows):
    if t >= max_rows:
        assert t % max_rows == 0
        return 1, max_rows
    bb = min(b, max_rows // t)
    assert b % bb == 0 and t % SUBLANES == 0
    return bb, t


def _inproj(x, scale, shift, g_norm, w_main, n, w_gate, max_rows):
    b, t, d = x.shape
    ng = w_gate.shape[1]
    bb, tt = _row_tiling(b, t, max_rows)
    rows = bb * tt
    tn = min(512, n)
    nt = t // tt
    return pl.pallas_call(
        _inproj_kernel,
        grid=(b // bb, nt, n // tn),
        in_specs=[pl.BlockSpec((bb, tt, d), lambda i, k, j: (i, k, 0)),
                  pl.BlockSpec((bb, 1, d), lambda i, k, j: (i, 0, 0)),
                  pl.BlockSpec((bb, 1, d), lambda i, k, j: (i, 0, 0)),
                  pl.BlockSpec((1, 1, d), lambda i, k, j: (0, 0, 0)),
                  pl.BlockSpec((d, tn), lambda i, k, j: (0, j)),
                  pl.BlockSpec((d, ng), lambda i, k, j: (0, 0))],
        out_specs=[pl.BlockSpec((rows, tn), lambda i, k, j: (i * nt + k, j)),
                   pl.BlockSpec((rows, ng), lambda i, k, j: (i * nt + k, 0))],
        out_shape=[jax.ShapeDtypeStruct((b * t, n), F32), jax.ShapeDtypeStruct((b * t, ng), F32)],
        scratch_shapes=[pltpu.VMEM((rows, d), BF16)],
        compiler_params=_cparams(("arbitrary", "arbitrary", "arbitrary")),
        name="inproj",
    )(x, scale, shift, g_norm.reshape(1, 1, d), w_main, w_gate)


def _grpa_kernel(u_ref, v_ref, lg_ref, lb_ref, w_ref, bias_ref, ya_ref, *maybe_v_out):
    rows, da = u_ref.shape
    nh = w_ref.shape[0]
    g = jax.nn.gelu(v_ref[...])
    mu = jnp.mean(g, axis=-1, keepdims=True)
    gc = g - mu
    var = jnp.mean(gc * gc, axis=-1, keepdims=True)
    v = gc * lax.rsqrt(var + EPS) * lg_ref[...] + lb_ref[...]
    if maybe_v_out:
        maybe_v_out[0][...] = v
    vb = v.astype(BF16)
    for c in range(rows // TILE):
        r0 = c * TILE
        for h in range(nh):
            c0 = h * LANES
            z = jnp.dot(w_ref[h], vb[r0:r0 + TILE, c0:c0 + LANES], preferred_element_type=F32)
            u = jax.nn.gelu(u_ref[r0:r0 + TILE, c0:c0 + LANES])
            y = u * (z + bias_ref[:, c0:c0 + LANES])
            ya_ref[r0:r0 + TILE, c0:c0 + LANES] = y.astype(BF16)


def _grpa(proj, ln_g, ln_b, w_mix, bias_full, da, want_v):
    n_rows = proj.shape[0]
    rows = min(512, n_rows)
    nh = w_mix.shape[0]
    out_specs = [pl.BlockSpec((rows, da), lambda i: (i, 0))]
    out_shape = [jax.ShapeDtypeStruct((n_rows, da), BF16)]
    if want_v:
        out_specs.append(pl.BlockSpec((rows, da), lambda i: (i, 0)))
        out_shape.append(jax.ShapeDtypeStruct((n_rows, da), F32))
    return pl.pallas_call(
        _grpa_kernel,
        grid=(n_rows // rows,),
        in_specs=[pl.BlockSpec((rows, da), lambda i: (i, 0)),
                  pl.BlockSpec((rows, da), lambda i: (i, 1)),
                  pl.BlockSpec((1, da), lambda i: (0, 0)),
                  pl.BlockSpec((1, da), lambda i: (0, 0)),
                  pl.BlockSpec((nh, TILE, TILE), lambda i: (0, 0, 0)),
                  pl.BlockSpec((TILE, da), lambda i: (0, 0))],
        out_specs=out_specs,
        out_shape=out_shape,
        compiler_params=_cparams(("arbitrary",)),
        name="grpa",
    )(proj, proj, ln_g.reshape(1, da), ln_b.reshape(1, da), w_mix, bias_full)


def _group_last(x, L):
    if L == TILE:
        return jnp.broadcast_to(x[TILE - 1:TILE, :], x.shape)
    g = TILE // L
    x3 = x.reshape(g, L, x.shape[1])
    return jnp.broadcast_to(x3[:, L - 1:L, :], x3.shape).reshape(x.shape)


def _mlstm_kernel(*refs, L, carry, nh, k_scale):
    if carry:
        (xm_ref, zo_ref, g_ref, wconv_ref, bconv_ref, wq_ref, wk_ref, wkt_ref, wv_ref, bi_ref, bf_ref, gmh_ref,
         yb_ref, c_ref, n_ref, m_ref, prev_scr) = refs
        c_in_ref = c_ref
        first = pl.program_id(1) == 0

        @pl.when(first)
        def _():
            c_ref[...] = jnp.zeros(c_ref.shape, F32)
            n_ref[...] = jnp.zeros(n_ref.shape, F32)
            m_ref[...] = jnp.zeros(m_ref.shape, F32)
            prev_scr[...] = jnp.zeros(prev_scr.shape, F32)

        prev = prev_scr[...]
        n0_rows = jnp.broadcast_to(n_ref[0], (TILE, n_ref.shape[2]))
        m0 = jnp.broadcast_to(m_ref[0], (TILE, LANES))
    else:
        (xm_ref, zo_ref, g_ref, prev_ref, c_layer_ref, n0_ref, m0_ref,
         wconv_ref, bconv_ref, wq_ref, wk_ref, wkt_ref, wv_ref, bi_ref, bf_ref, gmh_ref,
         yb_ref, c_ref, n_ref, m_ref) = refs
        c_in_ref = c_layer_ref.at[0]
        prev = prev_ref[...]
        n0_rows = n0_ref[...]
        m0 = m0_ref[...]

    groups = TILE // L
    x = xm_ref[...]
    row_in_group = lax.broadcasted_iota(jnp.int32, (TILE, 1), 0) % L

    acc = bconv_ref[...] + wconv_ref[CONV_W - 1:CONV_W, :] * x
    for j in range(1, CONV_W):
        from_prev = pltpu.roll(prev, (j - L) % TILE, 0)
        shifted = jnp.where(row_in_group < j, from_prev, pltpu.roll(x, j, 0))
        acc = acc + wconv_ref[CONV_W - 1 - j:CONV_W - j, :] * shifted
    xc = acc * jax.nn.sigmoid(acc)
    if carry:
        prev_scr[...] = x

    gates = g_ref[...]
    gi = gates[:, :LANES] + bi_ref[...]
    gf = gates[:, LANES:] + bf_ref[...]
    lf = jnp.minimum(gf, 0.0) - jnp.log1p(jnp.exp(-jnp.abs(gf)))

    r_i = lax.broadcasted_iota(jnp.int32, (TILE, TILE), 0)
    c_i = lax.broadcasted_iota(jnp.int32, (TILE, TILE), 1)
    mask = (c_i <= r_i) & ((c_i // L) == (r_i // L)) if L != TILE else (c_i <= r_i)
    mask_b = jnp.where(mask, 1.0, 0.0).astype(BF16)
    same_group = ((c_i // L) == (r_i // L)) if L != TILE else (c_i >= 0)
    ones_b = jnp.ones((TILE, LANES), BF16)
    lf1, lf2, lf3 = _split3(lf)
    cum = (jnp.dot(mask_b, lf1, preferred_element_type=F32) + jnp.dot(mask_b, lf2, preferred_element_type=F32)
           + jnp.dot(mask_b, lf3, preferred_element_type=F32))
    cum_t = cum.T
    gi_t = gi.T
    log_inter = cum + m0

    def d_matrix(h):
        d = cum[:, h:h + 1] - cum_t[h:h + 1, :] + gi_t[h:h + 1, :]
        return jnp.where(mask, d, NEG_INF)

    run = gi - cum
    step = 1
    while step < L:
        run = jnp.maximum(run, jnp.where(row_in_group >= step, pltpu.roll(run, step, 0), NEG_INF))
        step *= 2
    mt = jnp.maximum(log_inter, cum + run)

    m_last = _group_last(mt, L)
    w_inter = jnp.exp(log_inter - mt)
    w_last = jnp.exp(_group_last(cum, L) - cum + gi - m_last)
    w_last_t = w_last.T
    inv_floor = jnp.exp(-mt)

    heads = range(nh)
    cols = [slice(h * LANES, (h + 1) * LANES) for h in heads]
    nt_dims = (((1,), (1,)), ((), ()))

    def mm(a, b):
        return jnp.dot(a, b, preferred_element_type=F32)

    xcb = [xc[:, cols[h]].astype(BF16) for h in heads]
    q = [mm(xcb[h], wq_ref[h]) for h in heads]
    k = [mm(xcb[h], wk_ref[h]) * k_scale for h in heads]
    k_t = [lax.dot_general(wkt_ref[h], xcb[h], nt_dims, preferred_element_type=F32) * k_scale for h in heads]
    vb = [mm(x[:, cols[h]].astype(BF16), wv_ref[h]).astype(BF16) for h in heads]
    qb = [q[h].astype(BF16) for h in heads]

    w_intra = [jnp.exp(d_matrix(h) - mt[:, h:h + 1]) for h in heads]
    s = [mm(qb[h], k_t[h].astype(BF16)) * w_intra[h] for h in heads]
    if carry:
        qc = [mm(qb[h], c_in_ref[0, h].astype(BF16)) for h in heads]
    else:
        qc = [jnp.concatenate([mm(q[h][g * L:(g + 1) * L, :].astype(BF16), c_in_ref[g, h].astype(BF16))
                               for g in range(groups)], axis=0) for h in heads]

    s_hi = [s[h].astype(BF16) for h in heads]
    s_lo = [(s[h] - s_hi[h].astype(F32)).astype(BF16) for h in heads]
    sv = [mm(s_hi[h], jnp.concatenate([vb[h], ones_b], axis=1)) for h in heads]
    s_sum = [sv[h][:, LANES:] + mm(s_lo[h], ones_b) for h in heads]
    wi_rep = [jnp.broadcast_to(w_inter[:, h:h + 1], (TILE, LANES)) for h in heads]
    n0_b = [n0_rows[:, cols[h]].astype(BF16) for h in heads]
    if carry:
        qn = [lax.dot_general(qb[h], n0_b[h], nt_dims, preferred_element_type=F32) for h in heads]
    else:
        qn = [jnp.sum(qb[h].astype(F32) * n0_b[h].astype(F32), axis=1, keepdims=True) for h in heads]
    den = [wi_rep[h] * qn[h] + s_sum[h] for h in heads]
    hh = [(wi_rep[h] * qc[h] + sv[h][:, :LANES]) / jnp.maximum(jnp.abs(den[h]), inv_floor[:, h:h + 1])
          for h in heads]
    h_sq = [mm((hh[h] * hh[h]).astype(BF16), ones_b) for h in heads]
    for h in heads:
        hn = hh[h] * lax.rsqrt(h_sq[h] * (1.0 / LANES) + EPS) * gmh_ref[:, cols[h]]
        yb_ref[:, cols[h]] = (jax.nn.sigmoid(zo_ref[:, cols[h]]) * hn).astype(BF16)

    for h in heads:
        w_last_row = w_last_t[h:h + 1, :]
        kw_t = k_t[h] * w_last_row
        wl_rows = jnp.where(same_group, w_last_row, 0.0).astype(BF16)
        n_rows = wi_rep[h] * n0_rows[:, cols[h]] + mm(wl_rows, k[h].astype(BF16))
        if carry:
            decay = jnp.broadcast_to(w_inter[TILE - 1:TILE, h:h + 1], (LANES, LANES))
            c_ref[0, h] = decay * c_in_ref[0, h] + mm(kw_t.astype(BF16), vb[h])
            n_ref[0, :, cols[h]] = n_rows[TILE - 1:TILE, :]
        else:
            n_ref[:, cols[h]] = n_rows
            for g in range(groups):
                kw_g = jnp.where((c_i // L) == g, kw_t, 0.0).astype(BF16)
                last = g * L + L - 1
                decay = jnp.broadcast_to(w_inter[last:last + 1, h:h + 1], (LANES, LANES))
                c_ref[g, h] = decay * c_in_ref[g, h] + mm(kw_g, vb[h])

    if carry:
        m_ref[0] = mt[TILE - 1:TILE, :]
    else:
        m_ref[...] = mt


def _mlstm_weights(p):
    return [p["w_conv"], p["b_conv"], p["w_q"], p["w_k"], p["w_kt"], p["w_v"], p["b_i"], p["b_f"], p["g_mh"]]


def _mlstm_weight_specs(nh, db, nidx):
    z2 = (lambda i, c: (0, 0)) if nidx == 2 else (lambda i: (0, 0))
    z3 = (lambda i, c: (0, 0, 0)) if nidx == 2 else (lambda i: (0, 0, 0))
    hw = pl.BlockSpec((nh, LANES, LANES), z3)
    return [pl.BlockSpec((CONV_W, db), z2), pl.BlockSpec((1, db), z2), hw, hw, hw, hw,
            pl.BlockSpec((1, LANES), z2), pl.BlockSpec((1, LANES), z2), pl.BlockSpec((1, db), z2)]


def _mlstm_prompt(proj, gates, p, b, t, db, nh, xm_col, zo_col):
    nt = t // TILE
    kern = functools.partial(_mlstm_kernel, L=TILE, carry=True, nh=nh, k_scale=p["k_scale"])
    return pl.pallas_call(
        kern,
        grid=(b, nt),
        in_specs=[pl.BlockSpec((TILE, db), lambda i, c: (i * nt + c, xm_col)),
                  pl.BlockSpec((TILE, db), lambda i, c: (i * nt + c, zo_col)),
                  pl.BlockSpec((TILE, 2 * LANES), lambda i, c: (i * nt + c, 0))] + _mlstm_weight_specs(nh, db, 2),
        out_specs=[pl.BlockSpec((TILE, db), lambda i, c: (i * nt + c, 0)),
                   pl.BlockSpec((1, nh, LANES, LANES), lambda i, c: (i, 0, 0, 0)),
                   pl.BlockSpec((1, 1, db), lambda i, c: (i, 0, 0)),
                   pl.BlockSpec((1, 1, LANES), lambda i, c: (i, 0, 0))],
        out_shape=[jax.ShapeDtypeStruct((b * t, db), BF16),
                   jax.ShapeDtypeStruct((b, nh, LANES, LANES), F32),
                   jax.ShapeDtypeStruct((b, 1, db), F32),
                   jax.ShapeDtypeStruct((b, 1, LANES), F32)],
        scratch_shapes=[pltpu.VMEM((TILE, db), F32)],
        compiler_params=_cparams(("arbitrary", "arbitrary")),
        name="mlstm_prompt",
    )(proj, proj, gates, *_mlstm_weights(p))


def _mlstm_sample(proj, gates, prev_rows, c0, layer, n0_rows, m0_rows, p, L, db, nh, xm_col, zo_col):
    n_rows = proj.shape[0]
    groups = TILE // L
    kern = functools.partial(_mlstm_kernel, L=L, carry=False, nh=nh, k_scale=p["k_scale"])
    row_spec = pl.BlockSpec((TILE, db), lambda i: (i, 0))
    return pl.pallas_call(
        kern,
        grid=(n_rows // TILE,),
        in_specs=[pl.BlockSpec((TILE, db), lambda i: (i, xm_col)),
                  pl.BlockSpec((TILE, db), lambda i: (i, zo_col)),
                  pl.BlockSpec((TILE, 2 * LANES), lambda i: (i, 0)),
                  row_spec,
                  pl.BlockSpec((1, groups, nh, LANES, LANES), lambda i: (layer, i, 0, 0, 0)),
                  row_spec,
                  pl.BlockSpec((TILE, LANES), lambda i: (i, 0))] + _mlstm_weight_specs(nh, db, 1),
        out_specs=[row_spec,
                   pl.BlockSpec((groups, nh, LANES, LANES), lambda i: (i, 0, 0, 0)),
                   row_spec,
                   pl.BlockSpec((TILE, LANES), lambda i: (i, 0))],
        out_shape=[jax.ShapeDtypeStruct((n_rows, db), BF16),
                   jax.ShapeDtypeStruct(c0.shape[1:], F32),
                   jax.ShapeDtypeStruct((n_rows, db), F32),
                   jax.ShapeDtypeStruct((n_rows, LANES), F32)],
        compiler_params=_cparams(("arbitrary",)),
        name="mlstm_sample",
    )(proj, proj, gates, prev_rows, c0, n0_rows, m0_rows, *_mlstm_weights(p))


def _route(lg, n_groups, per_group):
    lane_i = lax.broadcasted_iota(jnp.int32, lg.shape, 1)
    lane = lane_i.astype(F32)
    far = float(LANES)
    is_g = lane_i < n_groups
    lgg = jnp.where(is_g, lg, NEG_INF)
    mg = jnp.max(lgg, axis=1, keepdims=True)
    g_sel = jnp.min(jnp.where(lgg == mg, lane, far), axis=1, keepdims=True)
    p_g = 1.0 / jnp.sum(jnp.where(is_g, jnp.exp(lgg - mg), 0.0), axis=1, keepdims=True)
    lo = n_groups + g_sel * per_group
    sel = (lane >= lo) & (lane < lo + per_group)
    le = jnp.where(sel, lg, NEG_INF)
    m1 = jnp.max(le, axis=1, keepdims=True)
    i1 = jnp.min(jnp.where(le == m1, lane, far), axis=1, keepdims=True)
    le2 = jnp.where(lane == i1, NEG_INF, le)
    m2 = jnp.max(le2, axis=1, keepdims=True)
    i2 = jnp.min(jnp.where(le2 == m2, lane, far), axis=1, keepdims=True)
    ratio = jnp.exp(m2 - m1)
    gate1 = p_g / (1.0 + ratio)
    gate2 = p_g * ratio / (1.0 + ratio)
    out = jnp.where(lane_i == 0, i1 - n_groups, 0.0)
    out = jnp.where(lane_i == 1, i2 - n_groups, out)
    out = jnp.where(lane_i == 2, gate1, out)
    return jnp.where(lane_i == 3, gate2, out)


def _outproj_kernel(x_ref, ya_ref, yb_ref, w_ref, g1_ref, sc_ref, sh_ref, gn_ref, wr_ref, br_ref, h2_all_ref,
                    xn_ref, h2_ref, rt_ref, *, n_groups, per_group):
    del h2_all_ref
    da = ya_ref.shape[1]
    mix = (jnp.dot(ya_ref[...], w_ref[:da, :], preferred_element_type=F32)
           + jnp.dot(yb_ref[...], w_ref[da:, :], preferred_element_type=F32))
    x = x_ref[...] + g1_ref[...] * mix.reshape(x_ref.shape)
    xn_ref[...] = x
    ms = jnp.mean(x * x, axis=-1, keepdims=True)
    h = x * lax.rsqrt(ms + EPS) * gn_ref[...]
    h = (h * (1.0 + sc_ref[...]) + sh_ref[...]).reshape(h2_ref.shape)
    h2_ref[...] = h
    lg = jnp.dot(h.astype(BF16), wr_ref[...], preferred_element_type=F32) + br_ref[...]
    rt_ref[...] = _route(lg, n_groups, per_group)


def _outproj(x, ya, yb, w_out, gate, scale, shift, g_norm, w_router, b_router, n_groups, per_group, h2_all, tok0,
             max_rows):
    b, t, d = x.shape
    bb, tt = _row_tiling(b, t, max_rows)
    rows = bb * tt
    nt = t // tt
    da = ya.shape[1]
    assert tok0 % rows == 0
    blk0 = tok0 // rows
    xspec = pl.BlockSpec((bb, tt, d), lambda i, k: (i, k, 0))
    mspec = pl.BlockSpec((bb, 1, d), lambda i, k: (i, 0, 0))
    kern = functools.partial(_outproj_kernel, n_groups=n_groups, per_group=per_group)
    return pl.pallas_call(
        kern,
        grid=(b // bb, nt),
        in_specs=[xspec,
                  pl.BlockSpec((rows, da), lambda i, k: (i * nt + k, 0)),
                  pl.BlockSpec((rows, ya.shape[1]), lambda i, k: (i * nt + k, 0)),
                  pl.BlockSpec(w_out.shape, lambda i, k: (0, 0), pipeline_mode=pl.Buffered(1)),
                  mspec, mspec, mspec,
                  pl.BlockSpec((1, 1, d), lambda i, k: (0, 0, 0)),
                  pl.BlockSpec(w_router.shape, lambda i, k: (0, 0), pipeline_mode=pl.Buffered(1)),
                  pl.BlockSpec((1, LANES), lambda i, k: (0, 0)),
                  pl.BlockSpec(memory_space=pl.ANY)],
        out_specs=[xspec,
                   pl.BlockSpec((rows, d), lambda i, k: (blk0 + i * nt + k, 0)),
                   pl.BlockSpec((rows, LANES), lambda i, k: (i * nt + k, 0))],
        out_shape=[jax.ShapeDtypeStruct(x.shape, F32),
                   jax.ShapeDtypeStruct(h2_all.shape, F32),
                   jax.ShapeDtypeStruct((b * t, LANES), F32)],
        input_output_aliases={10: 1},
        compiler_params=_cparams(("arbitrary", "arbitrary")),
        name="outproj",
    )(x, ya, yb, w_out, gate, scale, shift, g_norm.reshape(1, 1, d), w_router, b_router, h2_all)


def _gather_kernel(idx_ref, used_ref, src_ref, out_ref, buf, sem, *, chunk):
    i = pl.program_id(0)

    def in_use(step):
        return step * chunk < used_ref[0]

    def issue_step(step, slot):
        def issue(r, _):
            t = idx_ref[step * chunk + r]
            pltpu.make_async_copy(src_ref.at[pl.ds(t, 1)], buf.at[slot, pl.ds(r, 1)], sem.at[slot]).start()
            return 0

        lax.fori_loop(0, chunk, issue, 0, unroll=8)

    @pl.when(i == 0)
    def _():
        issue_step(0, 0)

    @pl.when((i + 1 < pl.num_programs(0)) & in_use(i + 1))
    def _():
        issue_step(i + 1, (i + 1) % 2)

    @pl.when(in_use(i))
    def _():
        slot = i % 2
        pltpu.make_async_copy(buf.at[slot], buf.at[slot], sem.at[slot]).wait()
        out_ref[...] = buf[slot].astype(out_ref.dtype)

    @pl.when(jnp.logical_not(in_use(i)))
    def _():
        out_ref[...] = jnp.zeros(out_ref.shape, out_ref.dtype)


def _gather_rows(idx, n_used, src, out_dtype):
    n_out = idx.shape[0]
    chunk = math.gcd(GATHER_CHUNK, n_out)
    d = src.shape[1]
    return pl.pallas_call(
        functools.partial(_gather_kernel, chunk=chunk),
        grid_spec=pltpu.PrefetchScalarGridSpec(
            num_scalar_prefetch=2,
            grid=(n_out // chunk,),
            in_specs=[pl.BlockSpec(memory_space=pl.ANY)],
            out_specs=pl.BlockSpec((chunk, d), lambda i, idx, used: (i, 0)),
            scratch_shapes=[pltpu.VMEM((2, chunk, d), src.dtype), pltpu.SemaphoreType.DMA((2,))]),
        out_shape=jax.ShapeDtypeStruct((n_out, d), out_dtype),
        compiler_params=_cparams(("arbitrary",)),
        name="gather_rows",
    )(idx, n_used, src)


def _expert_up_kernel(ie_ref, ij_ref, ib_ref, ioj_ref, inew_ref, islot_ref, ine_ref, inj_ref, ihas_ref, n_ref,
                      x_ref, wg_hbm, wu_hbm, o_ref, stage, wgb_scr, wub_scr, sem, *, layer):
    i = pl.program_id(0)
    tn = wgb_scr.shape[1]

    def weight_copies(e, j, slot):
        col = pl.multiple_of(j * tn, tn)
        rows = x_ref.shape[1] // WEIGHT_DMA_CHUNKS
        copies = []
        for t, w_hbm in enumerate((wg_hbm, wu_hbm)):
            for c in range(WEIGHT_DMA_CHUNKS):
                rs = pl.ds(c * rows, rows)
                copies.append(pltpu.make_async_copy(w_hbm.at[layer, e, rs, pl.ds(col, tn)],
                                                    stage.at[slot, t, rs, :], sem.at[slot, t]))
        return copies

    @pl.when(i == 0)
    def _():
        for cp in weight_copies(ie_ref[0], ij_ref[0], islot_ref[0]):
            cp.start()

    @pl.when(i >= n_ref[0])
    def _():
        o_ref[...] = jnp.zeros(o_ref.shape, o_ref.dtype)

    @pl.when(i < n_ref[0])
    def _():
        @pl.when(inew_ref[i] == 1)
        def _():
            slot = islot_ref[i]

            @pl.when(ihas_ref[i] == 1)
            def _():
                for cp in weight_copies(ine_ref[i], inj_ref[i], 1 - slot):
                    cp.start()

            for cp in weight_copies(ie_ref[i], ij_ref[i], slot):
                cp.wait()
            a = u = None
            for k0 in range(0, x_ref.shape[1], CAST_CHUNK):
                ks = slice(k0, k0 + CAST_CHUNK)
                wg_k = stage[slot, 0, ks, :].astype(BF16)
                wu_k = stage[slot, 1, ks, :].astype(BF16)
                wgb_scr[ks, :] = wg_k
                wub_scr[ks, :] = wu_k
                a_k = jnp.dot(x_ref[:, ks], wg_k, preferred_element_type=F32)
                u_k = jnp.dot(x_ref[:, ks], wu_k, preferred_element_type=F32)
                a = a_k if a is None else a + a_k
                u = u_k if u is None else u + u_k
            o_ref[...] = (a * jax.nn.sigmoid(a) * u).astype(BF16)

        @pl.when(inew_ref[i] != 1)
        def _():
            xb = x_ref[...]
            a = jnp.dot(xb, wgb_scr[...], preferred_element_type=F32)
            u = jnp.dot(xb, wub_scr[...], preferred_element_type=F32)
            o_ref[...] = (a * jax.nn.sigmoid(a) * u).astype(BF16)


def _expert_up(plan, x_buf, w_gate, w_up, layer):
    d, de = w_gate.shape[2], w_gate.shape[3]
    n_rows = x_buf.shape[0]
    tn = de // EXPERT_COL_TILES
    bm = MOE_BLOCK_ROWS
    up = plan["up"]
    n_items = up["e"].shape[0]
    names = ("e", "j", "b", "oj", "new", "slot", "next_e", "next_j", "has_next", "n")
    return pl.pallas_call(
        functools.partial(_expert_up_kernel, layer=layer),
        grid_spec=pltpu.PrefetchScalarGridSpec(
            num_scalar_prefetch=len(names),
            grid=(n_items,),
            in_specs=[pl.BlockSpec((bm, d), lambda i, *s: (s[2][i], 0)),
                      pl.BlockSpec(memory_space=pl.ANY), pl.BlockSpec(memory_space=pl.ANY)],
            out_specs=pl.BlockSpec((bm, tn), lambda i, *s: (s[2][i], s[3][i])),
            scratch_shapes=[pltpu.VMEM((2, 2, d, tn), F32), pltpu.VMEM((d, tn), BF16), pltpu.VMEM((d, tn), BF16),
                            pltpu.SemaphoreType.DMA((2, 2))]),
        out_shape=jax.ShapeDtypeStruct((n_rows, de), BF16),
        compiler_params=_cparams(("arbitrary",)),
        name="expert_up",
    )(*[up[k] for k in names], x_buf, w_gate, w_up)


def _expert_down_kernel(be_ref, bnew_ref, bslot_ref, bne_ref, bhas_ref, bne2_ref, bhas2_ref, n_ref, h_ref, wd_hbm,
                        o_ref, stage, wdb_scr, sem, *, layer):
    i = pl.program_id(0)
    n_slots = stage.shape[0]

    def weight_copies(e, slot):
        rows = h_ref.shape[1] // WEIGHT_DMA_CHUNKS
        return [pltpu.make_async_copy(wd_hbm.at[layer, e, pl.ds(c * rows, rows), :],
                                      stage.at[slot, pl.ds(c * rows, rows), :], sem.at[slot])
                for c in range(WEIGHT_DMA_CHUNKS)]

    @pl.when(i == 0)
    def _():
        for cp in weight_copies(be_ref[0], bslot_ref[0]):
            cp.start()

        @pl.when(bhas_ref[0] == 1)
        def _():
            for cp in weight_copies(bne_ref[0], (bslot_ref[0] + 1) % n_slots):
                cp.start()

    @pl.when(i >= n_ref[0])
    def _():
        o_ref[...] = jnp.zeros(o_ref.shape, o_ref.dtype)

    @pl.when(i < n_ref[0])
    def _():
        @pl.when(bnew_ref[i] == 1)
        def _():
            slot = bslot_ref[i]

            @pl.when(bhas2_ref[i] == 1)
            def _():
                for cp in weight_copies(bne2_ref[i], (slot + 2) % n_slots):
                    cp.start()

            for cp in weight_copies(be_ref[i], slot):
                cp.wait()
            y = None
            for k0 in range(0, h_ref.shape[1], CAST_CHUNK):
                ks = slice(k0, k0 + CAST_CHUNK)
                wd_k = stage[slot, ks, :].astype(BF16)
                wdb_scr[ks, :] = wd_k
                y_k = jnp.dot(h_ref[:, ks], wd_k, preferred_element_type=F32)
                y = y_k if y is None else y + y_k
            o_ref[...] = y

        @pl.when(bnew_ref[i] != 1)
        def _():
            o_ref[...] = jnp.dot(h_ref[...], wdb_scr[...], preferred_element_type=F32)


def _expert_down(plan, h_buf, w_down, layer):
    n_rows, de = h_buf.shape
    d = w_down.shape[3]
    bm = MOE_BLOCK_ROWS
    down = plan["down"]
    n_blocks = down["e"].shape[0]
    names = ("e", "new", "slot", "next_e", "has_next", "next2_e", "has_next2", "n")
    return pl.pallas_call(
        functools.partial(_expert_down_kernel, layer=layer),
        grid_spec=pltpu.PrefetchScalarGridSpec(
            num_scalar_prefetch=len(names),
            grid=(n_blocks,),
            in_specs=[pl.BlockSpec((bm, de), lambda i, *s: (i, 0)),
                      pl.BlockSpec(memory_space=pl.ANY)],
            out_specs=pl.BlockSpec((bm, d), lambda i, *s: (i, 0)),
            scratch_shapes=[pltpu.VMEM((DOWN_STAGE_SLOTS, de, d), F32), pltpu.VMEM((de, d), BF16),
                            pltpu.SemaphoreType.DMA((DOWN_STAGE_SLOTS,))]),
        out_shape=jax.ShapeDtypeStruct((n_rows, d), F32),
        compiler_params=_cparams(("arbitrary",)),
        name="expert_down",
    )(*[down[k] for k in names], h_buf, w_down)


def _moe_plan(expert, n_experts):
    bm = MOE_BLOCK_ROWS
    n_tok = expert.shape[0]
    n_asg = n_tok * TOP_K
    e_flat = expert.reshape(-1)
    e_ids = jnp.arange(n_experts, dtype=jnp.int32)

    def take(table, idx):
        return jnp.sum(jnp.where(idx[:, None] == e_ids[None, :], table[None, :], 0), axis=1)

    hit = e_flat[:, None] == e_ids[None, :]
    csum = jnp.cumsum(hit.astype(jnp.int32), axis=0)
    counts = csum[-1]
    rank = jnp.sum(jnp.where(hit, csum, 0), axis=1) - 1
    nblk = (counts + bm - 1) // bm
    bend = jnp.cumsum(nblk)
    bstart = bend - nblk
    dest = take(bstart, e_flat) * bm + rank
    n_blocks = (n_asg + n_experts * (bm - 1) + bm - 1) // bm
    n_rows = -(-n_blocks * bm // GATHER_CHUNK) * GATHER_CHUNK
    tok = jnp.arange(n_asg, dtype=jnp.int32) // TOP_K
    row_src = (jnp.arange(n_rows, dtype=jnp.int32) % n_tok).at[dest].set(tok, unique_indices=True)
    used = bend[-1]

    def expert_of_block(b):
        return jnp.minimum(jnp.sum((bend[None, :] <= b[:, None]).astype(jnp.int32), axis=1), n_experts - 1)

    blk = jnp.arange(n_blocks, dtype=jnp.int32)
    blk_e = expert_of_block(jnp.minimum(blk, used - 1))
    blk_new = (blk == take(bstart, blk_e)).astype(jnp.int32)
    nct = EXPERT_COL_TILES
    n_items = nct * n_blocks
    idx = jnp.arange(n_items, dtype=jnp.int32)
    it = jnp.minimum(idx, nct * used - 1)
    item_e = expert_of_block(it // nct)
    item_bstart = take(bstart, item_e)
    item_nblk = take(nblk, item_e)
    off = it - nct * item_bstart
    item_j = (off // item_nblk).astype(jnp.int32)
    item_b = (item_bstart + off % item_nblk).astype(jnp.int32)
    item_new = (item_b == item_bstart).astype(jnp.int32)
    spare = idx - nct * used
    is_spare = spare >= 0
    item_b = jnp.where(is_spare, used + spare // nct, item_b)
    item_oj = jnp.where(is_spare, spare % nct, item_j)
    e_rank = jnp.cumsum((nblk > 0).astype(jnp.int32)) - 1
    e_after = expert_of_block(jnp.minimum(bend, used - 1))
    more_after = (bend < used).astype(jnp.int32)
    e_after2 = take(e_after, e_after)
    more_after2 = more_after * take(more_after, e_after)
    down = dict(e=blk_e, new=blk_new, slot=take(e_rank, blk_e) % DOWN_STAGE_SLOTS, next_e=take(e_after, blk_e),
                has_next=take(more_after, blk_e), next2_e=take(e_after2, blk_e), has_next2=take(more_after2, blk_e),
                n=used.reshape(1).astype(jnp.int32))
    not_last = item_j < nct - 1
    up = dict(e=item_e, j=item_j, b=item_b, oj=item_oj, new=item_new,
              slot=(nct * take(e_rank, item_e) + item_j) % 2,
              next_e=jnp.where(not_last, item_e, take(e_after, item_e)), next_j=jnp.where(not_last, item_j + 1, 0),
              has_next=jnp.where(not_last, 1, take(more_after, item_e)),
              n=(nct * used).reshape(1).astype(jnp.int32))
    return dict(dest=dest, row_src=row_src, up=up, down=down,
                used_rows=(used * bm).reshape(1).astype(jnp.int32))


def _combine_kernel(dst_ref, x_ref, rt_ref, g2_ref, gf_ref, y_hbm, o_ref, buf, sem, *, final_norm, tok0, n_all):
    bb, tt, _ = x_ref.shape
    rows = bb * tt
    nt = pl.num_programs(1)
    lin = pl.program_id(0) * nt + pl.program_id(1)

    def issue_step(step, slot):
        base = tok0 + step * rows

        def issue(r, _):
            for k in range(TOP_K):
                row = dst_ref[k * n_all + base + r]
                pltpu.make_async_copy(y_hbm.at[pl.ds(row, 1)], buf.at[slot, k, pl.ds(r, 1)], sem.at[slot]).start()
            return 0

        lax.fori_loop(0, rows, issue, 0, unroll=8)

    @pl.when(lin == 0)
    def _():
        issue_step(0, 0)

    @pl.when(lin + 1 < pl.num_programs(0) * nt)
    def _():
        issue_step(lin + 1, (lin + 1) % 2)

    slot = lin % 2
    pltpu.make_async_copy(buf.at[slot], buf.at[slot], sem.at[slot]).wait()
    rt = rt_ref[...]
    y = rt[:, 2:3] * buf[slot, 0] + rt[:, 3:4] * buf[slot, 1]
    x = x_ref[...] + g2_ref[...] * y.reshape(x_ref.shape)
    if final_norm:
        ms = jnp.mean(x * x, axis=-1, keepdims=True)
        x = x * lax.rsqrt(ms + EPS) * gf_ref[...]
    o_ref[...] = x


def _combine(x, y_buf, dest_t, tok0, route, gate, g_final, final_norm, max_rows):
    b, t, d = x.shape
    bb, tt = _row_tiling(b, t, max_rows)
    rows = bb * tt
    nt = t // tt
    xspec = pl.BlockSpec((bb, tt, d), lambda i, k, dst: (i, k, 0))
    kern = functools.partial(_combine_kernel, final_norm=final_norm, tok0=tok0, n_all=dest_t.shape[0] // TOP_K)
    return pl.pallas_call(
        kern,
        grid_spec=pltpu.PrefetchScalarGridSpec(
            num_scalar_prefetch=1,
            grid=(b // bb, nt),
            in_specs=[xspec,
                      pl.BlockSpec((rows, LANES), lambda i, k, dst: (i * nt + k, 0)),
                      pl.BlockSpec((bb, 1, d), lambda i, k, dst: (i, 0, 0)),
                      pl.BlockSpec((1, 1, d), lambda i, k, dst: (0, 0, 0)),
                      pl.BlockSpec(memory_space=pl.ANY)],
            out_specs=xspec,
            scratch_shapes=[pltpu.VMEM((2, TOP_K, rows, d), F32), pltpu.SemaphoreType.DMA((2,))]),
        out_shape=jax.ShapeDtypeStruct(x.shape, F32),
        compiler_params=_cparams(("arbitrary", "arbitrary")),
        name="combine",
    )(dest_t, x, route, gate, g_final.reshape(1, 1, d), y_buf)


def _layer_params(l, w):
    d = w["w_in"].shape[1]
    nh = w["w_q"].shape[1]
    da = w["ln_v_g"].shape[1]
    db = w["w_conv"].shape[2]
    n_main = 2 * da + 2 * db
    w_in = w["w_in"][l]
    w_gate = jnp.zeros((d, 2 * LANES), BF16)
    w_gate = w_gate.at[:, :nh].set(w_in[:, n_main:n_main + nh].astype(BF16))
    w_gate = w_gate.at[:, LANES:LANES + nh].set(w_in[:, n_main + nh:n_main + 2 * nh].astype(BF16))
    n_groups = w["w_r1"].shape[2]
    per_group = w["w_r2"].shape[3]
    w_r = jnp.concatenate([w["w_r1"][l], jnp.transpose(w["w_r2"][l], (1, 0, 2)).reshape(d, n_groups * per_group)],
                          axis=1)
    n_r = w_r.shape[1]
    w_r = jnp.pad(w_r, ((0, 0), (0, LANES - n_r)))
    b_r =jnp.pad(jnp.concatenate([w["b_r1"][l], w["b_r2"][l].reshape(-1)]), (0, LANES - n_r)).reshape(1, LANES)
    return dict(
        g_norm1=w["g_norm1"][l], g_norm2=w["g_norm2"][l],
        w_main=w_in.astype(BF16), n_main=n_main, w_gate=w_gate,
        ln_g=w["ln_v_g"][l], ln_b=w["ln_v_b"][l],
        w_conv=w["w_conv"][l], b_conv=w["b_conv"][l].reshape(1, db),
        w_q=w["w_q"][l].astype(BF16), w_k=w["w_k"][l].astype(BF16),
        w_kt=jnp.swapaxes(w["w_k"][l], 1, 2).astype(BF16), w_v=w["w_v"][l].astype(BF16),
        b_i=jnp.pad(w["b_i"][l], (0, LANES - nh)).reshape(1, LANES),
        b_f=jnp.pad(w["b_f"][l], (0, LANES - nh)).reshape(1, LANES),
        g_mh=w["g_mh"][l].reshape(1, db), k_scale=1.0 / math.sqrt(LANES),
        w_out=w["w_out"][l].astype(BF16),
        w_router=w_r.astype(BF16), b_router=b_r,
        n_groups=n_groups, per_group=per_group,
    )


def _mix_weights(w_spatial, b_spatial, L, da):
    nh = w_spatial.shape[0]
    causal = jnp.tril(jnp.ones((L, L), dtype=bool))
    w = jnp.where(causal[None], w_spatial[:, :L, :L], 0.0)
    reps = TILE // L
    if reps > 1:
        eye = jnp.eye(reps, dtype=F32)
        w = jnp.einsum("ab,hts->hatbs", eye, w).reshape(nh, TILE, TILE)
    bias = jnp.tile(jnp.transpose(b_spatial[:, :L]), (reps, 1))
    bias_full = jnp.repeat(bias, da // nh, axis=1)
    return w.astype(BF16), bias_full


def kernel(x_prompt, x_sample, c_prompt, c_sample, state_mlstm_C, state_mlstm_n, state_mlstm_m, state_conv, w_ada, b_ada, g_norm1, g_norm2, w_in, ln_v_g, ln_v_b, w_spatial, b_spatial, w_conv, b_conv, w_q, w_k, w_v, b_i, b_f, g_mh, w_out, w_r1, b_r1, w_r2, b_r2, w_e_gate, w_e_up, w_e_down, g_final):
    weights = dict(w_in=w_in, ln_v_g=ln_v_g, ln_v_b=ln_v_b, w_conv=w_conv, b_conv=b_conv, w_q=w_q, w_k=w_k,
                   w_v=w_v, b_i=b_i, b_f=b_f, g_mh=g_mh, w_out=w_out, w_r1=w_r1, b_r1=b_r1, w_r2=w_r2, b_r2=b_r2,
                   g_norm1=g_norm1, g_norm2=g_norm2)
    depth = w_in.shape[0]
    bp, tp, d = x_prompt.shape
    bs, ts, _ = x_sample.shape
    nh = w_q.shape[1]
    da = ln_v_g.shape[1]
    db = w_conv.shape[2]
    n_mod = w_ada.shape[2] // d
    n_experts = w_e_gate.shape[1]
    n_p, n_s = bp * tp, bs * ts
    assert da == db and da % LANES == 0 and nh * LANES == db and tp % TILE == 0 and TILE % ts == 0
    assert ts >= CONV_W - 1 and ts % SUBLANES == 0 and n_s % TILE == 0
    xm_col, zo_col = 2 * da // db, 2 * da // db + 1

    mod = _ada(jnp.concatenate([c_prompt, c_sample], axis=0), w_ada, b_ada).reshape(depth, bp + bs, n_mod, 1, d)

    xp, xs = x_prompt, x_sample
    h2_all = jnp.zeros((n_p + n_s, d), F32)
    outs = {k: [] for k in ("cp", "np", "mp", "bp", "cs", "ns", "ms", "bs", "vs")}
    for l in range(depth):
        p = _layer_params(l, weights)
        mods_p = [mod[l, :bp, i] for i in range(n_mod)]
        mods_s = [mod[l, bp:, i] for i in range(n_mod)]

        proj_p, gates_p = _inproj(xp, mods_p[1], mods_p[0], p["g_norm1"], p["w_main"], p["n_main"], p["w_gate"], 1024)
        w_mix_p, bias_p = _mix_weights(w_spatial[l], b_spatial[l], min(tp, TILE), da)
        (ya_p,) = _grpa(proj_p, p["ln_g"], p["ln_b"], w_mix_p, bias_p, da, False)
        yb_p, c_p, n_pn, m_p = _mlstm_prompt(proj_p, gates_p, p, bp, tp, db, nh, xm_col, zo_col)
        outs["cp"].append(c_p)
        outs["np"].append(n_pn.reshape(bp, nh, LANES))
        outs["mp"].append(m_p[:, 0, :nh])
        outs["bp"].append(proj_p.reshape(bp, tp, -1)[:, tp - (CONV_W - 1):, 2 * da:2 * da + db])

        proj_s, gates_s = _inproj(xs, mods_s[1], mods_s[0], p["g_norm1"], p["w_main"], p["n_main"], p["w_gate"], 512)
        w_mix_s, bias_s = _mix_weights(w_spatial[l], b_spatial[l], min(ts, TILE), da)
        ya_s, v_s = _grpa(proj_s, p["ln_g"], p["ln_b"], w_mix_s, bias_s, da, True)
        prev_rows = jnp.pad(state_conv[l], ((0, 0), (ts - (CONV_W - 1), 0), (0, 0))).reshape(n_s, db)
        n0_rows = jnp.repeat(state_mlstm_n[l].reshape(bs, db), ts, axis=0)
        m0_rows = jnp.pad(jnp.repeat(state_mlstm_m[l], ts, axis=0), ((0, 0), (0, LANES - nh)))
        yb_s, c_s, n_rows_s, m_rows_s = _mlstm_sample(proj_s, gates_s, prev_rows, state_mlstm_C, l, n0_rows, m0_rows,
                                                      p, math.gcd(ts, TILE), db, nh, xm_col, zo_col)
        outs["cs"].append(c_s)
        outs["ns"].append(n_rows_s.reshape(bs, ts, nh, LANES)[:, ts - 1])
        outs["ms"].append(m_rows_s.reshape(bs, ts, LANES)[:, ts - 1, :nh])
        outs["bs"].append(proj_s.reshape(bs, ts, -1)[:, ts - (CONV_W - 1):, 2 * da:2 * da + db])
        outs["vs"].append(v_s.reshape(bs, ts, da))

        rargs = (p["w_router"], p["b_router"], p["n_groups"], p["per_group"])
        xp, h2_all, rt_p = _outproj(xp, ya_p, yb_p, p["w_out"], mods_p[2], mods_p[4], mods_p[3], p["g_norm2"], *rargs,
                                    h2_all, 0, 512)
        xs, h2_all, rt_s = _outproj(xs, ya_s, yb_s, p["w_out"], mods_s[2], mods_s[4], mods_s[3], p["g_norm2"], *rargs,
                                    h2_all, n_p, 256)

        route = jnp.concatenate([rt_p, rt_s], axis=0)
        plan = _moe_plan(route[:, :TOP_K].astype(jnp.int32), n_experts)
        x_buf = _gather_rows(plan["row_src"], plan["used_rows"], h2_all, BF16)
        h_buf = _expert_up(plan, x_buf, w_e_gate, w_e_up, l)
        y_buf = _expert_down(plan, h_buf, w_e_down, l)
        dest_t = jnp.transpose(plan["dest"].reshape(n_p + n_s, TOP_K)).reshape(-1)
        last = l == depth - 1
        xp = _combine(xp, y_buf, dest_t, 0, rt_p, mods_p[5], g_final, last, 256)
        xs = _combine(xs, y_buf, dest_t, n_p, rt_s, mods_s[5], g_final, last, 256)

    st = jnp.stack
    return (xp, xs, st(outs["cp"]), st(outs["np"]), st(outs["mp"]), st(outs["bp"]),
            st(outs["cs"]), st(outs["ns"]), st(outs["ms"]), st(outs["bs"]), st(outs["vs"]))
```

```python
import functools
import math

import jax
import jax.numpy as jnp
from jax import lax
from jax.experimental import pallas as pl
from jax.experimental.pallas import tpu as pltpu

F32 = jnp.float32
BF16 = jnp.bfloat16
EPS = 1e-6
LANES = 128
SUBLANES = 8
TILE = 128
CONV_W = 4
TOP_K = 2
VMEM_LIMIT_BYTES = 56 * 1024 * 1024
MOE_BLOCK_ROWS = 256
GATHER_CHUNK = 1024
CAST_CHUNK = 256
EXPERT_COL_TILES = 1
DOWN_STAGE_SLOTS = 3
WEIGHT_DMA_CHUNKS = 4
NEG_INF = float("-inf")


def _cparams(sem):
    return pltpu.CompilerParams(dimension_semantics=sem, vmem_limit_bytes=VMEM_LIMIT_BYTES)


def _split3(x):
    x1 = x.astype(BF16)
    r1 = x - x1.astype(F32)
    x2 = r1.astype(BF16)
    r2 = r1 - x2.astype(F32)
    return x1, x2, r2.astype(BF16)


def _ada_kernel(c_ref, w_ref, b_ref, o_ref):
    c = c_ref[...]
    a = c * jax.nn.sigmoid(c)
    o_ref[0] = jnp.dot(a.astype(BF16), w_ref[0].astype(BF16), preferred_element_type=F32) + b_ref[0]


def _ada(c_all, w_ada, b_ada):
    depth, d, n = w_ada.shape
    bc = c_all.shape[0]
    tn = min(1024, n)
    return pl.pallas_call(
        _ada_kernel,
        grid=(depth, n // tn),
        in_specs=[pl.BlockSpec((bc, d), lambda l, j: (0, 0)),
                  pl.BlockSpec((1, d, tn), lambda l, j: (l, 0, j)),
                  pl.BlockSpec((1, 1, tn), lambda l, j: (l, 0, j))],
        out_specs=pl.BlockSpec((1, bc, tn), lambda l, j: (l, 0, j)),
        out_shape=jax.ShapeDtypeStruct((depth, bc, n), F32),
        compiler_params=_cparams(("arbitrary", "arbitrary")),
        name="ada",
    )(c_all, w_ada, b_ada.reshape(depth, 1, n))


def _inproj_kernel(x_ref, sc_ref, sh_ref, g_ref, w_ref, wg_ref, o_ref, og_ref, hn_ref):
    @pl.when(pl.program_id(2) == 0)
    def _():
        x = x_ref[...]
        ms = jnp.mean(x * x, axis=-1, keepdims=True)
        h = x * lax.rsqrt(ms + EPS) * g_ref[...]
        h = h * (1.0 + sc_ref[...]) + sh_ref[...]
        hb = h.reshape(hn_ref.shape).astype(BF16)
        hn_ref[...] = hb
        og_ref[...] = jnp.dot(hb, wg_ref[...], preferred_element_type=F32)

    o_ref[...] = jnp.dot(hn_ref[...], w_ref[...], preferred_element_type=F32)


def _row_tiling(b, t, max_rows):
    if t >= max_rows:
        assert t % max_rows == 0
        return 1, max_rows
    bb = min(b, max_rows // t)
    assert b % bb == 0 and t % SUBLANES == 0
    return bb, t


def _inproj(x, scale, shift, g_norm, w_main, n, w_gate, max_rows):
    b, t, d = x.shape
    ng = w_gate.shape[1]
    bb, tt = _row_tiling(b, t, max_rows)
    rows = bb * tt
    tn = min(1024, n)
    nt = t // tt
    return pl.pallas_call(
        _inproj_kernel,
        grid=(b // bb, nt, n // tn),
        in_specs=[pl.BlockSpec((bb, tt, d), lambda i, k, j: (i, k, 0)),
                  pl.BlockSpec((bb, 1, d), lambda i, k, j: (i, 0, 0)),
                  pl.BlockSpec((bb, 1, d), lambda i, k, j: (i, 0, 0)),
                  pl.BlockSpec((1, 1, d), lambda i, k, j: (0, 0, 0)),
                  pl.BlockSpec((d, tn), lambda i, k, j: (0, j)),
                  pl.BlockSpec((d, ng), lambda i, k, j: (0, 0))],
        out_specs=[pl.BlockSpec((rows, tn), lambda i, k, j: (i * nt + k, j)),
                   pl.BlockSpec((rows, ng), lambda i, k, j: (i * nt + k, 0))],
        out_shape=[jax.ShapeDtypeStruct((b * t, n), F32), jax.ShapeDtypeStruct((b * t, ng), F32)],
        scratch_shapes=[pltpu.VMEM((rows, d), BF16)],
        compiler_params=_cparams(("arbitrary", "arbitrary", "arbitrary")),
        name="inproj",
    )(x, scale, shift, g_norm.reshape(1, 1, d), w_main, w_gate)


def _grpa_kernel(u_ref, v_ref, lg_ref, lb_ref, w_ref, bias_ref, ya_ref, *maybe_v_out):
    rows, da = u_ref.shape
    nh = w_ref.shape[0]
    g = jax.nn.gelu(v_ref[...])
    mu = jnp.mean(g, axis=-1, keepdims=True)
    gc = g - mu
    var = jnp.mean(gc * gc, axis=-1, keepdims=True)
    v = gc * lax.rsqrt(var + EPS) * lg_ref[...] + lb_ref[...]
    if maybe_v_out:
        maybe_v_out[0][...] = v
    vb = v.astype(BF16)
    for c in range(rows // TILE):
        r0 = c * TILE
        for h in range(nh):
            c0 = h * LANES
            z = jnp.dot(w_ref[h], vb[r0:r0 + TILE, c0:c0 + LANES], preferred_element_type=F32)
            u = jax.nn.gelu(u_ref[r0:r0 + TILE, c0:c0 + LANES])
            y = u * (z + bias_ref[:, c0:c0 + LANES])
            ya_ref[r0:r0 + TILE, c0:c0 + LANES] = y.astype(BF16)


def _grpa(proj, ln_g, ln_b, w_mix, bias_full, da, want_v):
    n_rows = proj.shape[0]
    rows = min(512, n_rows)
    nh = w_mix.shape[0]
    out_specs = [pl.BlockSpec((rows, da), lambda i: (i, 0))]
    out_shape = [jax.ShapeDtypeStruct((n_rows, da), BF16)]
    if want_v:
        out_specs.append(pl.BlockSpec((rows, da), lambda i: (i, 0)))
        out_shape.append(jax.ShapeDtypeStruct((n_rows, da), F32))
    return pl.pallas_call(
        _grpa_kernel,
        grid=(n_rows // rows,),
        in_specs=[pl.BlockSpec((rows, da), lambda i: (i, 0)),
                  pl.BlockSpec((rows, da), lambda i: (i, 1)),
                  pl.BlockSpec((1, da), lambda i: (0, 0)),
                  pl.BlockSpec((1, da), lambda i: (0, 0)),
                  pl.BlockSpec((nh, TILE, TILE), lambda i: (0, 0, 0)),
                  pl.BlockSpec((TILE, da), lambda i: (0, 0))],
        out_specs=out_specs,
        out_shape=out_shape,
        compiler_params=_cparams(("arbitrary",)),
        name="grpa",
    )(proj, proj, ln_g.reshape(1, da), ln_b.reshape(1, da), w_mix, bias_full)


def _group_last(x, L):
    if L == TILE:
        return jnp.broadcast_to(x[TILE - 1:TILE, :], x.shape)
    g = TILE // L
    x3 = x.reshape(g, L, x.shape[1])
    return jnp.broadcast_to(x3[:, L - 1:L, :], x3.shape).reshape(x.shape)


def _mlstm_kernel(*refs, L, carry, nh, k_scale):
    if carry:
        (xm_ref, zo_ref, g_ref, wconv_ref, bconv_ref, wq_ref, wk_ref, wkt_ref, wv_ref, bi_ref, bf_ref, gmh_ref,
         yb_ref, c_ref, n_ref, m_ref, prev_scr) = refs
        c_in_ref = c_ref
        first = pl.program_id(1) == 0

        @pl.when(first)
        def _():
            c_ref[...] = jnp.zeros(c_ref.shape, F32)
            n_ref[...] = jnp.zeros(n_ref.shape, F32)
            m_ref[...] = jnp.zeros(m_ref.shape, F32)
            prev_scr[...] = jnp.zeros(prev_scr.shape, F32)

        prev = prev_scr[...]
        n0_rows = jnp.broadcast_to(n_ref[0], (TILE, n_ref.shape[2]))
        m0 = jnp.broadcast_to(m_ref[0], (TILE, LANES))
    else:
        (xm_ref, zo_ref, g_ref, prev_ref, c_layer_ref, n0_ref, m0_ref,
         wconv_ref, bconv_ref, wq_ref, wk_ref, wkt_ref, wv_ref, bi_ref, bf_ref, gmh_ref,
         yb_ref, c_ref, n_ref, m_ref) = refs
        c_in_ref = c_layer_ref.at[0]
        prev = prev_ref[...]
        n0_rows = n0_ref[...]
        m0 = m0_ref[...]

    groups = TILE // L
    x = xm_ref[...]
    row_in_group = lax.broadcasted_iota(jnp.int32, (TILE, 1), 0) % L

    acc = bconv_ref[...] + wconv_ref[CONV_W - 1:CONV_W, :] * x
    for j in range(1, CONV_W):
        from_prev = pltpu.roll(prev, (j - L) % TILE, 0)
        shifted = jnp.where(row_in_group < j, from_prev, pltpu.roll(x, j, 0))
        acc = acc + wconv_ref[CONV_W - 1 - j:CONV_W - j, :] * shifted
    xc = acc * jax.nn.sigmoid(acc)
    if carry:
        prev_scr[...] = x

    gates = g_ref[...]
    gi = gates[:, :LANES] + bi_ref[...]
    gf = gates[:, LANES:] + bf_ref[...]
    lf = jnp.minimum(gf, 0.0) - jnp.log1p(jnp.exp(-jnp.abs(gf)))

    r_i = lax.broadcasted_iota(jnp.int32, (TILE, TILE), 0)
    c_i = lax.broadcasted_iota(jnp.int32, (TILE, TILE), 1)
    mask = (c_i <= r_i) & ((c_i // L) == (r_i // L)) if L != TILE else (c_i <= r_i)
    mask_b = jnp.where(mask, 1.0, 0.0).astype(BF16)
    same_group = ((c_i // L) == (r_i // L)) if L != TILE else (c_i >= 0)
    ones_b = jnp.ones((TILE, LANES), BF16)
    lf1, lf2, lf3 = _split3(lf)
    cum = (jnp.dot(mask_b, lf1, preferred_element_type=F32) + jnp.dot(mask_b, lf2, preferred_element_type=F32)
           + jnp.dot(mask_b, lf3, preferred_element_type=F32))
    cum_t = cum.T
    gi_t = gi.T
    log_inter = cum + m0

    def d_matrix(h):
        d = cum[:, h:h + 1] - cum_t[h:h + 1, :] + gi_t[h:h + 1, :]
        return jnp.where(mask, d, NEG_INF)

    run = gi - cum
    step = 1
    while step < L:
        run = jnp.maximum(run, jnp.where(row_in_group >= step, pltpu.roll(run, step, 0), NEG_INF))
        step *= 2
    mt = jnp.maximum(log_inter, cum + run)

    m_last = _group_last(mt, L)
    w_inter = jnp.exp(log_inter - mt)
    w_last = jnp.exp(_group_last(cum, L) - cum + gi - m_last)
    w_last_t = w_last.T
    inv_floor = jnp.exp(-mt)

    heads = range(nh)
    cols = [slice(h * LANES, (h + 1) * LANES) for h in heads]
    nt_dims = (((1,), (1,)), ((), ()))

    def mm(a, b):
        return jnp.dot(a, b, preferred_element_type=F32)

    xcb = [xc[:, cols[h]].astype(BF16) for h in heads]
    q = [mm(xcb[h], wq_ref[h]) for h in heads]
    k = [mm(xcb[h], wk_ref[h]) * k_scale for h in heads]
    k_t = [lax.dot_general(wkt_ref[h], xcb[h], nt_dims, preferred_element_type=F32) * k_scale for h in heads]
    vb = [mm(x[:, cols[h]].astype(BF16), wv_ref[h]).astype(BF16) for h in heads]
    qb = [q[h].astype(BF16) for h in heads]

    w_intra = [jnp.exp(d_matrix(h) - mt[:, h:h + 1]) for h in heads]
    s = [mm(qb[h], k_t[h].astype(BF16)) * w_intra[h] for h in heads]
    if carry:
        qc = [mm(qb[h], c_in_ref[0, h].astype(BF16)) for h in heads]
    else:
        qc = [jnp.concatenate([mm(q[h][g * L:(g + 1) * L, :].astype(BF16), c_in_ref[g, h].astype(BF16))
                               for g in range(groups)], axis=0) for h in heads]

    s_hi = [s[h].astype(BF16) for h in heads]
    s_lo = [(s[h] - s_hi[h].astype(F32)).astype(BF16) for h in heads]
    sv = [mm(s_hi[h], jnp.concatenate([vb[h], ones_b], axis=1)) for h in heads]
    s_sum = [sv[h][:, LANES:] + mm(s_lo[h], ones_b) for h in heads]
    wi_rep = [jnp.broadcast_to(w_inter[:, h:h + 1], (TILE, LANES)) for h in heads]
    n0_b = [n0_rows[:, cols[h]].astype(BF16) for h in heads]
    if carry:
        qn = [lax.dot_general(qb[h], n0_b[h], nt_dims, preferred_element_type=F32) for h in heads]
    else:
        qn = [jnp.sum(qb[h].astype(F32) * n0_b[h].astype(F32), axis=1, keepdims=True) for h in heads]
    den = [wi_rep[h] * qn[h] + s_sum[h] for h in heads]
    hh = [(wi_rep[h] * qc[h] + sv[h][:, :LANES]) / jnp.maximum(jnp.abs(den[h]), inv_floor[:, h:h + 1])
          for h in heads]
    h_sq = [mm((hh[h] * hh[h]).astype(BF16), ones_b) for h in heads]
    for h in heads:
        hn = hh[h] * lax.rsqrt(h_sq[h] * (1.0 / LANES) + EPS) * gmh_ref[:, cols[h]]
        yb_ref[:, cols[h]] = (jax.nn.sigmoid(zo_ref[:, cols[h]]) * hn).astype(BF16)

    for h in heads:
        w_last_row = w_last_t[h:h + 1, :]
        kw_t = k_t[h] * w_last_row
        wl_rows = jnp.where(same_group, w_last_row, 0.0).astype(BF16)
        n_rows = wi_rep[h] * n0_rows[:, cols[h]] + mm(wl_rows, k[h].astype(BF16))
        if carry:
            decay = jnp.broadcast_to(w_inter[TILE - 1:TILE, h:h + 1], (LANES, LANES))
            c_ref[0, h] = decay * c_in_ref[0, h] + mm(kw_t.astype(BF16), vb[h])
            n_ref[0, :, cols[h]] = n_rows[TILE - 1:TILE, :]
        else:
            n_ref[:, cols[h]] = n_rows
            for g in range(groups):
                kw_g = jnp.where((c_i // L) == g, kw_t, 0.0).astype(BF16)
                last = g * L + L - 1
                decay = jnp.broadcast_to(w_inter[last:last + 1, h:h + 1], (LANES, LANES))
                c_ref[g, h] = decay * c_in_ref[g, h] + mm(kw_g, vb[h])

    if carry:
        m_ref[0] = mt[TILE - 1:TILE, :]
    else:
        m_ref[...] = mt


def _mlstm_weights(p):
    return [p["w_conv"], p["b_conv"], p["w_q"], p["w_k"], p["w_kt"], p["w_v"], p["b_i"], p["b_f"], p["g_mh"]]


def _mlstm_weight_specs(nh, db, nidx):
    z2 = (lambda i, c: (0, 0)) if nidx == 2 else (lambda i: (0, 0))
    z3 = (lambda i, c: (0, 0, 0)) if nidx == 2 else (lambda i: (0, 0, 0))
    hw = pl.BlockSpec((nh, LANES, LANES), z3)
    return [pl.BlockSpec((CONV_W, db), z2), pl.BlockSpec((1, db), z2), hw, hw, hw, hw,
            pl.BlockSpec((1, LANES), z2), pl.BlockSpec((1, LANES), z2), pl.BlockSpec((1, db), z2)]


def _mlstm_prompt(proj, gates, p, b, t, db, nh, xm_col, zo_col):
    nt = t // TILE
    kern = functools.partial(_mlstm_kernel, L=TILE, carry=True, nh=nh, k_scale=p["k_scale"])
    return pl.pallas_call(
        kern,
        grid=(b, nt),
        in_specs=[pl.BlockSpec((TILE, db), lambda i, c: (i * nt + c, xm_col)),
                  pl.BlockSpec((TILE, db), lambda i, c: (i * nt + c, zo_col)),
                  pl.BlockSpec((TILE, 2 * LANES), lambda i, c: (i * nt + c, 0))] + _mlstm_weight_specs(nh, db, 2),
        out_specs=[pl.BlockSpec((TILE, db), lambda i, c: (i * nt + c, 0)),
                   pl.BlockSpec((1, nh, LANES, LANES), lambda i, c: (i, 0, 0, 0)),
                   pl.BlockSpec((1, 1, db), lambda i, c: (i, 0, 0)),
                   pl.BlockSpec((1, 1, LANES), lambda i, c: (i, 0, 0))],
        out_shape=[jax.ShapeDtypeStruct((b * t, db), BF16),
                   jax.ShapeDtypeStruct((b, nh, LANES, LANES), F32),
                   jax.ShapeDtypeStruct((b, 1, db), F32),
                   jax.ShapeDtypeStruct((b, 1, LANES), F32)],
        scratch_shapes=[pltpu.VMEM((TILE, db), F32)],
        compiler_params=_cparams(("arbitrary", "arbitrary")),
        name="mlstm_prompt",
    )(proj, proj, gates, *_mlstm_weights(p))


def _mlstm_sample(proj, gates, prev_rows, c0, layer, n0_rows, m0_rows, p, L, db, nh, xm_col, zo_col):
    n_rows = proj.shape[0]
    groups = TILE // L
    kern = functools.partial(_mlstm_kernel, L=L, carry=False, nh=nh, k_scale=p["k_scale"])
    row_spec = pl.BlockSpec((TILE, db), lambda i: (i, 0))
    return pl.pallas_call(
        kern,
        grid=(n_rows // TILE,),
        in_specs=[pl.BlockSpec((TILE, db), lambda i: (i, xm_col)),
                  pl.BlockSpec((TILE, db), lambda i: (i, zo_col)),
                  pl.BlockSpec((TILE, 2 * LANES), lambda i: (i, 0)),
                  row_spec,
                  pl.BlockSpec((1, groups, nh, LANES, LANES), lambda i: (layer, i, 0, 0, 0)),
                  row_spec,
                  pl.BlockSpec((TILE, LANES), lambda i: (i, 0))] + _mlstm_weight_specs(nh, db, 1),
        out_specs=[row_spec,
                   pl.BlockSpec((groups, nh, LANES, LANES), lambda i: (i, 0, 0, 0)),
                   row_spec,
                   pl.BlockSpec((TILE, LANES), lambda i: (i, 0))],
        out_shape=[jax.ShapeDtypeStruct((n_rows, db), BF16),
                   jax.ShapeDtypeStruct(c0.shape[1:], F32),
                   jax.ShapeDtypeStruct((n_rows, db), F32),
                   jax.ShapeDtypeStruct((n_rows, LANES), F32)],
        compiler_params=_cparams(("arbitrary",)),
        name="mlstm_sample",
    )(proj, proj, gates, prev_rows, c0, n0_rows, m0_rows, *_mlstm_weights(p))


def _route(lg, n_groups, per_group):
    lane_i = lax.broadcasted_iota(jnp.int32, lg.shape, 1)
    lane = lane_i.astype(F32)
    far = float(LANES)
    is_g = lane_i < n_groups
    lgg = jnp.where(is_g, lg, NEG_INF)
    mg = jnp.max(lgg, axis=1, keepdims=True)
    g_sel = jnp.min(jnp.where(lgg == mg, lane, far), axis=1, keepdims=True)
    p_g = 1.0 / jnp.sum(jnp.where(is_g, jnp.exp(lgg - mg), 0.0), axis=1, keepdims=True)
    lo = n_groups + g_sel * per_group
    sel = (lane >= lo) & (lane < lo + per_group)
    le = jnp.where(sel, lg, NEG_INF)
    m1 = jnp.max(le, axis=1, keepdims=True)
    i1 = jnp.min(jnp.where(le == m1, lane, far), axis=1, keepdims=True)
    le2 = jnp.where(lane == i1, NEG_INF, le)
    m2 = jnp.max(le2, axis=1, keepdims=True)
    i2 = jnp.min(jnp.where(le2 == m2, lane, far), axis=1, keepdims=True)
    ratio = jnp.exp(m2 - m1)
    gate1 = p_g / (1.0 + ratio)
    gate2 = p_g * ratio / (1.0 + ratio)
    out = jnp.where(lane_i == 0, i1 - n_groups, 0.0)
    out = jnp.where(lane_i == 1, i2 - n_groups, out)
    out = jnp.where(lane_i == 2, gate1, out)
    return jnp.where(lane_i == 3, gate2, out)


def _outproj_kernel(x_ref, ya_ref, yb_ref, w_ref, g1_ref, sc_ref, sh_ref, gn_ref, wr_ref, br_ref, h2_all_ref,
                    xn_ref, h2_ref, rt_ref, *, n_groups, per_group):
    del h2_all_ref
    da = ya_ref.shape[1]
    mix = (jnp.dot(ya_ref[...], w_ref[:da, :], preferred_element_type=F32)
           + jnp.dot(yb_ref[...], w_ref[da:, :], preferred_element_type=F32))
    x = x_ref[...] + g1_ref[...] * mix.reshape(x_ref.shape)
    xn_ref[...] = x
    ms = jnp.mean(x * x, axis=-1, keepdims=True)
    h = x * lax.rsqrt(ms + EPS) * gn_ref[...]
    h = (h * (1.0 + sc_ref[...]) + sh_ref[...]).reshape(h2_ref.shape)
    h2_ref[...] = h
    lg = jnp.dot(h.astype(BF16), wr_ref[...], preferred_element_type=F32) + br_ref[...]
    rt_ref[...] = _route(lg, n_groups, per_group)


def _outproj(x, ya, yb, w_out, gate, scale, shift, g_norm, w_router, b_router, n_groups, per_group, h2_all, tok0,
             max_rows):
    b, t, d = x.shape
    bb, tt = _row_tiling(b, t, max_rows)
    rows = bb * tt
    nt = t // tt
    da = ya.shape[1]
    assert tok0 % rows == 0
    blk0 = tok0 // rows
    xspec = pl.BlockSpec((bb, tt, d), lambda i, k: (i, k, 0))
    mspec = pl.BlockSpec((bb, 1, d), lambda i, k: (i, 0, 0))
    kern = functools.partial(_outproj_kernel, n_groups=n_groups, per_group=per_group)
    return pl.pallas_call(
        kern,
        grid=(b // bb, nt),
        in_specs=[xspec,
                  pl.BlockSpec((rows, da), lambda i, k: (i * nt + k, 0)),
                  pl.BlockSpec((rows, ya.shape[1]), lambda i, k: (i * nt + k, 0)),
                  pl.BlockSpec(w_out.shape, lambda i, k: (0, 0), pipeline_mode=pl.Buffered(1)),
                  mspec, mspec, mspec,
                  pl.BlockSpec((1, 1, d), lambda i, k: (0, 0, 0)),
                  pl.BlockSpec(w_router.shape, lambda i, k: (0, 0), pipeline_mode=pl.Buffered(1)),
                  pl.BlockSpec((1, LANES), lambda i, k: (0, 0)),
                  pl.BlockSpec(memory_space=pl.ANY)],
        out_specs=[xspec,
                   pl.BlockSpec((rows, d), lambda i, k: (blk0 + i * nt + k, 0)),
                   pl.BlockSpec((rows, LANES), lambda i, k: (i * nt + k, 0))],
        out_shape=[jax.ShapeDtypeStruct(x.shape, F32),
                   jax.ShapeDtypeStruct(h2_all.shape, F32),
                   jax.ShapeDtypeStruct((b * t, LANES), F32)],
        input_output_aliases={10: 1},
        compiler_params=_cparams(("arbitrary", "arbitrary")),
        name="outproj",
    )(x, ya, yb, w_out, gate, scale, shift, g_norm.reshape(1, 1, d), w_router, b_router, h2_all)


def _gather_kernel(idx_ref, used_ref, src_ref, out_ref, buf, sem, *, chunk):
    i = pl.program_id(0)

    def in_use(step):
        return step * chunk < used_ref[0]

    def issue_step(step, slot):
        def issue(r, _):
            t = idx_ref[step * chunk + r]
            pltpu.make_async_copy(src_ref.at[pl.ds(t, 1)], buf.at[slot, pl.ds(r, 1)], sem.at[slot]).start()
            return 0

        lax.fori_loop(0, chunk, issue, 0, unroll=8)

    @pl.when(i == 0)
    def _():
        issue_step(0, 0)

    @pl.when((i + 1 < pl.num_programs(0)) & in_use(i + 1))
    def _():
        issue_step(i + 1, (i + 1) % 2)

    @pl.when(in_use(i))
    def _():
        slot = i % 2
        pltpu.make_async_copy(buf.at[slot], buf.at[slot], sem.at[slot]).wait()
        out_ref[...] = buf[slot].astype(out_ref.dtype)

    @pl.when(jnp.logical_not(in_use(i)))
    def _():
        out_ref[...] = jnp.zeros(out_ref.shape, out_ref.dtype)


def _gather_rows(idx, n_used, src, out_dtype):
    n_out = idx.shape[0]
    chunk = math.gcd(GATHER_CHUNK, n_out)
    d = src.shape[1]
    return pl.pallas_call(
        functools.partial(_gather_kernel, chunk=chunk),
        grid_spec=pltpu.PrefetchScalarGridSpec(
            num_scalar_prefetch=2,
            grid=(n_out // chunk,),
            in_specs=[pl.BlockSpec(memory_space=pl.ANY)],
            out_specs=pl.BlockSpec((chunk, d), lambda i, idx, used: (i, 0)),
            scratch_shapes=[pltpu.VMEM((2, chunk, d), src.dtype), pltpu.SemaphoreType.DMA((2,))]),
        out_shape=jax.ShapeDtypeStruct((n_out, d), out_dtype),
        compiler_params=_cparams(("arbitrary",)),
        name="gather_rows",
    )(idx, n_used, src)


def _expert_up_kernel(ie_ref, ij_ref, ib_ref, ioj_ref, inew_ref, islot_ref, ine_ref, inj_ref, ihas_ref, n_ref,
                      x_ref, wg_hbm, wu_hbm, o_ref, stage, wgb_scr, wub_scr, sem, *, layer):
    i = pl.program_id(0)
    tn = wgb_scr.shape[1]

    def weight_copies(e, j, slot):
        col = pl.multiple_of(j * tn, tn)
        rows = x_ref.shape[1] // WEIGHT_DMA_CHUNKS
        copies = []
        for t, w_hbm in enumerate((wg_hbm, wu_hbm)):
            for c in range(WEIGHT_DMA_CHUNKS):
                rs = pl.ds(c * rows, rows)
                copies.append(pltpu.make_async_copy(w_hbm.at[layer, e, rs, pl.ds(col, tn)],
                                                    stage.at[slot, t, rs, :], sem.at[slot, t]))
        return copies

    @pl.when(i == 0)
    def _():
        for cp in weight_copies(ie_ref[0], ij_ref[0], islot_ref[0]):
            cp.start()

    @pl.when(i >= n_ref[0])
    def _():
        o_ref[...] = jnp.zeros(o_ref.shape, o_ref.dtype)

    @pl.when(i < n_ref[0])
    def _():
        @pl.when(inew_ref[i] == 1)
        def _():
            slot = islot_ref[i]

            @pl.when(ihas_ref[i] == 1)
            def _():
                for cp in weight_copies(ine_ref[i], inj_ref[i], 1 - slot):
                    cp.start()

            for cp in weight_copies(ie_ref[i], ij_ref[i], slot):
                cp.wait()
            a = u = None
            for k0 in range(0, x_ref.shape[1], CAST_CHUNK):
                ks = slice(k0, k0 + CAST_CHUNK)
                wg_k = stage[slot, 0, ks, :].astype(BF16)
                wu_k = stage[slot, 1, ks, :].astype(BF16)
                wgb_scr[ks, :] = wg_k
                wub_scr[ks, :] = wu_k
                a_k = jnp.dot(x_ref[:, ks], wg_k, preferred_element_type=F32)
                u_k = jnp.dot(x_ref[:, ks], wu_k, preferred_element_type=F32)
                a = a_k if a is None else a + a_k
                u = u_k if u is None else u + u_k
            o_ref[...] = (a * jax.nn.sigmoid(a) * u).astype(BF16)

        @pl.when(inew_ref[i] != 1)
        def _():
            xb = x_ref[...]
            a = jnp.dot(xb, wgb_scr[...], preferred_element_type=F32)
            u = jnp.dot(xb, wub_scr[...], preferred_element_type=F32)
            o_ref[...] = (a * jax.nn.sigmoid(a) * u).astype(BF16)


def _expert_up(plan, x_buf, w_gate, w_up, layer):
    d, de = w_gate.shape[2], w_gate.shape[3]
    n_rows = x_buf.shape[0]
    tn = de // EXPERT_COL_TILES
    bm = MOE_BLOCK_ROWS
    up = plan["up"]
    n_items = up["e"].shape[0]
    names = ("e", "j", "b", "oj", "new", "slot", "next_e", "next_j", "has_next", "n")
    return pl.pallas_call(
        functools.partial(_expert_up_kernel, layer=layer),
        grid_spec=pltpu.PrefetchScalarGridSpec(
            num_scalar_prefetch=len(names),
            grid=(n_items,),
            in_specs=[pl.BlockSpec((bm, d), lambda i, *s: (s[2][i], 0)),
                      pl.BlockSpec(memory_space=pl.ANY), pl.BlockSpec(memory_space=pl.ANY)],
            out_specs=pl.BlockSpec((bm, tn), lambda i, *s: (s[2][i], s[3][i])),
            scratch_shapes=[pltpu.VMEM((2, 2, d, tn), F32), pltpu.VMEM((d, tn), BF16), pltpu.VMEM((d, tn), BF16),
                            pltpu.SemaphoreType.DMA((2, 2))]),
        out_shape=jax.ShapeDtypeStruct((n_rows, de), BF16),
        compiler_params=_cparams(("arbitrary",)),
        name="expert_up",
    )(*[up[k] for k in names], x_buf, w_gate, w_up)


def _expert_down_kernel(be_ref, bnew_ref, bslot_ref, bne_ref, bhas_ref, bne2_ref, bhas2_ref, n_ref, h_ref, wd_hbm,
                        o_ref, stage, wdb_scr, sem, *, layer):
    i = pl.program_id(0)
    n_slots = stage.shape[0]

    def weight_copies(e, slot):
        rows = h_ref.shape[1] // WEIGHT_DMA_CHUNKS
        return [pltpu.make_async_copy(wd_hbm.at[layer, e, pl.ds(c * rows, rows), :],
                                      stage.at[slot, pl.ds(c * rows, rows), :], sem.at[slot])
                for c in range(WEIGHT_DMA_CHUNKS)]

    @pl.when(i == 0)
    def _():
        for cp in weight_copies(be_ref[0], bslot_ref[0]):
            cp.start()

        @pl.when(bhas_ref[0] == 1)
        def _():
            for cp in weight_copies(bne_ref[0], (bslot_ref[0] + 1) % n_slots):
                cp.start()

    @pl.when(i >= n_ref[0])
    def _():
        o_ref[...] = jnp.zeros(o_ref.shape, o_ref.dtype)

    @pl.when(i < n_ref[0])
    def _():
        @pl.when(bnew_ref[i] == 1)
        def _():
            slot = bslot_ref[i]

            @pl.when(bhas2_ref[i] == 1)
            def _():
                for cp in weight_copies(bne2_ref[i], (slot + 2) % n_slots):
                    cp.start()

            for cp in weight_copies(be_ref[i], slot):
                cp.wait()
            y = None
            for k0 in range(0, h_ref.shape[1], CAST_CHUNK):
                ks = slice(k0, k0 + CAST_CHUNK)
                wd_k = stage[slot, ks, :].astype(BF16)
                wdb_scr[ks, :] = wd_k
                y_k = jnp.dot(h_ref[:, ks], wd_k, preferred_element_type=F32)
                y = y_k if y is None else y + y_k
            o_ref[...] = y

        @pl.when(bnew_ref[i] != 1)
        def _():
            o_ref[...] = jnp.dot(h_ref[...], wdb_scr[...], preferred_element_type=F32)


def _expert_down(plan, h_buf, w_down, layer):
    n_rows, de = h_buf.shape
    d = w_down.shape[3]
    bm = MOE_BLOCK_ROWS
    down = plan["down"]
    n_blocks = down["e"].shape[0]
    names = ("e", "new", "slot", "next_e", "has_next", "next2_e", "has_next2", "n")
    return pl.pallas_call(
        functools.partial(_expert_down_kernel, layer=layer),
        grid_spec=pltpu.PrefetchScalarGridSpec(
            num_scalar_prefetch=len(names),
            grid=(n_blocks,),
            in_specs=[pl.BlockSpec((bm, de), lambda i, *s: (i, 0)),
                      pl.BlockSpec(memory_space=pl.ANY)],
            out_specs=pl.BlockSpec((bm, d), lambda i, *s: (i, 0)),
            scratch_shapes=[pltpu.VMEM((DOWN_STAGE_SLOTS, de, d), F32), pltpu.VMEM((de, d), BF16),
                            pltpu.SemaphoreType.DMA((DOWN_STAGE_SLOTS,))]),
        out_shape=jax.ShapeDtypeStruct((n_rows, d), F32),
        compiler_params=_cparams(("arbitrary",)),
        name="expert_down",
    )(*[down[k] for k in names], h_buf, w_down)


def _moe_plan(expert, n_experts):
    bm = MOE_BLOCK_ROWS
    n_tok = expert.shape[0]
    n_asg = n_tok * TOP_K
    e_flat = expert.reshape(-1)
    e_ids = jnp.arange(n_experts, dtype=jnp.int32)

    def take(table, idx):
        return jnp.sum(jnp.where(idx[:, None] == e_ids[None, :], table[None, :], 0), axis=1)

    hit = e_flat[:, None] == e_ids[None, :]
    csum = jnp.cumsum(hit.astype(jnp.int32), axis=0)
    counts = csum[-1]
    rank = jnp.sum(jnp.where(hit, csum, 0), axis=1) - 1
    nblk = (counts + bm - 1) // bm
    bend = jnp.cumsum(nblk)
    bstart = bend - nblk
    dest = take(bstart, e_flat) * bm + rank
    n_blocks = (n_asg + n_experts * (bm - 1) + bm - 1) // bm
    n_rows = -(-n_blocks * bm // GATHER_CHUNK) * GATHER_CHUNK
    tok = jnp.arange(n_asg, dtype=jnp.int32) // TOP_K
    row_src = (jnp.arange(n_rows, dtype=jnp.int32) % n_tok).at[dest].set(tok, unique_indices=True)
    used = bend[-1]

    def expert_of_block(b):
        return jnp.minimum(jnp.sum((bend[None, :] <= b[:, None]).astype(jnp.int32), axis=1), n_experts - 1)

    blk = jnp.arange(n_blocks, dtype=jnp.int32)
    blk_e = expert_of_block(jnp.minimum(blk, used - 1))
    blk_new = (blk == take(bstart, blk_e)).astype(jnp.int32)
    nct = EXPERT_COL_TILES
    n_items = nct * n_blocks
    idx = jnp.arange(n_items, dtype=jnp.int32)
    it = jnp.minimum(idx, nct * used - 1)
    item_e = expert_of_block(it // nct)
    item_bstart = take(bstart, item_e)
    item_nblk = take(nblk, item_e)
    off = it - nct * item_bstart
    item_j = (off // item_nblk).astype(jnp.int32)
    item_b = (item_bstart + off % item_nblk).astype(jnp.int32)
    item_new = (item_b == item_bstart).astype(jnp.int32)
    spare = idx - nct * used
    is_spare = spare >= 0
    item_b = jnp.where(is_spare, used + spare // nct, item_b)
    item_oj = jnp.where(is_spare, spare % nct, item_j)
    e_rank = jnp.cumsum((nblk > 0).astype(jnp.int32)) - 1
    e_after = expert_of_block(jnp.minimum(bend, used - 1))
    more_after = (bend < used).astype(jnp.int32)
    e_after2 = take(e_after, e_after)
    more_after2 = more_after * take(more_after, e_after)
    down = dict(e=blk_e, new=blk_new, slot=take(e_rank, blk_e) % DOWN_STAGE_SLOTS, next_e=take(e_after, blk_e),
                has_next=take(more_after, blk_e), next2_e=take(e_after2, blk_e), has_next2=take(more_after2, blk_e),
                n=used.reshape(1).astype(jnp.int32))
    not_last = item_j < nct - 1
    up = dict(e=item_e, j=item_j, b=item_b, oj=item_oj, new=item_new,
              slot=(nct * take(e_rank, item_e) + item_j) % 2,
              next_e=jnp.where(not_last, item_e, take(e_after, item_e)), next_j=jnp.where(not_last, item_j + 1, 0),
              has_next=jnp.where(not_last, 1, take(more_after, item_e)),
              n=(nct * used).reshape(1).astype(jnp.int32))
    return dict(dest=dest, row_src=row_src, up=up, down=down,
                used_rows=(used * bm).reshape(1).astype(jnp.int32))


def _combine_kernel(dst_ref, x_ref, rt_ref, g2_ref, gf_ref, y_hbm, o_ref, buf, sem, *, final_norm, tok0, n_all):
    bb, tt, _ = x_ref.shape
    rows = bb * tt
    nt = pl.num_programs(1)
    lin = pl.program_id(0) * nt + pl.program_id(1)

    def issue_step(step, slot):
        base = tok0 + step * rows

        def issue(r, _):
            for k in range(TOP_K):
                row = dst_ref[k * n_all + base + r]
                pltpu.make_async_copy(y_hbm.at[pl.ds(row, 1)], buf.at[slot, k, pl.ds(r, 1)], sem.at[slot]).start()
            return 0

        lax.fori_loop(0, rows, issue, 0, unroll=8)

    @pl.when(lin == 0)
    def _():
        issue_step(0, 0)

    @pl.when(lin + 1 < pl.num_programs(0) * nt)
    def _():
        issue_step(lin + 1, (lin + 1) % 2)

    slot = lin % 2
    pltpu.make_async_copy(buf.at[slot], buf.at[slot], sem.at[slot]).wait()
    rt = rt_ref[...]
    y = rt[:, 2:3] * buf[slot, 0] + rt[:, 3:4] * buf[slot, 1]
    x = x_ref[...] + g2_ref[...] * y.reshape(x_ref.shape)
    if final_norm:
        ms = jnp.mean(x * x, axis=-1, keepdims=True)
        x = x * lax.rsqrt(ms + EPS) * gf_ref[...]
    o_ref[...] = x


def _combine(x, y_buf, dest_t, tok0, route, gate, g_final, final_norm, max_rows):
    b, t, d = x.shape
    bb, tt = _row_tiling(b, t, max_rows)
    rows = bb * tt
    nt = t // tt
    xspec = pl.BlockSpec((bb, tt, d), lambda i, k, dst: (i, k, 0))
    kern = functools.partial(_combine_kernel, final_norm=final_norm, tok0=tok0, n_all=dest_t.shape[0] // TOP_K)
    return pl.pallas_call(
        kern,
        grid_spec=pltpu.PrefetchScalarGridSpec(
            num_scalar_prefetch=1,
            grid=(b // bb, nt),
            in_specs=[xspec,
                      pl.BlockSpec((rows, LANES), lambda i, k, dst: (i * nt + k, 0)),
                      pl.BlockSpec((bb, 1, d), lambda i, k, dst: (i, 0, 0)),
                      pl.BlockSpec((1, 1, d), lambda i, k, dst: (0, 0, 0)),
                      pl.BlockSpec(memory_space=pl.ANY)],
            out_specs=xspec,
            scratch_shapes=[pltpu.VMEM((2, TOP_K, rows, d), F32), pltpu.SemaphoreType.DMA((2,))]),
        out_shape=jax.ShapeDtypeStruct(x.shape, F32),
        compiler_params=_cparams(("arbitrary", "arbitrary")),
        name="combine",
    )(dest_t, x, route, gate, g_final.reshape(1, 1, d), y_buf)


def _layer_params(l, w):
    d = w["w_in"].shape[1]
    nh = w["w_q"].shape[1]
    da = w["ln_v_g"].shape[1]
    db = w["w_conv"].shape[2]
    n_main = 2 * da + 2 * db
    w_in = w["w_in"][l]
    w_gate = jnp.zeros((d, 2 * LANES), BF16)
    w_gate = w_gate.at[:, :nh].set(w_in[:, n_main:n_main + nh].astype(BF16))
    w_gate = w_gate.at[:, LANES:LANES + nh].set(w_in[:, n_main + nh:n_main + 2 * nh].astype(BF16))
    n_groups = w["w_r1"].shape[2]
    per_group = w["w_r2"].shape[3]
    w_r = jnp.concatenate([w["w_r1"][l], jnp.transpose(w["w_r2"][l], (1, 0, 2)).reshape(d, n_groups * per_group)],
                          axis=1)
    n_r = w_r.shape[1]
    w_r = jnp.pad(w_r, ((0, 0), (0, LANES - n_r)))
    b_r =jnp.pad(jnp.concatenate([w["b_r1"][l], w["b_r2"][l].reshape(-1)]), (0, LANES - n_r)).reshape(1, LANES)
    return dict(
        g_norm1=w["g_norm1"][l], g_norm2=w["g_norm2"][l],
        w_main=w_in.astype(BF16), n_main=n_main, w_gate=w_gate,
        ln_g=w["ln_v_g"][l], ln_b=w["ln_v_b"][l],
        w_conv=w["w_conv"][l], b_conv=w["b_conv"][l].reshape(1, db),
        w_q=w["w_q"][l].astype(BF16), w_k=w["w_k"][l].astype(BF16),
        w_kt=jnp.swapaxes(w["w_k"][l], 1, 2).astype(BF16), w_v=w["w_v"][l].astype(BF16),
        b_i=jnp.pad(w["b_i"][l], (0, LANES - nh)).reshape(1, LANES),
        b_f=jnp.pad(w["b_f"][l], (0, LANES - nh)).reshape(1, LANES),
        g_mh=w["g_mh"][l].reshape(1, db), k_scale=1.0 / math.sqrt(LANES),
        w_out=w["w_out"][l].astype(BF16),
        w_router=w_r.astype(BF16), b_router=b_r,
        n_groups=n_groups, per_group=per_group,
    )


def _mix_weights(w_spatial, b_spatial, L, da):
    nh = w_spatial.shape[0]
    causal = jnp.tril(jnp.ones((L, L), dtype=bool))
    w = jnp.where(causal[None], w_spatial[:, :L, :L], 0.0)
    reps = TILE // L
    if reps > 1:
        eye = jnp.eye(reps, dtype=F32)
        w = jnp.einsum("ab,hts->hatbs", eye, w).reshape(nh, TILE, TILE)
    bias = jnp.tile(jnp.transpose(b_spatial[:, :L]), (reps, 1))
    bias_full = jnp.repeat(bias, da // nh, axis=1)
    return w.astype(BF16), bias_full


def kernel(x_prompt, x_sample, c_prompt, c_sample, state_mlstm_C, state_mlstm_n, state_mlstm_m, state_conv, w_ada, b_ada, g_norm1, g_norm2, w_in, ln_v_g, ln_v_b, w_spatial, b_spatial, w_conv, b_conv, w_q, w_k, w_v, b_i, b_f, g_mh, w_out, w_r1, b_r1, w_r2, b_r2, w_e_gate, w_e_up, w_e_down, g_final):
    weights = dict(w_in=w_in, ln_v_g=ln_v_g, ln_v_b=ln_v_b, w_conv=w_conv, b_conv=b_conv, w_q=w_q, w_k=w_k,
                   w_v=w_v, b_i=b_i, b_f=b_f, g_mh=g_mh, w_out=w_out, w_r1=w_r1, b_r1=b_r1, w_r2=w_r2, b_r2=b_r2,
                   g_norm1=g_norm1, g_norm2=g_norm2)
    depth = w_in.shape[0]
    bp, tp, d = x_prompt.shape
    bs, ts, _ = x_sample.shape
    nh = w_q.shape[1]
    da = ln_v_g.shape[1]
    db = w_conv.shape[2]
    n_mod = w_ada.shape[2] // d
    n_experts = w_e_gate.shape[1]
    n_p, n_s = bp * tp, bs * ts
    assert da == db and da % LANES == 0 and nh * LANES == db and tp % TILE == 0 and TILE % ts == 0
    assert ts >= CONV_W - 1 and ts % SUBLANES == 0 and n_s % TILE == 0
    xm_col, zo_col = 2 * da // db, 2 * da // db + 1

    mod = _ada(jnp.concatenate([c_prompt, c_sample], axis=0), w_ada, b_ada).reshape(depth, bp + bs, n_mod, 1, d)

    xp, xs = x_prompt, x_sample
    h2_all = jnp.zeros((n_p + n_s, d), F32)
    outs = {k: [] for k in ("cp", "np", "mp", "bp", "cs", "ns", "ms", "bs", "vs")}
    for l in range(depth):
        p = _layer_params(l, weights)
        mods_p = [mod[l, :bp, i] for i in range(n_mod)]
        mods_s = [mod[l, bp:, i] for i in range(n_mod)]

        proj_p, gates_p = _inproj(xp, mods_p[1], mods_p[0], p["g_norm1"], p["w_main"], p["n_main"], p["w_gate"], 1024)
        w_mix_p, bias_p = _mix_weights(w_spatial[l], b_spatial[l], min(tp, TILE), da)
        (ya_p,) = _grpa(proj_p, p["ln_g"], p["ln_b"], w_mix_p, bias_p, da, False)
        yb_p, c_p, n_pn, m_p = _mlstm_prompt(proj_p, gates_p, p, bp, tp, db, nh, xm_col, zo_col)
        outs["cp"].append(c_p)
        outs["np"].append(n_pn.reshape(bp, nh, LANES))
        outs["mp"].append(m_p[:, 0, :nh])
        outs["bp"].append(proj_p.reshape(bp, tp, -1)[:, tp - (CONV_W - 1):, 2 * da:2 * da + db])

        proj_s, gates_s = _inproj(xs, mods_s[1], mods_s[0], p["g_norm1"], p["w_main"], p["n_main"], p["w_gate"], 512)
        w_mix_s, bias_s = _mix_weights(w_spatial[l], b_spatial[l], min(ts, TILE), da)
        ya_s, v_s = _grpa(proj_s, p["ln_g"], p["ln_b"], w_mix_s, bias_s, da, True)
        prev_rows = jnp.pad(state_conv[l], ((0, 0), (ts - (CONV_W - 1), 0), (0, 0))).reshape(n_s, db)
        n0_rows = jnp.repeat(state_mlstm_n[l].reshape(bs, db), ts, axis=0)
        m0_rows = jnp.pad(jnp.repeat(state_mlstm_m[l], ts, axis=0), ((0, 0), (0, LANES - nh)))
        yb_s, c_s, n_rows_s, m_rows_s = _mlstm_sample(proj_s, gates_s, prev_rows, state_mlstm_C, l, n0_rows, m0_rows,
                                                      p, math.gcd(ts, TILE), db, nh, xm_col, zo_col)
        outs["cs"].append(c_s)
        outs["ns"].append(n_rows_s.reshape(bs, ts, nh, LANES)[:, ts - 1])
        outs["ms"].append(m_rows_s.reshape(bs, ts, LANES)[:, ts - 1, :nh])
        outs["bs"].append(proj_s.reshape(bs, ts, -1)[:, ts - (CONV_W - 1):, 2 * da:2 * da + db])
        outs["vs"].append(v_s.reshape(bs, ts, da))

        rargs = (p["w_router"], p["b_router"], p["n_groups"], p["per_group"])
        xp, h2_all, rt_p = _outproj(xp, ya_p, yb_p, p["w_out"], mods_p[2], mods_p[4], mods_p[3], p["g_norm2"], *rargs,
                                    h2_all, 0, 512)
        xs, h2_all, rt_s = _outproj(xs, ya_s, yb_s, p["w_out"], mods_s[2], mods_s[4], mods_s[3], p["g_norm2"], *rargs,
                                    h2_all, n_p, 256)

        route = jnp.concatenate([rt_p, rt_s], axis=0)
        plan = _moe_plan(route[:, :TOP_K].astype(jnp.int32), n_experts)
        x_buf = _gather_rows(plan["row_src"], plan["used_rows"], h2_all, BF16)
        h_buf = _expert_up(plan, x_buf, w_e_gate, w_e_up, l)
        y_buf = _expert_down(plan, h_buf, w_e_down, l)
        dest_t = jnp.transpose(plan["dest"].reshape(n_p + n_s, TOP_K)).reshape(-1)
        last = l == depth - 1
        xp = _combine(xp, y_buf, dest_t, 0, rt_p, mods_p[5], g_final, last, 512)
        xs = _combine(xs, y_buf, dest_t, n_p, rt_s, mods_s[5], g_final, last, 256)

    st = jnp.stack
    return (xp, xs, st(outs["cp"]), st(outs["np"]), st(outs["mp"]), st(outs["bp"]),
            st(outs["cs"]), st(outs["ns"]), st(outs["ms"]), st(outs["bs"]), st(outs["vs"]))
```
